```python
import jax, jax.numpy as jnp
from jax import lax
import numpy as np

D_MODEL = 1024
BATCH = 4
SEQ = 8192
DEPTH = 1

CTX_LEN = 256
GRID_W = 64
M_HEADS = 4
M_DK = 128
M_DV = 128
M_QK = M_HEADS * M_DK
M_WIDTH = M_HEADS * M_DV
M_CHUNK = 128
CONV_K = 3
G_HEADS = 4
G_DK = 64
G_DV = 128
G_QK = G_HEADS * G_DK
G_WIDTH = G_HEADS * G_DV
G_RANK = 16
G_TAU = 16.0
G_CHUNK = 64
MIX_WIDTH = M_WIDTH + G_WIDTH
IN_COLS = 2 * M_QK + 2 * M_WIDTH + 4 * M_HEADS + 2 * G_QK + 2 * G_WIDTH + 2 * G_RANK
D_FF = ((8 * D_MODEL // 3 + 255) // 256) * 256
EPS = 1e-6
NEG = -1e30

kernel_name = "hymba_mlstm_gla_prefix_block"


def rmsnorm(x, g):
    xf = x.astype(jnp.float32)
    y = xf * lax.rsqrt(jnp.mean(xf * xf, -1, keepdims=True) + EPS)
    return (y * g.astype(jnp.float32)).astype(x.dtype)


def head_rmsnorm(h, g):
    b, nh, t, d = h.shape
    y = h * lax.rsqrt(jnp.mean(h * h, -1, keepdims=True) + EPS)
    y = jnp.transpose(y, (0, 2, 1, 3)).reshape(b, t, nh * d)
    return y * g.astype(jnp.float32)


def modulate(h, shift, scale):
    return h * (1 + scale) + shift


def heads(a, nh):
    b, t, w = a.shape
    return a.reshape(b, t, nh, w // nh).transpose(0, 2, 1, 3).astype(jnp.float32)


def flip(a):
    return jnp.flip(a, 2)


def dwconv2d(u, w, rows, cols):
    b, n, ch = u.shape
    img = u.reshape(b, rows, cols, ch)
    out = lax.conv_general_dilated(img, w[:, :, None, :].astype(u.dtype), window_strides=(1, 1), padding='SAME',
                                   dimension_numbers=('NHWC', 'HWIO', 'NHWC'), feature_group_count=ch)
    return out.reshape(b, n, ch)


def to_chunks(a, chunk):
    b, h, t = a.shape[:3]
    a = a.reshape(b, h, t // chunk, chunk, *a.shape[3:])
    return jnp.moveaxis(a, 2, 0)


def from_chunks(a):
    a = jnp.moveaxis(a, 0, 2)
    return a.reshape(a.shape[0], a.shape[1], -1, a.shape[-1])


def mlstm_scan(q, k, v, logi, logf, state):
    xs = tuple(to_chunks(a, M_CHUNK) for a in (q, k, v, logi, logf))
    lower = jnp.tril(jnp.ones((M_CHUNK, M_CHUNK), bool))

    def step(carry, inp):
        C, nv, m = carry
        qc, kc, vc, ic, fc = inp
        bcum = jnp.cumsum(fc, -1)
        dmat = jnp.where(lower, bcum[..., :, None] - bcum[..., None, :] + ic[..., None, :], NEG)
        inter = m[..., None] + bcum
        m_t = jnp.maximum(inter, jnp.max(dmat, -1))
        w_inter = jnp.exp(inter - m_t)
        scores = jnp.einsum('bhtd,bhsd->bhts', qc, kc) * jnp.exp(dmat - m_t[..., None])
        num = w_inter[..., None] * jnp.einsum('bhtd,bhde->bhte', qc, C) + jnp.einsum('bhts,bhse->bhte', scores, vc)
        den = w_inter * jnp.einsum('bhtd,bhd->bht', qc, nv) + jnp.sum(scores, -1)
        hout = num / jnp.maximum(jnp.abs(den), jnp.exp(-m_t))[..., None]
        btot = bcum[..., -1]
        src = btot[..., None] - bcum + ic
        m_new = jnp.maximum(m + btot, jnp.max(src, -1))
        ws = jnp.exp(src - m_new[..., None])
        wc = jnp.exp(m + btot - m_new)
        C_new = wc[..., None, None] * C + jnp.einsum('bhs,bhsd,bhse->bhde', ws, kc, vc)
        n_new = wc[..., None] * nv + jnp.einsum('bhs,bhsd->bhd', ws, kc)
        return (C_new, n_new, m_new), hout

    state, hs = lax.scan(step, state, xs)
    return from_chunks(hs), state


def mlstm_state(k, v, logi, logf):
    bcum = jnp.cumsum(logf, -1)
    src = bcum[..., -1:] - bcum + logi
    m = jnp.max(src, -1)
    w = jnp.exp(src - m[..., None])
    return (jnp.einsum('bhs,bhsd,bhse->bhde', w, k, v), jnp.einsum('bhs,bhsd->bhd', w, k), m)


def gla_scan(q, k, v, loga, S):
    xs = tuple(to_chunks(a, G_CHUNK) for a in (q, k, v, loga))
    lower = jnp.tril(jnp.ones((G_CHUNK, G_CHUNK), bool))[..., None]

    def step(S, inp):
        qc, kc, vc, ac = inp
        bcum = jnp.cumsum(ac, -2)
        decay = jnp.where(lower, bcum[..., :, None, :] - bcum[..., None, :, :], NEG)
        att = jnp.einsum('bhtk,bhsk,bhtsk->bhts', qc, kc, jnp.exp(decay))
        out = jnp.einsum('bhtk,bhkv->bhtv', qc * jnp.exp(bcum), S) + jnp.einsum('bhts,bhsv->bhtv', att, vc)
        btot = bcum[..., -1:, :]
        S_new = jnp.exp(btot[..., 0, :])[..., None] * S + jnp.einsum('bhsk,bhsv->bhkv', kc * jnp.exp(btot - bcum), vc)
        return S_new, out

    S, outs = lax.scan(step, S, xs)
    return from_chunks(outs), S


def gla_state(k, v, loga):
    bcum = jnp.cumsum(loga, -2)
    return jnp.einsum('bhsk,bhsv->bhkv', k * jnp.exp(bcum[..., -1:, :] - bcum), v)


def mixer_features(h, w_in, conv_w, m_gate_b, g_gate_w, g_gate_b, rows, cols):
    z = h @ w_in
    sizes = (2 * M_QK, M_WIDTH, M_WIDTH, 4 * M_HEADS, G_QK, G_QK, G_WIDTH, G_WIDTH, 2 * G_RANK)
    parts, off = [], 0
    for s in sizes:
        parts.append(z[..., off:off + s])
        off += s
    mqk, mv, mo, mg, gq, gk, gv, gr, glr = parts
    mqk = jax.nn.silu(dwconv2d(mqk, conv_w, rows, cols))
    mq, mk = jnp.split(mqk, 2, -1)
    b, t, _ = h.shape
    mg = (mg + m_gate_b).astype(jnp.float32).reshape(b, t, 4, M_HEADS).transpose(2, 0, 3, 1)
    loga = [jax.nn.log_sigmoid((glr[..., i * G_RANK:(i + 1) * G_RANK] @ g_gate_w[i] + g_gate_b[i]).astype(jnp.float32)) / G_TAU
            for i in range(2)]
    return dict(
        mq=heads(mq, M_HEADS), mk=heads(mk, M_HEADS) * (M_DK ** -0.5), mv=heads(mv, M_HEADS), mo=mo,
        mi_f=mg[0], mf_f=jax.nn.log_sigmoid(mg[1]), mi_b=mg[2], mf_b=jax.nn.log_sigmoid(mg[3]),
        gq=heads(gq, G_HEADS) * (G_DK ** -0.5), gk=heads(gk, G_HEADS), gv=heads(gv, G_HEADS), gr=gr,
        ga_f=heads(loga[0], G_HEADS), ga_b=heads(loga[1], G_HEADS))


def zero_states(b):
    f32 = jnp.float32
    ms = (jnp.zeros((b, M_HEADS, M_DK, M_DV), f32), jnp.zeros((b, M_HEADS, M_DK), f32), jnp.full((b, M_HEADS), NEG, f32))
    gs = jnp.zeros((b, G_HEADS, G_DK, G_DV), f32)
    return (ms, ms, gs, gs)


def context_states(f):
    return (mlstm_state(f['mk'], f['mv'], f['mi_f'], f['mf_f']),
            mlstm_state(flip(f['mk']), flip(f['mv']), flip(f['mi_b']), flip(f['mf_b'])),
            gla_state(f['gk'], f['gv'], f['ga_f']),
            gla_state(flip(f['gk']), flip(f['gv']), flip(f['ga_b'])))


def mixer_apply(f, states, m_norm_g, g_norm_g, w_out):
    smf, smb, sgf, sgb = states
    hf, smf = mlstm_scan(f['mq'], f['mk'], f['mv'], f['mi_f'], f['mf_f'], smf)
    hb, smb = mlstm_scan(flip(f['mq']), flip(f['mk']), flip(f['mv']), flip(f['mi_b']), flip(f['mf_b']), smb)
    hm = head_rmsnorm(hf + flip(hb), m_norm_g) * jax.nn.sigmoid(f['mo'].astype(jnp.float32))
    of, sgf = gla_scan(f['gq'], f['gk'], f['gv'], f['ga_f'], sgf)
    ob, sgb = gla_scan(flip(f['gq']), flip(f['gk']), flip(f['gv']), flip(f['ga_b']), sgb)
    hg = head_rmsnorm(of + flip(ob), g_norm_g) * jax.nn.silu(f['gr'].astype(jnp.float32))
    out = jnp.concatenate([hm, hg], -1).astype(w_out.dtype) @ w_out
    return out, (smf, smb, sgf, sgb)


def swiglu(h, w_gu, w_down):
    a, g = jnp.split(h @ w_gu, 2, -1)
    return (jax.nn.silu(a) * g) @ w_down


def setup_inputs(seed: int = 0) -> dict:
    key = jax.random.key(seed)
    ks = jax.random.split(key, 24)
    f32 = jnp.float32
    D = D_MODEL

    def nrm(k, shape, s):
        return s * jax.random.normal(k, shape, f32)

    fb = jnp.linspace(3.0, 6.0, M_HEADS, dtype=f32)
    zb = jnp.zeros((M_HEADS,), f32)
    m_gate_base = jnp.concatenate([zb, fb, zb, fb])
    return {
        "x": nrm(ks[0], (BATCH, SEQ, D), 1.0),
        "c": nrm(ks[1], (BATCH, D), 1.0),
        "ctx": nrm(ks[2], (BATCH, CTX_LEN, D), 1.0),
        "c_ctx": nrm(ks[3], (D,), 1.0),
        "w_ada": nrm(ks[4], (DEPTH, D, 6 * D), 0.5 * D ** -0.5),
        "b_ada": nrm(ks[5], (DEPTH, 6 * D), 0.01),
        "g_mix": 1.0 + nrm(ks[6], (DEPTH, D), 0.02),
        "w_in": nrm(ks[7], (DEPTH, D, IN_COLS), D ** -0.5),
        "conv_w": nrm(ks[8], (DEPTH, CONV_K, CONV_K, 2 * M_QK), 1.0 / CONV_K),
        "m_gate_b": m_gate_base + nrm(ks[9], (DEPTH, 4 * M_HEADS), 0.1),
        "m_norm_g": 1.0 + nrm(ks[10], (DEPTH, M_WIDTH), 0.02),
        "g_gate_w": nrm(ks[11], (DEPTH, 2, G_RANK, G_QK), G_RANK ** -0.5),
        "g_gate_b": nrm(ks[12], (DEPTH, 2, G_QK), 0.01),
        "g_norm_g": 1.0 + nrm(ks[13], (DEPTH, G_WIDTH), 0.02),
        "w_out": nrm(ks[14], (DEPTH, MIX_WIDTH, D), MIX_WIDTH ** -0.5),
        "g_ffn": 1.0 + nrm(ks[15], (DEPTH, D), 0.02),
        "w_gu": nrm(ks[16], (DEPTH, D, 2 * D_FF), D ** -0.5),
        "w_down": nrm(ks[17], (DEPTH, D_FF, D), D_FF ** -0.5),
        "g_final": 1.0 + nrm(ks[18], (D,), 0.02),
    }


def reference(x, c, ctx, c_ctx, w_ada, b_ada, g_mix, w_in, conv_w, m_gate_b, m_norm_g, g_gate_w, g_gate_b,
              g_norm_g, w_out, g_ffn, w_gu, w_down, g_final):
    rows = x.shape[1] // GRID_W
    h_ctx = ctx
    for l in range(DEPTH):
        last = l == DEPTH - 1
        mod_x = (jax.nn.silu(c) @ w_ada[l] + b_ada[l])[:, None, :]
        mod_c = jax.nn.silu(c_ctx) @ w_ada[l] + b_ada[l]
        shx1, scx1, gx1, shx2, scx2, gx2 = jnp.split(mod_x, 6, -1)
        shc1, scc1, gc1, shc2, scc2, gc2 = jnp.split(mod_c, 6, -1)
        proj = (w_in[l], conv_w[l], m_gate_b[l], g_gate_w[l], g_gate_b[l])
        fc = mixer_features(modulate(rmsnorm(h_ctx, g_mix[l]), shc1, scc1), *proj, 1, h_ctx.shape[1])
        fx = mixer_features(modulate(rmsnorm(x, g_mix[l]), shx1, scx1), *proj, rows, GRID_W)
        if last:
            ctx_st = context_states(fc)
        else:
            oc, ctx_st = mixer_apply(fc, zero_states(h_ctx.shape[0]), m_norm_g[l], g_norm_g[l], w_out[l])
            h_ctx = h_ctx + gc1 * oc
            h_ctx = h_ctx + gc2 * swiglu(modulate(rmsnorm(h_ctx, g_ffn[l]), shc2, scc2), w_gu[l], w_down[l])
        ox, _ = mixer_apply(fx, ctx_st, m_norm_g[l], g_norm_g[l], w_out[l])
        x = x + gx1 * ox
        x = x + gx2 * swiglu(modulate(rmsnorm(x, g_ffn[l]), shx2, scx2), w_gu[l], w_down[l])
    return rmsnorm(x, g_final)
```

```python
import functools

import numpy as np
import jax
import jax.numpy as jnp
from jax import lax
from jax.experimental import pallas as pl
from jax.experimental.pallas import tpu as pltpu

F32 = jnp.float32
BF16 = jnp.bfloat16

GRID_W = 64
M_HEADS = 4
M_DK = 128
M_DV = 128
M_QK = M_HEADS * M_DK
M_WIDTH = M_HEADS * M_DV
M_CHUNK = 128
G_HEADS = 4
G_DK = 64
G_DV = 128
G_QK = G_HEADS * G_DK
G_WIDTH = G_HEADS * G_DV
G_RANK = 16
G_TAU = 16.0
G_CHUNK = 64
EPS = 1e-6
NEG = -1e30

LANE = 128
VMEM_LIMIT = 56 * 1024 * 1024


def _params(*sem):
    return pltpu.CompilerParams(dimension_semantics=sem, vmem_limit_bytes=VMEM_LIMIT)


def _sigmoid(x):
    return 1.0 / (1.0 + jnp.exp(-x))


def _log_sigmoid(x):
    return jnp.minimum(x, 0.0) - jnp.log1p(jnp.exp(-jnp.abs(x)))


def _split3(x):
    x1 = x.astype(BF16)
    r1 = x - x1.astype(F32)
    x2 = r1.astype(BF16)
    x3 = (r1 - x2.astype(F32)).astype(BF16)
    return x1, x2, x3


def _dot(a, b):
    return jnp.dot(a, b, preferred_element_type=F32)


def _dot_nt(a, b):
    return lax.dot_general(a, b, (((1,), (1,)), ((), ())), preferred_element_type=F32)


def _dot_tn(a, b):
    return lax.dot_general(a, b, (((0,), (0,)), ((), ())), preferred_element_type=F32)


def _sel_left(m01, x):
    x1, x2, x3 = _split3(x)
    return _dot(m01, x1) + _dot(m01, x2) + _dot(m01, x3)


def _sel_right(x, m01):
    x1, x2, x3 = _split3(x)
    return _dot(x1, m01) + _dot(x2, m01) + _dot(x3, m01)


def _mod_kernel(c_ref, w_ref, b_ref, o_ref):
    cs = c_ref[...]
    a = (cs * _sigmoid(cs)).astype(BF16)
    o_ref[...] = _dot(a, w_ref[...].astype(BF16)) + b_ref[...]


def _modulation(c_all, w_ada, b_ada):
    rows, d = c_all.shape
    n = w_ada.shape[1]
    bn = n // 6
    return pl.pallas_call(
        _mod_kernel,
        grid=(n // bn,),
        in_specs=[pl.BlockSpec((rows, d), lambda j: (0, 0)),
                  pl.BlockSpec((d, bn), lambda j: (0, j)),
                  pl.BlockSpec((1, bn), lambda j: (0, j))],
        out_specs=pl.BlockSpec((rows, bn), lambda j: (0, j)),
        out_shape=jax.ShapeDtypeStruct((rows, n), F32),
        compiler_params=_params("arbitrary"),
        name="adaln_mod",
    )(c_all, w_ada, b_ada.reshape(1, n))


_MAIN_SIZES = (("mqk", 2 * M_QK), ("mv", M_WIDTH), ("mo", M_WIDTH), ("gq", G_QK),
               ("gk", G_QK), ("gv", G_WIDTH), ("gr", G_WIDTH))
_MAIN_COLS = sum(s for _, s in _MAIN_SIZES)
N_GATES = 4 * M_HEADS
SMALL_COLS = LANE


def _feat_kernel(x_ref, sh_ref, sc_ref, g_ref, wmain_ref, wsmall_ref, wgt_ref, bsm_ref, bgt_ref,
                 mqk_ref, mv_ref, mo_ref, gq_ref, gk_ref, gv_ref, gr_ref, small_ref, gt_ref):
    x = x_ref[0]
    ms = jnp.mean(x * x, -1, keepdims=True)
    h = x * lax.rsqrt(ms + EPS) * g_ref[...]
    h = h * (1.0 + sc_ref[0]) + sh_ref[0]
    hb = h.astype(BF16)
    outs = dict(mqk=mqk_ref, mv=mv_ref, mo=mo_ref, gq=gq_ref, gk=gk_ref, gv=gv_ref, gr=gr_ref)
    off = 0
    for name, size in _MAIN_SIZES:
        z = _dot(hb, wmain_ref[:, off:off + size])
        if name == "gq":
            z = z * (G_DK ** -0.5)
        outs[name][0] = z.astype(BF16)
        off += size
    small_ref[0] = _dot(hb, wsmall_ref[...]) + bsm_ref[...]
    gt_ref[0] = _dot_nt(wgt_ref[...], hb) + bgt_ref[...]


def _features(x, shift, scale, g, wmain, wsmall, wgt, bsm, bgt, tm):
    b, t, d = x.shape
    nt = t // tm
    tok = lambda w: pl.BlockSpec((1, tm, w), lambda i, j: (i, j, 0))
    const = lambda a: pl.BlockSpec(a.shape, lambda i, j: (0,) * a.ndim)
    vec = pl.BlockSpec((1, 1, d), lambda i, j: (i, 0, 0))
    out_shape = [jax.ShapeDtypeStruct((b, t, s), BF16) for _, s in _MAIN_SIZES]
    out_specs = [tok(s) for _, s in _MAIN_SIZES]
    out_shape += [jax.ShapeDtypeStruct((b, t, SMALL_COLS), F32), jax.ShapeDtypeStruct((b, N_GATES, t), F32)]
    out_specs += [tok(SMALL_COLS), pl.BlockSpec((1, N_GATES, tm), lambda i, j: (i, 0, j))]
    return pl.pallas_call(
        _feat_kernel,
        grid=(b, nt),
        in_specs=[tok(d), vec, vec, const(g), const(wmain), const(wsmall), const(wgt), const(bsm), const(bgt)],
        out_specs=out_specs,
        out_shape=out_shape,
        compiler_params=_params("parallel", "arbitrary"),
        name="features",
    )(x, shift, scale, g, wmain, wsmall, wgt, bsm, bgt)


def _conv_kernel(cols, nrow, prev_ref, main_ref, next_ref, w_ref, ksc_ref, o_ref):
    i = pl.program_id(1)
    n = pl.num_programs(1)
    main = main_ref[0].astype(F32)
    prev = jnp.where(i > 0, prev_ref[0].astype(F32), 0.0)
    nxt = jnp.where(i < n - 1, next_ref[0].astype(F32), 0.0)
    ext = jnp.concatenate([prev, main, nxt], 0)
    tot = (nrow + 2) * cols
    col = lax.broadcasted_iota(jnp.int32, ext.shape, 0) % cols
    left = jnp.where(col == 0, 0.0, pltpu.roll(ext, 1, 0))
    right = jnp.where(col == cols - 1, 0.0, pltpu.roll(ext, tot - 1, 0))
    srcs = (left, ext, right)
    w = w_ref[...]
    acc = None
    for di in range(3):
        for dj in range(3):
            term = srcs[dj][di * cols:(di + nrow) * cols] * w[di * 3 + dj:di * 3 + dj + 1]
            acc = term if acc is None else acc + term
    o_ref[0] = (acc * _sigmoid(acc) * ksc_ref[...]).astype(BF16)


def _conv(mqk_pre, conv_w9, kscale, cols, nrow, cb):
    b, t, ch = mqk_pre.shape
    rows = t // cols
    nblk = rows // nrow
    main = pl.BlockSpec((1, nrow * cols, cb), lambda i, j, c: (i, j, c))
    prev = pl.BlockSpec((1, cols, cb), lambda i, j, c: (i, jnp.maximum(j * nrow - 1, 0), c))
    nxt = pl.BlockSpec((1, cols, cb), lambda i, j, c: (i, jnp.minimum((j + 1) * nrow, rows - 1), c))
    return pl.pallas_call(
        functools.partial(_conv_kernel, cols, nrow),
        grid=(b, nblk, ch // cb),
        in_specs=[prev, main, nxt,
                  pl.BlockSpec((9, cb), lambda i, j, c: (0, c)),
                  pl.BlockSpec((1, cb), lambda i, j, c: (0, c))],
        out_specs=main,
        out_shape=jax.ShapeDtypeStruct((b, t, ch), BF16),
        compiler_params=_params("parallel", "arbitrary", "arbitrary"),
        name="dwconv_silu",
    )(mqk_pre, mqk_pre, mqk_pre, conv_w9, kscale)


def _tri(n):
    low = np.tril(np.ones((n, n), np.float32))
    return jnp.asarray(low, BF16), jnp.asarray(low.T, BF16)


def _mlstm_kernel(emit, qf_ref, kf_ref, vf_ref, gcf_ref, grf_ref, qb_ref, kb_ref, vb_ref, gcb_ref, grb_ref,
                  low_ref, up_ref, c0_ref, m0_ref, *rest):
    if emit:
        hf_ref, hb_ref, cout_ref, mout_ref, c_s, m_s = rest
    else:
        cout_ref, mout_ref, c_s, m_s = rest
        hf_ref = hb_ref = None
    j = pl.program_id(1)
    L = M_CHUNK

    @pl.when(j == 0)
    def _():
        c_s[...] = c0_ref[0]
        m_s[...] = m0_ref[0]

    low = low_ref[...]
    up = up_ref[...]
    rid = lax.broadcasted_iota(jnp.int32, (L, L), 0)
    cid = lax.broadcasted_iota(jnp.int32, (L, L), 1)
    ones_col = jnp.where(lax.broadcasted_iota(jnp.int32, (L, LANE), 1) == 0, 1.0, 0.0).astype(BF16)

    for d in range(2):
        q_ref, k_ref, v_ref, gc_ref, gr_ref, h_ref = (
            (qf_ref, kf_ref, vf_ref, gcf_ref, grf_ref, hf_ref) if d == 0 else
            (qb_ref, kb_ref, vb_ref, gcb_ref, grb_ref, hb_ref))
        mask = (cid <= rid) if d == 0 else (cid >= rid)
        gcol = gc_ref[0][:, :N_GATES]
        grow = gr_ref[0]
        cum_col = _sel_left(low if d == 0 else up, _log_sigmoid(gcol))
        cum_row = _sel_right(_log_sigmoid(grow), up if d == 0 else low)
        for hd in range(M_HEADS):
            ci = 2 * M_HEADS * d + hd
            cf = ci + M_HEADS
            sl = slice(hd * M_DK, (hd + 1) * M_DK)
            r = d * M_HEADS + hd
            q = q_ref[0][:, sl]
            k = k_ref[0][:, sl]
            v_aug = jnp.concatenate([v_ref[0][:, sl], ones_col], 1)
            bc = cum_col[:, cf:cf + 1]
            br = cum_row[cf:cf + 1, :]
            ic = gcol[:, ci:ci + 1]
            ir = grow[ci:ci + 1, :]
            btot = bc[L - 1:L, :] if d == 0 else bc[0:1, :]
            m = m_s[r:r + 1, 0:1]
            c_aug = c_s[r]
            if emit:
                dm = jnp.where(mask, bc - br + ir, NEG)
                inter = m + bc
                mt = jnp.maximum(inter, jnp.max(dm, -1, keepdims=True))
                w_inter = jnp.exp(inter - mt)
                p = _dot_nt(q, k) * jnp.exp(dm - mt)
                num = w_inter * _dot(q, c_aug.astype(BF16)) + _dot(p.astype(BF16), v_aug)
                den = num[:, M_DV:M_DV + 1]
                h_ref[0, :, sl] = num[:, :M_DV] / jnp.maximum(jnp.abs(den), jnp.exp(-mt))
            src = btot - bc + ic
            m_new = jnp.maximum(m + btot, jnp.max(src, 0, keepdims=True))
            ws = jnp.exp(src - m_new)
            wc = jnp.exp(m + btot - m_new)
            kw = (k.astype(F32) * ws).astype(BF16)
            c_s[r] = wc * c_aug + _dot_tn(kw, v_aug)
            m_s[r:r + 1, :] = jnp.broadcast_to(m_new, (1, LANE))

    @pl.when(j == pl.num_programs(1) - 1)
    def _():
        cout_ref[0] = c_s[...]
        mout_ref[0] = m_s[...]


def _mlstm_scan(mqk, mv, small, gt, c0, m0, emit):
    b, t, _ = mv.shape
    L = M_CHUNK
    nc = t // L
    low, up = _tri(L)
    fw = lambda w, cblk: pl.BlockSpec((1, L, w), lambda i, j: (i, j, cblk))
    bw = lambda w, cblk: pl.BlockSpec((1, L, w), lambda i, j: (i, nc - 1 - j, cblk))
    gtf = pl.BlockSpec((1, N_GATES, L), lambda i, j: (i, 0, j))
    gtb = pl.BlockSpec((1, N_GATES, L), lambda i, j: (i, 0, nc - 1 - j))
    tri = pl.BlockSpec((L, L), lambda i, j: (0, 0))
    cst = pl.BlockSpec((1, 2 * M_HEADS, M_DK, 2 * M_DV), lambda i, j: (i, 0, 0, 0))
    mst = pl.BlockSpec((1, 2 * M_HEADS, LANE), lambda i, j: (i, 0, 0))
    in_specs = [fw(M_QK, 0), fw(M_QK, 1), fw(M_WIDTH, 0), fw(SMALL_COLS, 0), gtf,
                bw(M_QK, 0), bw(M_QK, 1), bw(M_WIDTH, 0), bw(SMALL_COLS, 0), gtb,
                tri, tri, cst, mst]
    out_specs = [cst, mst]
    out_shape = [jax.ShapeDtypeStruct(c0.shape, F32), jax.ShapeDtypeStruct(m0.shape, F32)]
    if emit:
        out_specs = [fw(M_WIDTH, 0), bw(M_WIDTH, 0)] + out_specs
        out_shape = [jax.ShapeDtypeStruct((b, t, M_WIDTH), F32)] * 2 + out_shape
    return pl.pallas_call(
        functools.partial(_mlstm_kernel, emit),
        grid=(b, nc),
        in_specs=in_specs,
        out_specs=out_specs,
        out_shape=out_shape,
        scratch_shapes=[pltpu.VMEM((2 * M_HEADS, M_DK, 2 * M_DV), F32), pltpu.VMEM((2 * M_HEADS, LANE), F32)],
        compiler_params=_params("parallel", "arbitrary"),
        name="mlstm_scan" if emit else "mlstm_ctx_state",
    )(mqk, mqk, mv, small, gt, mqk, mqk, mv, small, gt, low, up, c0, m0)


G_LEVELS = (32, 16, 8, 4, 2, 1)


def _gla_select_matrices(reverse):
    L = G_CHUNK
    t = np.arange(L)
    u = np.arange(L)[None, :]
    tt = t[:, None]
    mats = [(u <= tt), (u > tt)]
    for h in G_LEVELS:
        start = (tt // (2 * h)) * (2 * h)
        second = tt >= start + h
        m = np.where(second, (u >= start + h) & (u <= tt), (u > tt) & (u < start + h))
        mats.append(m)
    m = np.concatenate([x.astype(np.float32) for x in mats], 0)
    if reverse:
        m = m.reshape(len(mats), L, L)[:, ::-1, ::-1].reshape(len(mats) * L, L)
    return jnp.asarray(m, BF16)


def _gla_kernel(emit, qf_ref, kf_ref, vf_ref, lrf_ref, qb_ref, kb_ref, vb_ref, lrb_ref,
                self_ref, selb_ref, gw_ref, gb_ref, s0_ref, *rest):
    if emit:
        of_ref, ob_ref, sout_ref, s_s = rest
    else:
        sout_ref, s_s = rest
        of_ref = ob_ref = None
    j = pl.program_id(1)
    L = G_CHUNK

    @pl.when(j == 0)
    def _():
        s_s[...] = s0_ref[0]

    rid = lax.broadcasted_iota(jnp.int32, (L, L), 0)
    cid = lax.broadcasted_iota(jnp.int32, (L, L), 1)
    rid_w = lax.broadcasted_iota(jnp.int32, (L, G_DK), 0)
    ones_blk = jnp.ones((L, G_DV), BF16)

    for d in range(2):
        q_ref, k_ref, v_ref, lr_ref, sel_ref, o_ref = (
            (qf_ref, kf_ref, vf_ref, lrf_ref, self_ref, of_ref) if d == 0 else
            (qb_ref, kb_ref, vb_ref, lrb_ref, selb_ref, ob_ref))
        lr = lr_ref[0][:, N_GATES + d * G_RANK:N_GATES + (d + 1) * G_RANK].astype(BF16)
        loga = _log_sigmoid(_dot(lr, gw_ref[d]) + gb_ref[d]) / G_TAU
        expo = jnp.exp(_sel_left(sel_ref[...], loga))
        e_cum = expo[0:L]
        e_rest = expo[L:2 * L]
        a1, a2, a3 = _split3(loga)
        e_tot = jnp.exp(_dot_tn(a1, ones_blk) + _dot_tn(a2, ones_blk) + _dot_tn(a3, ones_blk))
        for hd in range(G_HEADS):
            ks = slice(hd * G_DK, (hd + 1) * G_DK)
            vs = slice(hd * G_DV, (hd + 1) * G_DV)
            r = d * G_HEADS + hd
            q = q_ref[0][:, ks].astype(F32)
            k = k_ref[0][:, ks].astype(F32)
            v = v_ref[0][:, vs]
            s_mat = s_s[r]
            if emit:
                att = jnp.where(rid == cid, _dot_nt(q.astype(BF16), k.astype(BF16)), 0.0)
                for lv, h in enumerate(G_LEVELS):
                    e = expo[(2 + lv) * L:(3 + lv) * L, ks]
                    second = (rid_w // h) % 2 == 1
                    if d == 1:
                        second = jnp.logical_not(second)
                    qt = jnp.where(second, q * e, 0.0).astype(BF16)
                    kt = jnp.where(second, 0.0, k * e).astype(BF16)
                    same = (rid // (2 * h)) == (cid // (2 * h))
                    att = att + jnp.where(same, _dot_nt(qt, kt), 0.0)
                inter = _dot((q * e_cum[:, ks]).astype(BF16), s_mat.astype(BF16))
                o_ref[0, :, vs] = inter + _dot(att.astype(BF16), v)
            kd = (k * e_rest[:, ks]).astype(BF16)
            s_s[r] = e_tot[ks, :] * s_mat + _dot_tn(kd, v)

    @pl.when(j == pl.num_programs(1) - 1)
    def _():
        sout_ref[0] = s_s[...]


def _gla_scan(gq, gk, gv, small, g_gate_w, g_gate_b, s0, emit):
    b, t, _ = gv.shape
    L = G_CHUNK
    nc = t // L
    sel_f = _gla_select_matrices(False)
    sel_b = _gla_select_matrices(True)
    fw = lambda w: pl.BlockSpec((1, L, w), lambda i, j: (i, j, 0))
    bw = lambda w: pl.BlockSpec((1, L, w), lambda i, j: (i, nc - 1 - j, 0))
    const = lambda a: pl.BlockSpec(a.shape, lambda i, j: (0,) * a.ndim)
    sst = pl.BlockSpec((1, 2 * G_HEADS, G_DK, G_DV), lambda i, j: (i, 0, 0, 0))
    in_specs = [fw(G_QK), fw(G_QK), fw(G_WIDTH), fw(SMALL_COLS),
                bw(G_QK), bw(G_QK), bw(G_WIDTH), bw(SMALL_COLS),
                const(sel_f), const(sel_b), const(g_gate_w), const(g_gate_b), sst]
    out_specs = [sst]
    out_shape = [jax.ShapeDtypeStruct(s0.shape, F32)]
    if emit:
        out_specs = [fw(G_WIDTH), bw(G_WIDTH)] + out_specs
        out_shape = [jax.ShapeDtypeStruct((b, t, G_WIDTH), F32)] * 2 + out_shape
    return pl.pallas_call(
        functools.partial(_gla_kernel, emit),
        grid=(b, nc),
        in_specs=in_specs,
        out_specs=out_specs,
        out_shape=out_shape,
        scratch_shapes=[pltpu.VMEM((2 * G_HEADS, G_DK, G_DV), F32)],
        compiler_params=_params("parallel", "arbitrary"),
        name="gla_scan" if emit else "gla_ctx_state",
    )(gq, gk, gv, small, gq, gk, gv, small, sel_f, sel_b, g_gate_w, g_gate_b, s0)


FFN_CHUNK = 256


def _head_norm(h, nheads, width):
    parts = []
    for hd in range(nheads):
        blk = h[:, hd * width:(hd + 1) * width]
        parts.append(blk * lax.rsqrt(jnp.mean(blk * blk, -1, keepdims=True) + EPS))
    return jnp.concatenate(parts, 1)


def _out_kernel(d_ff, x_ref, hf_ref, hb_ref, of_ref, ob_ref, mo_ref, gr_ref, mod_ref,
                mng_ref, gng_ref, gffn_ref, gfin_ref, wout_ref, wgu_ref, wdown_ref, o_ref):
    x = x_ref[0]
    mod = mod_ref[0]
    hm = _head_norm(hf_ref[0] + hb_ref[0], M_HEADS, M_DV) * mng_ref[...] * _sigmoid(mo_ref[0].astype(F32))
    gr = gr_ref[0].astype(F32)
    hg = _head_norm(of_ref[0] + ob_ref[0], G_HEADS, G_DV) * gng_ref[...] * (gr * _sigmoid(gr))
    mix = jnp.concatenate([hm, hg], 1).astype(BF16)
    x1 = x + mod[0:1] * _dot(mix, wout_ref[...])
    h2 = x1 * lax.rsqrt(jnp.mean(x1 * x1, -1, keepdims=True) + EPS) * gffn_ref[...]
    h2 = (h2 * (1.0 + mod[2:3]) + mod[1:2]).astype(BF16)
    acc = jnp.zeros_like(x1)
    for c0 in range(0, d_ff, FFN_CHUNK):
        a = _dot(h2, wgu_ref[:, c0:c0 + FFN_CHUNK])
        g = _dot(h2, wgu_ref[:, d_ff + c0:d_ff + c0 + FFN_CHUNK])
        act = (a * _sigmoid(a) * g).astype(BF16)
        acc = acc + _dot(act, wdown_ref[c0:c0 + FFN_CHUNK, :])
    x2 = x1 + mod[3:4] * acc
    o_ref[0] = x2 * lax.rsqrt(jnp.mean(x2 * x2, -1, keepdims=True) + EPS) * gfin_ref[...]


def _out_block(x, hf, hb, of, ob, mo, gr, mod4, mng, gng, gffn, gfin, wout, wgu, wdown, tm):
    b, t, d = x.shape
    d_ff = wdown.shape[0]
    tok = lambda w: pl.BlockSpec((1, tm, w), lambda i, j: (i, j, 0))
    const = lambda a: pl.BlockSpec(a.shape, lambda i, j: (0,) * a.ndim, pipeline_mode=pl.Buffered(1))
    return pl.pallas_call(
        functools.partial(_out_kernel, d_ff),
        grid=(b, t // tm),
        in_specs=[tok(d), tok(M_WIDTH), tok(M_WIDTH), tok(G_WIDTH), tok(G_WIDTH), tok(M_WIDTH), tok(G_WIDTH),
                  pl.BlockSpec((1, 4, d), lambda i, j: (i, 0, 0)),
                  const(mng), const(gng), const(gffn), const(gfin), const(wout), const(wgu), const(wdown)],
        out_specs=tok(d),
        out_shape=jax.ShapeDtypeStruct((b, t, d), F32),
        compiler_params=_params("parallel", "arbitrary"),
        name="out_ffn",
    )(x, hf, hb, of, ob, mo, gr, mod4, mng, gng, gffn, gfin, wout, wgu, wdown)


def _largest_tile(t, cap):
    tm = min(t, cap)
    while t % tm:
        tm //= 2
    return tm


def kernel(x, c, ctx, c_ctx, w_ada, b_ada, g_mix, w_in, conv_w, m_gate_b, m_norm_g, g_gate_w, g_gate_b,
           g_norm_g, w_out, g_ffn, w_gu, w_down, g_final):
    b, t, d = x.shape
    tc = ctx.shape[1]
    assert w_ada.shape[0] == 1, "single layer block"
    assert t % (GRID_W * 2) == 0 and tc % M_CHUNK == 0

    pad = (-(b + 1)) % 8
    c_all = jnp.concatenate([c, c_ctx[None, :], jnp.zeros((pad, d), F32)], 0)
    mod = _modulation(c_all, w_ada[0], b_ada[0])
    mod_x = mod[:b].reshape(b, 6, d)
    mod_c = jnp.broadcast_to(mod[b].reshape(1, 6, d), (b, 6, d))

    w = w_in[0]
    sizes = (2 * M_QK, M_WIDTH, M_WIDTH, N_GATES, G_QK, G_QK, G_WIDTH, G_WIDTH, 2 * G_RANK)
    offs = np.concatenate([[0], np.cumsum(sizes)])
    seg = lambda i: w[:, offs[i]:offs[i + 1]]
    wmain = jnp.concatenate([seg(0), seg(1), seg(2), seg(4), seg(5), seg(6), seg(7)], 1).astype(BF16)
    n_small = N_GATES + 2 * G_RANK
    wsmall = jnp.concatenate([seg(3), seg(8), jnp.zeros((d, SMALL_COLS - n_small), F32)], 1).astype(BF16)
    wgt = jnp.transpose(seg(3)).astype(BF16)
    bsm = jnp.concatenate([m_gate_b[0], jnp.zeros((SMALL_COLS - N_GATES,), F32)]).reshape(1, SMALL_COLS)
    bgt = m_gate_b[0].reshape(N_GATES, 1)
    g_mix2 = g_mix[0].reshape(1, d)
    conv_w9 = conv_w[0].reshape(9, 2 * M_QK)
    kscale = jnp.concatenate([jnp.ones((M_QK,), F32), jnp.full((M_QK,), M_DK ** -0.5, F32)]).reshape(1, 2 * M_QK)
    ggw = g_gate_w[0].astype(BF16)
    ggb = g_gate_b[0].reshape(2, 1, G_QK)

    def features(inp, m6, tm):
        return _features(inp, m6[:, 0:1], m6[:, 1:2], g_mix2, wmain, wsmall, wgt, bsm, bgt, tm)

    fc = features(ctx, mod_c, _largest_tile(tc, 512))
    mqk_c = _conv(fc[0], conv_w9, kscale, tc, 1, 256)
    c0 = jnp.zeros((b, 2 * M_HEADS, M_DK, 2 * M_DV), F32)
    m0 = jnp.full((b, 2 * M_HEADS, LANE), NEG, F32)
    c_ctx_state, m_ctx_state = _mlstm_scan(mqk_c, fc[1], fc[7], fc[8], c0, m0, False)
    s0 = jnp.zeros((b, 2 * G_HEADS, G_DK, G_DV), F32)
    (s_ctx_state,) = _gla_scan(fc[3], fc[4], fc[5], fc[7], ggw, ggb, s0, False)

    fx = features(x, mod_x, _largest_tile(t, 512))
    rows = t // GRID_W
    mqk_x = _conv(fx[0], conv_w9, kscale, GRID_W, _largest_tile(rows, 4), 256)
    hf, hb, _, _ = _mlstm_scan(mqk_x, fx[1], fx[7], fx[8], c_ctx_state, m_ctx_state, True)
    of, ob, _ = _gla_scan(fx[3], fx[4], fx[5], fx[7], ggw, ggb, s_ctx_state, True)

    mod4 = jnp.stack([mod_x[:, 2], mod_x[:, 3], mod_x[:, 4], mod_x[:, 5]], 1)
    return _out_block(x, hf, hb, of, ob, fx[2], fx[6], mod4,
                      m_norm_g[0].reshape(1, -1), g_norm_g[0].reshape(1, -1),
                      g_ffn[0].reshape(1, d), g_final.reshape(1, d),
                      w_out[0].astype(BF16), w_gu[0].astype(BF16), w_down[0].astype(BF16),
                      _largest_tile(t, 256))
```

```python
import functools

import numpy as np
import jax
import jax.numpy as jnp
from jax import lax
from jax.experimental import pallas as pl
from jax.experimental.pallas import tpu as pltpu

F32 = jnp.float32
BF16 = jnp.bfloat16

GRID_W = 64
M_HEADS = 4
M_DK = 128
M_DV = 128
M_QK = M_HEADS * M_DK
M_WIDTH = M_HEADS * M_DV
M_CHUNK = 128
G_HEADS = 4
G_DK = 64
G_DV = 128
G_QK = G_HEADS * G_DK
G_WIDTH = G_HEADS * G_DV
G_RANK = 16
G_TAU = 16.0
G_CHUNK = 64
EPS = 1e-6
NEG = -1e30
LOG2_E = 1.4426950408889634

LANE = 128
VMEM_LIMIT = 56 * 1024 * 1024


def _params(*sem):
    return pltpu.CompilerParams(dimension_semantics=sem, vmem_limit_bytes=VMEM_LIMIT)


def _sigmoid(x):
    return 1.0 / (1.0 + jnp.exp(-x))


def _log_sigmoid(x):
    return jnp.minimum(x, 0.0) - jnp.log(1.0 + jnp.exp(-jnp.abs(x)))


def _split3(x):
    x1 = x.astype(BF16)
    r1 = x - x1.astype(F32)
    x2 = r1.astype(BF16)
    x3 = (r1 - x2.astype(F32)).astype(BF16)
    return x1, x2, x3


def _dot(a, b):
    return jnp.dot(a, b, preferred_element_type=F32)


def _dot_nt(a, b):
    return lax.dot_general(a, b, (((1,), (1,)), ((), ())), preferred_element_type=F32)


def _dot_tn(a, b):
    return lax.dot_general(a, b, (((0,), (0,)), ((), ())), preferred_element_type=F32)


def _sel_left(m01, x):
    x1, x2, x3 = _split3(x)
    return _dot(m01, x1) + _dot(m01, x2) + _dot(m01, x3)


def _sel_right(x, m01):
    x1, x2, x3 = _split3(x)
    return _dot(x1, m01) + _dot(x2, m01) + _dot(x3, m01)


def _mod_kernel(c_ref, w_ref, b_ref, o_ref):
    cs = c_ref[...]
    a = (cs * _sigmoid(cs)).astype(BF16)
    o_ref[...] = _dot(a, w_ref[...].astype(BF16)) + b_ref[...]


def _modulation(c_all, w_ada, b_ada):
    rows, d = c_all.shape
    n = w_ada.shape[1]
    bn = n // 6
    return pl.pallas_call(
        _mod_kernel,
        grid=(n // bn,),
        in_specs=[pl.BlockSpec((rows, d), lambda j: (0, 0)),
                  pl.BlockSpec((d, bn), lambda j: (0, j)),
                  pl.BlockSpec((1, bn), lambda j: (0, j))],
        out_specs=pl.BlockSpec((rows, bn), lambda j: (0, j)),
        out_shape=jax.ShapeDtypeStruct((rows, n), F32),
        compiler_params=_params("arbitrary"),
        name="adaln_mod",
    )(c_all, w_ada, b_ada.reshape(1, n))


_MAIN_SIZES = (("mqk", 2 * M_QK), ("mv", M_WIDTH), ("mo", M_WIDTH), ("gq", G_QK),
               ("gk", G_QK), ("gv", G_WIDTH), ("gr", G_WIDTH))
_MAIN_COLS = sum(s for _, s in _MAIN_SIZES)
N_GATES = 4 * M_HEADS
SMALL_COLS = LANE


def _feat_kernel(x_ref, sh_ref, sc_ref, g_ref, wmain_ref, wsmall_ref, wgt_ref, bsm_ref, bgt_ref,
                 mqk_ref, mv_ref, mo_ref, gq_ref, gk_ref, gv_ref, gr_ref, small_ref, gt_ref):
    x = x_ref[0]
    ms = jnp.mean(x * x, -1, keepdims=True)
    h = x * lax.rsqrt(ms + EPS) * g_ref[...]
    h = h * (1.0 + sc_ref[0]) + sh_ref[0]
    hb = h.astype(BF16)
    outs = dict(mqk=mqk_ref, mv=mv_ref, mo=mo_ref, gq=gq_ref, gk=gk_ref, gv=gv_ref, gr=gr_ref)
    off = 0
    for name, size in _MAIN_SIZES:
        z = _dot(hb, wmain_ref[:, off:off + size])
        if name == "gq":
            z = z * (G_DK ** -0.5)
        outs[name][0] = z.astype(BF16)
        off += size
    small_ref[0] = _dot(hb, wsmall_ref[...]) + bsm_ref[...]
    gt_ref[0] = _dot_nt(wgt_ref[...], hb) + bgt_ref[...]


def _features(x, shift, scale, g, wmain, wsmall, wgt, bsm, bgt, tm):
    b, t, d = x.shape
    nt = t // tm
    tok = lambda w: pl.BlockSpec((1, tm, w), lambda i, j: (i, j, 0))
    const = lambda a: pl.BlockSpec(a.shape, lambda i, j: (0,) * a.ndim)
    vec = pl.BlockSpec((1, 1, d), lambda i, j: (i, 0, 0))
    out_shape = [jax.ShapeDtypeStruct((b, t, s), BF16) for _, s in _MAIN_SIZES]
    out_specs = [tok(s) for _, s in _MAIN_SIZES]
    out_shape += [jax.ShapeDtypeStruct((b, t, SMALL_COLS), F32), jax.ShapeDtypeStruct((b, N_GATES, t), F32)]
    out_specs += [tok(SMALL_COLS), pl.BlockSpec((1, N_GATES, tm), lambda i, j: (i, 0, j))]
    return pl.pallas_call(
        _feat_kernel,
        grid=(b, nt),
        in_specs=[tok(d), vec, vec, const(g), const(wmain), const(wsmall), const(wgt), const(bsm), const(bgt)],
        out_specs=out_specs,
        out_shape=out_shape,
        compiler_params=_params("parallel", "arbitrary"),
        name="features",
    )(x, shift, scale, g, wmain, wsmall, wgt, bsm, bgt)


def _conv_kernel(cols, nrow, prev_ref, main_ref, next_ref, w_ref, ksc_ref, o_ref):
    i = pl.program_id(1)
    n = pl.num_programs(1)
    main = main_ref[0].astype(F32)
    prev = jnp.where(i > 0, prev_ref[0].astype(F32), 0.0)
    nxt = jnp.where(i < n - 1, next_ref[0].astype(F32), 0.0)
    ext = jnp.concatenate([prev, main, nxt], 0)
    tot = (nrow + 2) * cols
    col = lax.broadcasted_iota(jnp.int32, ext.shape, 0) % cols
    left = jnp.where(col == 0, 0.0, pltpu.roll(ext, 1, 0))
    right = jnp.where(col == cols - 1, 0.0, pltpu.roll(ext, tot - 1, 0))
    srcs = (left, ext, right)
    w = w_ref[...]
    acc = None
    for di in range(3):
        for dj in range(3):
            term = srcs[dj][di * cols:(di + nrow) * cols] * w[di * 3 + dj:di * 3 + dj + 1]
            acc = term if acc is None else acc + term
    o_ref[0] = (acc * _sigmoid(acc) * ksc_ref[...]).astype(BF16)


def _conv(mqk_pre, conv_w9, kscale, cols, nrow, cb):
    b, t, ch = mqk_pre.shape
    rows = t // cols
    nblk = rows // nrow
    main = pl.BlockSpec((1, nrow * cols, cb), lambda i, j, c: (i, j, c))
    prev = pl.BlockSpec((1, cols, cb), lambda i, j, c: (i, jnp.maximum(j * nrow - 1, 0), c))
    nxt = pl.BlockSpec((1, cols, cb), lambda i, j, c: (i, jnp.minimum((j + 1) * nrow, rows - 1), c))
    return pl.pallas_call(
        functools.partial(_conv_kernel, cols, nrow),
        grid=(b, nblk, ch // cb),
        in_specs=[prev, main, nxt,
                  pl.BlockSpec((9, cb), lambda i, j, c: (0, c)),
                  pl.BlockSpec((1, cb), lambda i, j, c: (0, c))],
        out_specs=main,
        out_shape=jax.ShapeDtypeStruct((b, t, ch), BF16),
        compiler_params=_params("parallel", "arbitrary", "arbitrary"),
        name="dwconv_silu",
    )(mqk_pre, mqk_pre, mqk_pre, conv_w9, kscale)


def _tri(n):
    low = np.tril(np.ones((n, n), np.float32))
    return jnp.asarray(low, BF16), jnp.asarray(low.T, BF16)


def _mlstm_kernel(emit, qf_ref, kf_ref, vf_ref, gcf_ref, grf_ref, qb_ref, kb_ref, vb_ref, gcb_ref, grb_ref,
                  low_ref, up_ref, c0_ref, m0_ref, *rest):
    if emit:
        hf_ref, hb_ref, cout_ref, mout_ref, c_s, m_s = rest
    else:
        cout_ref, mout_ref, c_s, m_s = rest
        hf_ref = hb_ref = None
    j = pl.program_id(1)
    L = M_CHUNK

    @pl.when(j == 0)
    def _():
        c_s[...] = c0_ref[0]
        m_s[...] = m0_ref[0]

    low = low_ref[...]
    up = up_ref[...]
    rid = lax.broadcasted_iota(jnp.int32, (L, L), 0)
    cid = lax.broadcasted_iota(jnp.int32, (L, L), 1)
    ones_col = jnp.where(lax.broadcasted_iota(jnp.int32, (L, LANE), 1) == 0, 1.0, 0.0).astype(BF16)

    for d in range(2):
        q_ref, k_ref, v_ref, gc_ref, gr_ref, h_ref = (
            (qf_ref, kf_ref, vf_ref, gcf_ref, grf_ref, hf_ref) if d == 0 else
            (qb_ref, kb_ref, vb_ref, gcb_ref, grb_ref, hb_ref))
        mask = (cid <= rid) if d == 0 else (cid >= rid)
        gcol = gc_ref[0][:, :N_GATES]
        grow = gr_ref[0]
        cum_col = _sel_left(low if d == 0 else up, _log_sigmoid(gcol))
        cum_row = _sel_right(_log_sigmoid(grow), up if d == 0 else low)
        for hd in range(M_HEADS):
            ci = 2 * M_HEADS * d + hd
            cf = ci + M_HEADS
            sl = slice(hd * M_DK, (hd + 1) * M_DK)
            r = d * M_HEADS + hd
            q = q_ref[0][:, sl]
            k = k_ref[0][:, sl]
            v_aug = jnp.concatenate([v_ref[0][:, sl], ones_col], 1)
            bc = cum_col[:, cf:cf + 1]
            br = cum_row[cf:cf + 1, :]
            ic = gcol[:, ci:ci + 1]
            ir = grow[ci:ci + 1, :]
            btot = bc[L - 1:L, :] if d == 0 else bc[0:1, :]
            m = m_s[r:r + 1, 0:1]
            c_aug = c_s[r]
            if emit:
                dm = jnp.where(mask, bc - br + ir, NEG)
                inter = m + bc
                mt = jnp.maximum(inter, jnp.max(dm, -1, keepdims=True))
                w_inter = jnp.exp(inter - mt)
                p = _dot_nt(q, k) * jnp.exp(dm - mt)
                num = w_inter * _dot(q, c_aug.astype(BF16)) + _dot(p.astype(BF16), v_aug)
                den = num[:, M_DV:M_DV + 1]
                h_ref[0, :, sl] = num[:, :M_DV] / jnp.maximum(jnp.abs(den), jnp.exp(-mt))
            src = btot - bc + ic
            m_new = jnp.maximum(m + btot, jnp.max(src, 0, keepdims=True))
            ws = jnp.exp(src - m_new)
            wc = jnp.exp(m + btot - m_new)
            kw = (k.astype(F32) * ws).astype(BF16)
            c_s[r] = wc * c_aug + _dot_tn(kw, v_aug)
            m_s[r:r + 1, :] = jnp.broadcast_to(m_new, (1, LANE))

    @pl.when(j == pl.num_programs(1) - 1)
    def _():
        cout_ref[0] = c_s[...]
        mout_ref[0] = m_s[...]


def _mlstm_scan(mqk, mv, small, gt, c0, m0, emit):
    b, t, _ = mv.shape
    L = M_CHUNK
    nc = t // L
    low, up = _tri(L)
    fw = lambda w, cblk: pl.BlockSpec((1, L, w), lambda i, j: (i, j, cblk))
    bw = lambda w, cblk: pl.BlockSpec((1, L, w), lambda i, j: (i, nc - 1 - j, cblk))
    gtf = pl.BlockSpec((1, N_GATES, L), lambda i, j: (i, 0, j))
    gtb = pl.BlockSpec((1, N_GATES, L), lambda i, j: (i, 0, nc - 1 - j))
    tri = pl.BlockSpec((L, L), lambda i, j: (0, 0))
    cst = pl.BlockSpec((1, 2 * M_HEADS, M_DK, 2 * M_DV), lambda i, j: (i, 0, 0, 0))
    mst = pl.BlockSpec((1, 2 * M_HEADS, LANE), lambda i, j: (i, 0, 0))
    in_specs = [fw(M_QK, 0), fw(M_QK, 1), fw(M_WIDTH, 0), fw(SMALL_COLS, 0), gtf,
                bw(M_QK, 0), bw(M_QK, 1), bw(M_WIDTH, 0), bw(SMALL_COLS, 0), gtb,
                tri, tri, cst, mst]
    out_specs = [cst, mst]
    out_shape = [jax.ShapeDtypeStruct(c0.shape, F32), jax.ShapeDtypeStruct(m0.shape, F32)]
    if emit:
        out_specs = [fw(M_WIDTH, 0), bw(M_WIDTH, 0)] + out_specs
        out_shape = [jax.ShapeDtypeStruct((b, t, M_WIDTH), F32)] * 2 + out_shape
    return pl.pallas_call(
        functools.partial(_mlstm_kernel, emit),
        grid=(b, nc),
        in_specs=in_specs,
        out_specs=out_specs,
        out_shape=out_shape,
        scratch_shapes=[pltpu.VMEM((2 * M_HEADS, M_DK, 2 * M_DV), F32), pltpu.VMEM((2 * M_HEADS, LANE), F32)],
        compiler_params=_params("parallel", "arbitrary"),
        name="mlstm_scan" if emit else "mlstm_ctx_state",
    )(mqk, mqk, mv, small, gt, mqk, mqk, mv, small, gt, low, up, c0, m0)


G_LEVELS = (32, 16, 8, 4, 2, 1)


def _gla_select_matrices(reverse):
    L = G_CHUNK
    t = np.arange(L)
    u = np.arange(L)[None, :]
    tt = t[:, None]
    mats = [(u <= tt), (u > tt)]
    for h in G_LEVELS:
        start = (tt // (2 * h)) * (2 * h)
        second = tt >= start + h
        m = np.where(second, (u >= start + h) & (u <= tt), (u > tt) & (u < start + h))
        mats.append(m)
    m = np.concatenate([x.astype(np.float32) for x in mats], 0)
    if reverse:
        m = m.reshape(len(mats), L, L)[:, ::-1, ::-1].reshape(len(mats) * L, L)
    return jnp.asarray(np.concatenate([m, m, m], 1), BF16)


def _gla_pair_masks(reverse):
    L = G_CHUNK
    s = np.arange(L)[:, None]
    t = np.arange(L)[None, :]
    masks = [s == t]
    for h in G_LEVELS:
        same = (s // (2 * h)) == (t // (2 * h))
        masks.append(same & ((t // h) % 2 == 1) & ((s // h) % 2 == 0))
    m = np.stack(masks).astype(np.float32)
    if reverse:
        m = m[:, ::-1, ::-1]
    return jnp.asarray(np.tile(m, (1, 1, G_HEADS)), F32)


def _gla_head_mask():
    r = np.arange(G_HEADS * G_CHUNK)[:, None] // G_CHUNK
    c = np.arange(G_QK)[None, :] // G_DK
    return jnp.asarray((r == c).astype(np.float32), BF16)


def _gla_kernel(emit, nch, qf_ref, kf_ref, vf_ref, lrf_ref, qb_ref, kb_ref, vb_ref, lrb_ref,
                self_ref, selb_ref, mkf_ref, mkb_ref, hm_ref, gw_ref, gb_ref, s0_ref, *rest):
    if emit:
        of_ref, ob_ref, sout_ref, s_s = rest
    else:
        sout_ref, s_s = rest
        of_ref = ob_ref = None
    j = pl.program_id(1)
    L = G_CHUNK

    @pl.when(j == 0)
    def _():
        s_s[...] = s0_ref[0]

    hm = hm_ref[...]
    stack = lambda a: jnp.concatenate([a] * G_HEADS, 0)

    for d in range(2):
        q_ref, k_ref, v_ref, lr_ref, sel_ref, mk_ref, o_ref = (
            (qf_ref, kf_ref, vf_ref, lrf_ref, self_ref, mkf_ref, of_ref) if d == 0 else
            (qb_ref, kb_ref, vb_ref, lrb_ref, selb_ref, mkb_ref, ob_ref))
        lr = lr_ref[0][:, N_GATES + d * G_RANK:N_GATES + (d + 1) * G_RANK].astype(BF16)
        loga = _log_sigmoid(_dot(lr, gw_ref[d]) + gb_ref[d]) * (LOG2_E / G_TAU)
        pieces = _split3(loga)
        a3 = [jnp.concatenate([p[c * L:(c + 1) * L] for p in pieces], 0) for c in range(nch)]
        expo = jnp.exp2(_dot(sel_ref[...], jnp.concatenate(a3, 1)))
        for c in (range(nch) if d == 0 else reversed(range(nch))):
            rows = slice(c * L, (c + 1) * L)
            cols = slice(c * G_QK, (c + 1) * G_QK)
            k = k_ref[0, rows, :]
            v_st = jnp.concatenate([v_ref[0, rows, hd * G_DV:(hd + 1) * G_DV] for hd in range(G_HEADS)], 0)
            e = lambda i: expo[i * L:(i + 1) * L, cols].astype(BF16)
            kd_st = stack(k * e(1)) * hm
            e_tot = expo[L - 1:L, cols] if d == 0 else expo[0:1, cols]
            st = s_s[d]
            if emit:
                qs = stack(q_ref[0, rows, :]) * hm
                att_t = mk_ref[0] * _dot_nt(k, qs)
                for lv in range(len(G_LEVELS)):
                    ev = e(2 + lv)
                    att_t = att_t + mk_ref[1 + lv] * _dot_nt(k * ev, qs * stack(ev))
                inter_t = _dot_nt(st.astype(BF16), qs * stack(e(0)))
                att_st = stack(att_t.astype(BF16)) * hm
                res = _dot_tn(v_st, jnp.concatenate([att_st, kd_st], 1))
                o = jnp.transpose(inter_t + res[:, :G_HEADS * L])
                for hd in range(G_HEADS):
                    o_ref[0, rows, hd * G_DV:(hd + 1) * G_DV] = o[hd * L:(hd + 1) * L]
                u_t = res[:, G_HEADS * L:]
            else:
                u_t = _dot_tn(v_st, kd_st)
            s_s[d] = e_tot * st + u_t

    @pl.when(j == pl.num_programs(1) - 1)
    def _():
        sout_ref[0] = s_s[...]


def _gla_scan(gq, gk, gv, small, g_gate_w, g_gate_b, s0, emit, tb):
    b, t, _ = gv.shape
    nb = t // tb
    consts = (_gla_select_matrices(False), _gla_select_matrices(True),
              _gla_pair_masks(False), _gla_pair_masks(True), _gla_head_mask(), g_gate_w, g_gate_b)
    fw = lambda w: pl.BlockSpec((1, tb, w), lambda i, j: (i, j, 0))
    bw = lambda w: pl.BlockSpec((1, tb, w), lambda i, j: (i, nb - 1 - j, 0))
    const = lambda a: pl.BlockSpec(a.shape, lambda i, j: (0,) * a.ndim)
    sst = pl.BlockSpec((1, 2, G_DV, G_QK), lambda i, j: (i, 0, 0, 0))
    in_specs = [fw(G_QK), fw(G_QK), fw(G_WIDTH), fw(SMALL_COLS),
                bw(G_QK), bw(G_QK), bw(G_WIDTH), bw(SMALL_COLS)] + [const(a) for a in consts] + [sst]
    out_specs = [sst]
    out_shape = [jax.ShapeDtypeStruct(s0.shape, F32)]
    if emit:
        out_specs = [fw(G_WIDTH), bw(G_WIDTH)] + out_specs
        out_shape = [jax.ShapeDtypeStruct((b, t, G_WIDTH), F32)] * 2 + out_shape
    return pl.pallas_call(
        functools.partial(_gla_kernel, emit, tb // G_CHUNK),
        grid=(b, nb),
        in_specs=in_specs,
        out_specs=out_specs,
        out_shape=out_shape,
        scratch_shapes=[pltpu.VMEM((2, G_DV, G_QK), F32)],
        compiler_params=_params("parallel", "arbitrary"),
        name="gla_scan" if emit else "gla_ctx_state",
    )(gq, gk, gv, small, gq, gk, gv, small, *consts, s0)


FFN_CHUNK = 256


def _head_norm(h, nheads, width):
    parts = []
    for hd in range(nheads):
        blk = h[:, hd * width:(hd + 1) * width]
        parts.append(blk * lax.rsqrt(jnp.mean(blk * blk, -1, keepdims=True) + EPS))
    return jnp.concatenate(parts, 1)


def _out_kernel(d_ff, x_ref, hf_ref, hb_ref, of_ref, ob_ref, mo_ref, gr_ref, mod_ref,
                mng_ref, gng_ref, gffn_ref, gfin_ref, wout_ref, wgu_ref, wdown_ref, o_ref):
    x = x_ref[0]
    mod = mod_ref[0]
    hm = _head_norm(hf_ref[0] + hb_ref[0], M_HEADS, M_DV) * mng_ref[...] * _sigmoid(mo_ref[0].astype(F32))
    gr = gr_ref[0].astype(F32)
    hg = _head_norm(of_ref[0] + ob_ref[0], G_HEADS, G_DV) * gng_ref[...] * (gr * _sigmoid(gr))
    mix = jnp.concatenate([hm, hg], 1).astype(BF16)
    x1 = x + mod[0:1] * _dot(mix, wout_ref[...])
    h2 = x1 * lax.rsqrt(jnp.mean(x1 * x1, -1, keepdims=True) + EPS) * gffn_ref[...]
    h2 = (h2 * (1.0 + mod[2:3]) + mod[1:2]).astype(BF16)
    acc = jnp.zeros_like(x1)
    for c0 in range(0, d_ff, FFN_CHUNK):
        a = _dot(h2, wgu_ref[:, c0:c0 + FFN_CHUNK])
        g = _dot(h2, wgu_ref[:, d_ff + c0:d_ff + c0 + FFN_CHUNK])
        act = (a * _sigmoid(a) * g).astype(BF16)
        acc = acc + _dot(act, wdown_ref[c0:c0 + FFN_CHUNK, :])
    x2 = x1 + mod[3:4] * acc
    o_ref[0] = x2 * lax.rsqrt(jnp.mean(x2 * x2, -1, keepdims=True) + EPS) * gfin_ref[...]


def _out_block(x, hf, hb, of, ob, mo, gr, mod4, mng, gng, gffn, gfin, wout, wgu, wdown, tm):
    b, t, d = x.shape
    d_ff = wdown.shape[0]
    tok = lambda w: pl.BlockSpec((1, tm, w), lambda i, j: (i, j, 0))
    const = lambda a: pl.BlockSpec(a.shape, lambda i, j: (0,) * a.ndim, pipeline_mode=pl.Buffered(1))
    return pl.pallas_call(
        functools.partial(_out_kernel, d_ff),
        grid=(b, t // tm),
        in_specs=[tok(d), tok(M_WIDTH), tok(M_WIDTH), tok(G_WIDTH), tok(G_WIDTH), tok(M_WIDTH), tok(G_WIDTH),
                  pl.BlockSpec((1, 4, d), lambda i, j: (i, 0, 0)),
                  const(mng), const(gng), const(gffn), const(gfin), const(wout), const(wgu), const(wdown)],
        out_specs=tok(d),
        out_shape=jax.ShapeDtypeStruct((b, t, d), F32),
        compiler_params=_params("parallel", "arbitrary"),
        name="out_ffn",
    )(x, hf, hb, of, ob, mo, gr, mod4, mng, gng, gffn, gfin, wout, wgu, wdown)


def _largest_tile(t, cap):
    tm = min(t, cap)
    while t % tm:
        tm //= 2
    return tm


def kernel(x, c, ctx, c_ctx, w_ada, b_ada, g_mix, w_in, conv_w, m_gate_b, m_norm_g, g_gate_w, g_gate_b,
           g_norm_g, w_out, g_ffn, w_gu, w_down, g_final):
    b, t, d = x.shape
    tc = ctx.shape[1]
    assert w_ada.shape[0] == 1, "single layer block"
    assert t % (GRID_W * 2) == 0 and tc % M_CHUNK == 0

    pad = (-(b + 1)) % 8
    c_all = jnp.concatenate([c, c_ctx[None, :], jnp.zeros((pad, d), F32)], 0)
    mod = _modulation(c_all, w_ada[0], b_ada[0])
    mod_x = mod[:b].reshape(b, 6, d)
    mod_c = jnp.broadcast_to(mod[b].reshape(1, 6, d), (b, 6, d))

    w = w_in[0]
    sizes = (2 * M_QK, M_WIDTH, M_WIDTH, N_GATES, G_QK, G_QK, G_WIDTH, G_WIDTH, 2 * G_RANK)
    offs = np.concatenate([[0], np.cumsum(sizes)])
    seg = lambda i: w[:, offs[i]:offs[i + 1]]
    wmain = jnp.concatenate([seg(0), seg(1), seg(2), seg(4), seg(5), seg(6), seg(7)], 1).astype(BF16)
    n_small = N_GATES + 2 * G_RANK
    wsmall = jnp.concatenate([seg(3), seg(8), jnp.zeros((d, SMALL_COLS - n_small), F32)], 1).astype(BF16)
    wgt = jnp.transpose(seg(3)).astype(BF16)
    bsm = jnp.concatenate([m_gate_b[0], jnp.zeros((SMALL_COLS - N_GATES,), F32)]).reshape(1, SMALL_COLS)
    bgt = m_gate_b[0].reshape(N_GATES, 1)
    g_mix2 = g_mix[0].reshape(1, d)
    conv_w9 = conv_w[0].reshape(9, 2 * M_QK)
    kscale = jnp.concatenate([jnp.ones((M_QK,), F32), jnp.full((M_QK,), M_DK ** -0.5, F32)]).reshape(1, 2 * M_QK)
    ggw = g_gate_w[0].astype(BF16)
    ggb = g_gate_b[0].reshape(2, 1, G_QK)

    def features(inp, m6, tm):
        return _features(inp, m6[:, 0:1], m6[:, 1:2], g_mix2, wmain, wsmall, wgt, bsm, bgt, tm)

    fc = features(ctx, mod_c, _largest_tile(tc, 512))
    mqk_c = _conv(fc[0], conv_w9, kscale, tc, 1, 256)
    c0 = jnp.zeros((b, 2 * M_HEADS, M_DK, 2 * M_DV), F32)
    m0 = jnp.full((b, 2 * M_HEADS, LANE), NEG, F32)
    c_ctx_state, m_ctx_state = _mlstm_scan(mqk_c, fc[1], fc[7], fc[8], c0, m0, False)
    s0 = jnp.zeros((b, 2, G_DV, G_QK), F32)
    (s_ctx_state,) = _gla_scan(fc[3], fc[4], fc[5], fc[7], ggw, ggb, s0, False, _largest_tile(tc, 256))

    fx = features(x, mod_x, _largest_tile(t, 512))
    rows = t // GRID_W
    mqk_x = _conv(fx[0], conv_w9, kscale, GRID_W, _largest_tile(rows, 4), 256)
    hf, hb, _, _ = _mlstm_scan(mqk_x, fx[1], fx[7], fx[8], c_ctx_state, m_ctx_state, True)
    of, ob, _ = _gla_scan(fx[3], fx[4], fx[5], fx[7], ggw, ggb, s_ctx_state, True, _largest_tile(t, 256))

    mod4 = jnp.stack([mod_x[:, 2], mod_x[:, 3], mod_x[:, 4], mod_x[:, 5]], 1)
    return _out_block(x, hf, hb, of, ob, fx[2], fx[6], mod4,
                      m_norm_g[0].reshape(1, -1), g_norm_g[0].reshape(1, -1),
                      g_ffn[0].reshape(1, d), g_final.reshape(1, d),
                      w_out[0].astype(BF16), w_gu[0].astype(BF16), w_down[0].astype(BF16),
                      _largest_tile(t, 256))
```

```python
import functools

import numpy as np
import jax
import jax.numpy as jnp
from jax import lax
from jax.experimental import pallas as pl
from jax.experimental.pallas import tpu as pltpu

F32 = jnp.float32
BF16 = jnp.bfloat16

GRID_W = 64
M_HEADS = 4
M_DK = 128
M_DV = 128
M_QK = M_HEADS * M_DK
M_WIDTH = M_HEADS * M_DV
M_CHUNK = 128
G_HEADS = 4
G_DK = 64
G_DV = 128
G_QK = G_HEADS * G_DK
G_WIDTH = G_HEADS * G_DV
G_RANK = 16
G_TAU = 16.0
G_CHUNK = 64
EPS = 1e-6
NEG = -1e30
LOG2_E = 1.4426950408889634

LANE = 128
VMEM_LIMIT = 56 * 1024 * 1024


def _params(*sem):
    return pltpu.CompilerParams(dimension_semantics=sem, vmem_limit_bytes=VMEM_LIMIT)


def _sigmoid(x):
    return 1.0 / (1.0 + jnp.exp(-x))


def _log_sigmoid(x):
    return jnp.minimum(x, 0.0) - jnp.log(1.0 + jnp.exp(-jnp.abs(x)))


def _split3(x):
    x1 = x.astype(BF16)
    r1 = x - x1.astype(F32)
    x2 = r1.astype(BF16)
    x3 = (r1 - x2.astype(F32)).astype(BF16)
    return x1, x2, x3


def _dot(a, b):
    return jnp.dot(a, b, preferred_element_type=F32)


def _dot_nt(a, b):
    return lax.dot_general(a, b, (((1,), (1,)), ((), ())), preferred_element_type=F32)


def _dot_tn(a, b):
    return lax.dot_general(a, b, (((0,), (0,)), ((), ())), preferred_element_type=F32)


def _sel_left(m01, x):
    x1, x2, x3 = _split3(x)
    return _dot(m01, x1) + _dot(m01, x2) + _dot(m01, x3)


def _sel_right(x, m01):
    x1, x2, x3 = _split3(x)
    return _dot(x1, m01) + _dot(x2, m01) + _dot(x3, m01)


def _mod_kernel(c_ref, w_ref, b_ref, o_ref):
    cs = c_ref[...]
    a = (cs * _sigmoid(cs)).astype(BF16)
    o_ref[...] = _dot(a, w_ref[...].astype(BF16)) + b_ref[...]


def _modulation(c_all, w_ada, b_ada):
    rows, d = c_all.shape
    n = w_ada.shape[1]
    bn = n // 6
    return pl.pallas_call(
        _mod_kernel,
        grid=(n // bn,),
        in_specs=[pl.BlockSpec((rows, d), lambda j: (0, 0)),
                  pl.BlockSpec((d, bn), lambda j: (0, j)),
                  pl.BlockSpec((1, bn), lambda j: (0, j))],
        out_specs=pl.BlockSpec((rows, bn), lambda j: (0, j)),
        out_shape=jax.ShapeDtypeStruct((rows, n), F32),
        compiler_params=_params("arbitrary"),
        name="adaln_mod",
    )(c_all, w_ada, b_ada.reshape(1, n))


_TOK_SIZES = (("mqk", 2 * M_QK), ("gq", G_QK), ("gk", G_QK), ("gv", G_WIDTH), ("gr", G_WIDTH))
_FEAT_SIZES = (("mv", M_WIDTH), ("mo", M_WIDTH))
N_GATES = 4 * M_HEADS
SMALL_COLS = LANE


def _feat_kernel(x_ref, sh_ref, sc_ref, g_ref, wtok_ref, wfeat_ref, wsmall_ref, wgt_ref, bsm_ref, bgt_ref,
                 mqk_ref, gq_ref, gk_ref, gv_ref, gr_ref, mvt_ref, mot_ref, small_ref, gt_ref):
    x = x_ref[0]
    ms = jnp.mean(x * x, -1, keepdims=True)
    h = x * lax.rsqrt(ms + EPS) * g_ref[...]
    h = h * (1.0 + sc_ref[0]) + sh_ref[0]
    hb = h.astype(BF16)
    outs = dict(mqk=mqk_ref, gq=gq_ref, gk=gk_ref, gv=gv_ref, gr=gr_ref, mv=mvt_ref, mo=mot_ref)
    off = 0
    for name, size in _TOK_SIZES:
        z = _dot(hb, wtok_ref[:, off:off + size])
        if name == "gq":
            z = z * (G_DK ** -0.5)
        outs[name][0] = z.astype(BF16)
        off += size
    off = 0
    for name, size in _FEAT_SIZES:
        outs[name][0] = _dot_nt(wfeat_ref[off:off + size, :], hb).astype(BF16)
        off += size
    small_ref[0] = _dot(hb, wsmall_ref[...]) + bsm_ref[...]
    gt_ref[0] = _dot_nt(wgt_ref[...], hb) + bgt_ref[...]


def _features(x, shift, scale, g, wtok, wfeat, wsmall, wgt, bsm, bgt, tm):
    b, t, d = x.shape
    nt = t // tm
    tok = lambda w: pl.BlockSpec((1, tm, w), lambda i, j: (i, j, 0))
    feat = lambda w: pl.BlockSpec((1, w, tm), lambda i, j: (i, 0, j))
    const = lambda a: pl.BlockSpec(a.shape, lambda i, j: (0,) * a.ndim)
    vec = pl.BlockSpec((1, 1, d), lambda i, j: (i, 0, 0))
    out_shape = [jax.ShapeDtypeStruct((b, t, s), BF16) for _, s in _TOK_SIZES]
    out_specs = [tok(s) for _, s in _TOK_SIZES]
    out_shape += [jax.ShapeDtypeStruct((b, s, t), BF16) for _, s in _FEAT_SIZES]
    out_specs += [feat(s) for _, s in _FEAT_SIZES]
    out_shape += [jax.ShapeDtypeStruct((b, t, SMALL_COLS), F32), jax.ShapeDtypeStruct((b, N_GATES, t), F32)]
    out_specs += [tok(SMALL_COLS), feat(N_GATES)]
    consts = (g, wtok, wfeat, wsmall, wgt, bsm, bgt)
    return pl.pallas_call(
        _feat_kernel,
        grid=(b, nt),
        in_specs=[tok(d), vec, vec] + [const(a) for a in consts],
        out_specs=out_specs,
        out_shape=out_shape,
        compiler_params=_params("parallel", "arbitrary"),
        name="features",
    )(x, shift, scale, *consts)


def _conv_kernel(cols, nrow, prev_ref, main_ref, next_ref, w_ref, ksc_ref, o_ref):
    i = pl.program_id(1)
    n = pl.num_programs(1)
    main = main_ref[0].astype(F32)
    prev = jnp.where(i > 0, prev_ref[0].astype(F32), 0.0)
    nxt = jnp.where(i < n - 1, next_ref[0].astype(F32), 0.0)
    ext = jnp.concatenate([prev, main, nxt], 0)
    tot = (nrow + 2) * cols
    col = lax.broadcasted_iota(jnp.int32, ext.shape, 0) % cols
    left = jnp.where(col == 0, 0.0, pltpu.roll(ext, 1, 0))
    right = jnp.where(col == cols - 1, 0.0, pltpu.roll(ext, tot - 1, 0))
    srcs = (left, ext, right)
    w = w_ref[...]
    acc = None
    for di in range(3):
        for dj in range(3):
            term = srcs[dj][di * cols:(di + nrow) * cols] * w[di * 3 + dj:di * 3 + dj + 1]
            acc = term if acc is None else acc + term
    o_ref[0] = (acc * _sigmoid(acc) * ksc_ref[...]).astype(BF16)


def _conv(mqk_pre, conv_w9, kscale, cols, nrow, cb):
    b, t, ch = mqk_pre.shape
    rows = t // cols
    nblk = rows // nrow
    main = pl.BlockSpec((1, nrow * cols, cb), lambda i, j, c: (i, j, c))
    prev = pl.BlockSpec((1, cols, cb), lambda i, j, c: (i, jnp.maximum(j * nrow - 1, 0), c))
    nxt = pl.BlockSpec((1, cols, cb), lambda i, j, c: (i, jnp.minimum((j + 1) * nrow, rows - 1), c))
    return pl.pallas_call(
        functools.partial(_conv_kernel, cols, nrow),
        grid=(b, nblk, ch // cb),
        in_specs=[prev, main, nxt,
                  pl.BlockSpec((9, cb), lambda i, j, c: (0, c)),
                  pl.BlockSpec((1, cb), lambda i, j, c: (0, c))],
        out_specs=main,
        out_shape=jax.ShapeDtypeStruct((b, t, ch), BF16),
        compiler_params=_params("parallel", "arbitrary", "arbitrary"),
        name="dwconv_silu",
    )(mqk_pre, mqk_pre, mqk_pre, conv_w9, kscale)


M_AUG = M_DV + 16


def _mlstm_cumsum_matrices(reverse):
    L = M_CHUNK
    low = np.tril(np.ones((L, L), np.float32))
    tri = low.T if reverse else low
    col = np.concatenate([tri, tri, tri], 1)
    row = np.tile(np.concatenate([tri.T, np.ones((L, L), np.float32)], 1), (3, 1))
    return jnp.asarray(col, BF16), jnp.asarray(row, BF16)


def _mlstm_kernel(emit, nch, qf_ref, kf_ref, vf_ref, gcf_ref, grf_ref, qb_ref, kb_ref, vb_ref, gcb_ref, grb_ref,
                  colf_ref, rowf_ref, colb_ref, rowb_ref, c0_ref, m0_ref, *rest):
    if emit:
        hf_ref, hb_ref, cout_ref, mout_ref, c_s, m_s = rest
    else:
        cout_ref, mout_ref, c_s, m_s = rest
        hf_ref = hb_ref = None
    j = pl.program_id(1)
    L = M_CHUNK

    @pl.when(j == 0)
    def _():
        c_s[...] = c0_ref[0]
        m_s[...] = m0_ref[0]

    sid = lax.broadcasted_iota(jnp.int32, (L, L), 0)
    tid = lax.broadcasted_iota(jnp.int32, (L, L), 1)
    ones_rows = jnp.ones((M_AUG - M_DV, L), BF16)

    for d in range(2):
        q_ref, k_ref, v_ref, gc_ref, gr_ref, col_ref, row_ref, h_ref = (
            (qf_ref, kf_ref, vf_ref, gcf_ref, grf_ref, colf_ref, rowf_ref, hf_ref) if d == 0 else
            (qb_ref, kb_ref, vb_ref, gcb_ref, grb_ref, colb_ref, rowb_ref, hb_ref))
        mask = (sid <= tid) if d == 0 else (sid >= tid)
        for c in (range(nch) if d == 0 else reversed(range(nch))):
            tok = slice(c * L, (c + 1) * L)
            gcol = gc_ref[0, tok, :][:, :N_GATES]
            grow = gr_ref[0, :, tok]
            cum_col = _dot(col_ref[...], jnp.concatenate(_split3(_log_sigmoid(gcol) * LOG2_E), 0))
            cum_ext = _dot(jnp.concatenate(_split3(_log_sigmoid(grow) * LOG2_E), 1), row_ref[...])
            gcol = gcol * LOG2_E
            grow = grow * LOG2_E
            for hd in range(M_HEADS):
                ci = 2 * M_HEADS * d + hd
                cf = ci + M_HEADS
                sl = slice(hd * M_DK, (hd + 1) * M_DK)
                r = d * M_HEADS + hd
                q = q_ref[0, tok, sl]
                k = k_ref[0, tok, sl]
                v_aug = jnp.concatenate([v_ref[0, sl, tok], ones_rows], 0)
                br = cum_ext[cf:cf + 1, :L]
                btot = cum_ext[cf:cf + 1, L:]
                ir = grow[ci:ci + 1, :]
                m = m_s[r:r + 1, :]
                ct = c_s[r]
                if emit:
                    cc = gcol[:, ci:ci + 1] - cum_col[:, cf:cf + 1]
                    dmt = jnp.where(mask, cc + br, NEG)
                    mt = jnp.maximum(m + br, jnp.max(dmt, 0, keepdims=True))
                    big = _dot_nt(jnp.concatenate([k, ct.astype(BF16)], 0), q)
                    pt = big[:L] * jnp.exp2(dmt - mt)
                    num = jnp.exp2(m + br - mt) * big[L:] + _dot(v_aug, pt.astype(BF16))
                    inv = 1.0 / jnp.maximum(jnp.abs(num[M_DV:M_DV + 1]), jnp.exp2(-mt))
                    h_ref[0, sl, tok] = num[:M_DV] * inv
                src = btot + ir - br
                m_new = jnp.maximum(m + btot, jnp.max(src, -1, keepdims=True))
                ws = jnp.exp2(src - m_new).astype(BF16)
                wc = jnp.exp2(m + btot - m_new)
                c_s[r] = wc * ct + _dot(v_aug * ws, k)
                m_s[r:r + 1, :] = m_new

    @pl.when(j == pl.num_programs(1) - 1)
    def _():
        cout_ref[0] = c_s[...]
        mout_ref[0] = m_s[...]


def _mlstm_scan(mqk, mvt, small, gt, c0, m0, emit, tb):
    b, _, t = mvt.shape
    nb = t // tb
    consts = _mlstm_cumsum_matrices(False) + _mlstm_cumsum_matrices(True)
    fw = lambda w, cblk: pl.BlockSpec((1, tb, w), lambda i, j: (i, j, cblk))
    bw = lambda w, cblk: pl.BlockSpec((1, tb, w), lambda i, j: (i, nb - 1 - j, cblk))
    ftf = lambda w: pl.BlockSpec((1, w, tb), lambda i, j: (i, 0, j))
    ftb = lambda w: pl.BlockSpec((1, w, tb), lambda i, j: (i, 0, nb - 1 - j))
    const = lambda a: pl.BlockSpec(a.shape, lambda i, j: (0,) * a.ndim)
    cst = pl.BlockSpec((1, 2 * M_HEADS, M_AUG, M_DK), lambda i, j: (i, 0, 0, 0))
    mst = pl.BlockSpec((1, 2 * M_HEADS, LANE), lambda i, j: (i, 0, 0))
    in_specs = [fw(M_QK, 0), fw(M_QK, 1), ftf(M_WIDTH), fw(SMALL_COLS, 0), ftf(N_GATES),
                bw(M_QK, 0), bw(M_QK, 1), ftb(M_WIDTH), bw(SMALL_COLS, 0), ftb(N_GATES)]
    in_specs += [const(a) for a in consts] + [cst, mst]
    out_specs = [cst, mst]
    out_shape = [jax.ShapeDtypeStruct(c0.shape, F32), jax.ShapeDtypeStruct(m0.shape, F32)]
    if emit:
        out_specs = [ftf(M_WIDTH), ftb(M_WIDTH)] + out_specs
        out_shape = [jax.ShapeDtypeStruct((b, M_WIDTH, t), F32)] * 2 + out_shape
    return pl.pallas_call(
        functools.partial(_mlstm_kernel, emit, tb // M_CHUNK),
        grid=(b, nb),
        in_specs=in_specs,
        out_specs=out_specs,
        out_shape=out_shape,
        scratch_shapes=[pltpu.VMEM((2 * M_HEADS, M_AUG, M_DK), F32), pltpu.VMEM((2 * M_HEADS, LANE), F32)],
        compiler_params=_params("parallel", "arbitrary"),
        name="mlstm_scan" if emit else "mlstm_ctx_state",
    )(mqk, mqk, mvt, small, gt, mqk, mqk, mvt, small, gt, *consts, c0, m0)


G_LEVELS = (32, 16, 8, 4, 2, 1)


def _gla_select_matrices(reverse):
    L = G_CHUNK
    t = np.arange(L)
    u = np.arange(L)[None, :]
    tt = t[:, None]
    mats = [(u <= tt), (u > tt)]
    for h in G_LEVELS:
        start = (tt // (2 * h)) * (2 * h)
        second = tt >= start + h
        m = np.where(second, (u >= start + h) & (u <= tt), (u > tt) & (u < start + h))
        mats.append(m)
    m = np.concatenate([x.astype(np.float32) for x in mats], 0)
    if reverse:
        m = m.reshape(len(mats), L, L)[:, ::-1, ::-1].reshape(len(mats) * L, L)
    return jnp.asarray(np.concatenate([m, m, m], 1), BF16)


def _gla_pair_masks(reverse):
    L = G_CHUNK
    s = np.arange(L)[:, None]
    t = np.arange(L)[None, :]
    masks = [s == t]
    for h in G_LEVELS:
        same = (s // (2 * h)) == (t // (2 * h))
        masks.append(same & ((t // h) % 2 == 1) & ((s // h) % 2 == 0))
    m = np.stack(masks).astype(np.float32)
    if reverse:
        m = m[:, ::-1, ::-1]
    return jnp.asarray(np.tile(m, (1, 1, G_HEADS)), F32)


def _gla_head_mask():
    r = np.arange(G_HEADS * G_CHUNK)[:, None] // G_CHUNK
    c = np.arange(G_QK)[None, :] // G_DK
    return jnp.asarray((r == c).astype(np.float32), BF16)


def _gla_kernel(emit, nch, qf_ref, kf_ref, vf_ref, lrf_ref, qb_ref, kb_ref, vb_ref, lrb_ref,
                self_ref, selb_ref, mkf_ref, mkb_ref, hm_ref, gw_ref, gb_ref, s0_ref, *rest):
    if emit:
        of_ref, ob_ref, sout_ref, s_s = rest
    else:
        sout_ref, s_s = rest
        of_ref = ob_ref = None
    j = pl.program_id(1)
    L = G_CHUNK

    @pl.when(j == 0)
    def _():
        s_s[...] = s0_ref[0]

    hm = hm_ref[...]
    stack = lambda a: jnp.concatenate([a] * G_HEADS, 0)

    for d in range(2):
        q_ref, k_ref, v_ref, lr_ref, sel_ref, mk_ref, o_ref = (
            (qf_ref, kf_ref, vf_ref, lrf_ref, self_ref, mkf_ref, of_ref) if d == 0 else
            (qb_ref, kb_ref, vb_ref, lrb_ref, selb_ref, mkb_ref, ob_ref))
        lr = lr_ref[0][:, N_GATES + d * G_RANK:N_GATES + (d + 1) * G_RANK].astype(BF16)
        loga = _log_sigmoid(_dot(lr, gw_ref[d]) + gb_ref[d]) * (LOG2_E / G_TAU)
        pieces = _split3(loga)
        a3 = [jnp.concatenate([p[c * L:(c + 1) * L] for p in pieces], 0) for c in range(nch)]
        expo = jnp.exp2(_dot(sel_ref[...], jnp.concatenate(a3, 1)))
        for c in (range(nch) if d == 0 else reversed(range(nch))):
            rows = slice(c * L, (c + 1) * L)
            cols = slice(c * G_QK, (c + 1) * G_QK)
            k = k_ref[0, rows, :]
            v_st = jnp.concatenate([v_ref[0, rows, hd * G_DV:(hd + 1) * G_DV] for hd in range(G_HEADS)], 0)
            e = lambda i: expo[i * L:(i + 1) * L, cols].astype(BF16)
            kd_st = stack(k * e(1)) * hm
            e_tot = expo[L - 1:L, cols] if d == 0 else expo[0:1, cols]
            st = s_s[d]
            if emit:
                qs = stack(q_ref[0, rows, :]) * hm
                att_t = mk_ref[0] * _dot_nt(k, qs)
                for lv in range(len(G_LEVELS)):
                    ev = e(2 + lv)
                    att_t = att_t + mk_ref[1 + lv] * _dot_nt(k * ev, qs * stack(ev))
                inter_t = _dot_nt(st.astype(BF16), qs * stack(e(0)))
                att_st = stack(att_t.astype(BF16)) * hm
                res = _dot_tn(v_st, jnp.concatenate([att_st, kd_st], 1))
                o = jnp.transpose(inter_t + res[:, :G_HEADS * L])
                for hd in range(G_HEADS):
                    o_ref[0, rows, hd * G_DV:(hd + 1) * G_DV] = o[hd * L:(hd + 1) * L]
                u_t = res[:, G_HEADS * L:]
            else:
                u_t = _dot_tn(v_st, kd_st)
            s_s[d] = e_tot * st + u_t

    @pl.when(j == pl.num_programs(1) - 1)
    def _():
        sout_ref[0] = s_s[...]


def _gla_scan(gq, gk, gv, small, g_gate_w, g_gate_b, s0, emit, tb):
    b, t, _ = gv.shape
    nb = t // tb
    consts = (_gla_select_matrices(False), _gla_select_matrices(True),
              _gla_pair_masks(False), _gla_pair_masks(True), _gla_head_mask(), g_gate_w, g_gate_b)
    fw = lambda w: pl.BlockSpec((1, tb, w), lambda i, j: (i, j, 0))
    bw = lambda w: pl.BlockSpec((1, tb, w), lambda i, j: (i, nb - 1 - j, 0))
    const = lambda a: pl.BlockSpec(a.shape, lambda i, j: (0,) * a.ndim)
    sst = pl.BlockSpec((1, 2, G_DV, G_QK), lambda i, j: (i, 0, 0, 0))
    in_specs = [fw(G_QK), fw(G_QK), fw(G_WIDTH), fw(SMALL_COLS),
                bw(G_QK), bw(G_QK), bw(G_WIDTH), bw(SMALL_COLS)] + [const(a) for a in consts] + [sst]
    out_specs = [sst]
    out_shape = [jax.ShapeDtypeStruct(s0.shape, F32)]
    if emit:
        out_specs = [fw(G_WIDTH), bw(G_WIDTH)] + out_specs
        out_shape = [jax.ShapeDtypeStruct((b, t, G_WIDTH), F32)] * 2 + out_shape
    return pl.pallas_call(
        functools.partial(_gla_kernel, emit, tb // G_CHUNK),
        grid=(b, nb),
        in_specs=in_specs,
        out_specs=out_specs,
        out_shape=out_shape,
        scratch_shapes=[pltpu.VMEM((2, G_DV, G_QK), F32)],
        compiler_params=_params("parallel", "arbitrary"),
        name="gla_scan" if emit else "gla_ctx_state",
    )(gq, gk, gv, small, gq, gk, gv, small, *consts, s0)


FFN_CHUNK = 256


def _head_norm(h, nheads, width):
    parts = []
    for hd in range(nheads):
        blk = h[:, hd * width:(hd + 1) * width]
        parts.append(blk * lax.rsqrt(jnp.mean(blk * blk, -1, keepdims=True) + EPS))
    return jnp.concatenate(parts, 1)


def _out_kernel(d_ff, x_ref, hf_ref, hb_ref, of_ref, ob_ref, mo_ref, gr_ref, mod_ref,
                mng_ref, gng_ref, gffn_ref, gfin_ref, wout_ref, wgu_ref, wdown_ref, o_ref):
    x = x_ref[0]
    mod = mod_ref[0]
    hm_t = hf_ref[0] + hb_ref[0]
    parts = []
    for hd in range(M_HEADS):
        blk = hm_t[hd * M_DV:(hd + 1) * M_DV]
        parts.append(blk * lax.rsqrt(jnp.mean(blk * blk, 0, keepdims=True) + EPS))
    hm_t = jnp.concatenate(parts, 0) * mng_ref[...] * _sigmoid(mo_ref[0].astype(F32))
    gr = gr_ref[0].astype(F32)
    hg = _head_norm(of_ref[0] + ob_ref[0], G_HEADS, G_DV) * gng_ref[...] * (gr * _sigmoid(gr))
    mixed = _dot_tn(hm_t.astype(BF16), wout_ref[:M_WIDTH, :]) + _dot(hg.astype(BF16), wout_ref[M_WIDTH:, :])
    x1 = x + mod[0:1] * mixed
    h2 = x1 * lax.rsqrt(jnp.mean(x1 * x1, -1, keepdims=True) + EPS) * gffn_ref[...]
    h2 = (h2 * (1.0 + mod[2:3]) + mod[1:2]).astype(BF16)
    acc = jnp.zeros_like(x1)
    for c0 in range(0, d_ff, FFN_CHUNK):
        a = _dot(h2, wgu_ref[:, c0:c0 + FFN_CHUNK])
        g = _dot(h2, wgu_ref[:, d_ff + c0:d_ff + c0 + FFN_CHUNK])
        act = (a * _sigmoid(a) * g).astype(BF16)
        acc = acc + _dot(act, wdown_ref[c0:c0 + FFN_CHUNK, :])
    x2 = x1 + mod[3:4] * acc
    o_ref[0] = x2 * lax.rsqrt(jnp.mean(x2 * x2, -1, keepdims=True) + EPS) * gfin_ref[...]


def _out_block(x, hf, hb, of, ob, mo, gr, mod4, mng, gng, gffn, gfin, wout, wgu, wdown, tm):
    b, t, d = x.shape
    d_ff = wdown.shape[0]
    tok = lambda w: pl.BlockSpec((1, tm, w), lambda i, j: (i, j, 0))
    feat = lambda w: pl.BlockSpec((1, w, tm), lambda i, j: (i, 0, j))
    const = lambda a: pl.BlockSpec(a.shape, lambda i, j: (0,) * a.ndim, pipeline_mode=pl.Buffered(1))
    return pl.pallas_call(
        functools.partial(_out_kernel, d_ff),
        grid=(b, t // tm),
        in_specs=[tok(d), feat(M_WIDTH), feat(M_WIDTH), tok(G_WIDTH), tok(G_WIDTH), feat(M_WIDTH), tok(G_WIDTH),
                  pl.BlockSpec((1, 4, d), lambda i, j: (i, 0, 0)),
                  const(mng), const(gng), const(gffn), const(gfin), const(wout), const(wgu), const(wdown)],
        out_specs=tok(d),
        out_shape=jax.ShapeDtypeStruct((b, t, d), F32),
        compiler_params=_params("parallel", "arbitrary"),
        name="out_ffn",
    )(x, hf, hb, of, ob, mo, gr, mod4, mng, gng, gffn, gfin, wout, wgu, wdown)


def _largest_tile(t, cap):
    tm = min(t, cap)
    while t % tm:
        tm //= 2
    return tm


def kernel(x, c, ctx, c_ctx, w_ada, b_ada, g_mix, w_in, conv_w, m_gate_b, m_norm_g, g_gate_w, g_gate_b,
           g_norm_g, w_out, g_ffn, w_gu, w_down, g_final):
    b, t, d = x.shape
    tc = ctx.shape[1]
    assert w_ada.shape[0] == 1, "single layer block"
    assert t % (GRID_W * 2) == 0 and tc % M_CHUNK == 0

    pad = (-(b + 1)) % 8
    c_all = jnp.concatenate([c, c_ctx[None, :], jnp.zeros((pad, d), F32)], 0)
    mod = _modulation(c_all, w_ada[0], b_ada[0])
    mod_x = mod[:b].reshape(b, 6, d)
    mod_c = jnp.broadcast_to(mod[b].reshape(1, 6, d), (b, 6, d))

    w = w_in[0]
    sizes = (2 * M_QK, M_WIDTH, M_WIDTH, N_GATES, G_QK, G_QK, G_WIDTH, G_WIDTH, 2 * G_RANK)
    offs = np.concatenate([[0], np.cumsum(sizes)])
    seg = lambda i: w[:, offs[i]:offs[i + 1]]
    wtok = jnp.concatenate([seg(0), seg(4), seg(5), seg(6), seg(7)], 1).astype(BF16)
    wfeat = jnp.transpose(jnp.concatenate([seg(1), seg(2)], 1)).astype(BF16)
    n_small = N_GATES + 2 * G_RANK
    wsmall = jnp.concatenate([seg(3), seg(8), jnp.zeros((d, SMALL_COLS - n_small), F32)], 1).astype(BF16)
    wgt = jnp.transpose(seg(3)).astype(BF16)
    bsm = jnp.concatenate([m_gate_b[0], jnp.zeros((SMALL_COLS - N_GATES,), F32)]).reshape(1, SMALL_COLS)
    bgt = m_gate_b[0].reshape(N_GATES, 1)
    g_mix2 = g_mix[0].reshape(1, d)
    conv_w9 = conv_w[0].reshape(9, 2 * M_QK)
    kscale = jnp.concatenate([jnp.ones((M_QK,), F32), jnp.full((M_QK,), M_DK ** -0.5, F32)]).reshape(1, 2 * M_QK)
    ggw = g_gate_w[0].astype(BF16)
    ggb = g_gate_b[0].reshape(2, 1, G_QK)

    def features(inp, m6, tm):
        mqk, gq, gk, gv, gr, mvt, mot, small, gt = _features(
            inp, m6[:, 0:1], m6[:, 1:2], g_mix2, wtok, wfeat, wsmall, wgt, bsm, bgt, tm)
        return dict(mqk=mqk, gq=gq, gk=gk, gv=gv, gr=gr, mvt=mvt, mot=mot, small=small, gt=gt)

    fc = features(ctx, mod_c, _largest_tile(tc, 512))
    mqk_c = _conv(fc["mqk"], conv_w9, kscale, tc, 1, 256)
    c0 = jnp.zeros((b, 2 * M_HEADS, M_AUG, M_DK), F32)
    m0 = jnp.full((b, 2 * M_HEADS, LANE), NEG, F32)
    c_ctx_state, m_ctx_state = _mlstm_scan(mqk_c, fc["mvt"], fc["small"], fc["gt"], c0, m0, False,
                                           _largest_tile(tc, 256))
    s0 = jnp.zeros((b, 2, G_DV, G_QK), F32)
    (s_ctx_state,) = _gla_scan(fc["gq"], fc["gk"], fc["gv"], fc["small"], ggw, ggb, s0, False,
                               _largest_tile(tc, 256))

    fx = features(x, mod_x, _largest_tile(t, 512))
    rows = t // GRID_W
    mqk_x = _conv(fx["mqk"], conv_w9, kscale, GRID_W, _largest_tile(rows, 4), 256)
    hf_t, hb_t, _, _ = _mlstm_scan(mqk_x, fx["mvt"], fx["small"], fx["gt"], c_ctx_state, m_ctx_state, True,
                                   _largest_tile(t, 256))
    of, ob, _ = _gla_scan(fx["gq"], fx["gk"], fx["gv"], fx["small"], ggw, ggb, s_ctx_state, True,
                          _largest_tile(t, 256))

    tm = _largest_tile(t, 256)
    mod4 = jnp.stack([mod_x[:, 2], mod_x[:, 3], mod_x[:, 4], mod_x[:, 5]], 1)
    mng = jnp.broadcast_to(m_norm_g[0].reshape(M_WIDTH, 1), (M_WIDTH, tm))
    return _out_block(x, hf_t, hb_t, of, ob, fx["mot"], fx["gr"], mod4,
                      mng, g_norm_g[0].reshape(1, -1),
                      g_ffn[0].reshape(1, d), g_final.reshape(1, d),
                      w_out[0].astype(BF16), w_gu[0].astype(BF16), w_down[0].astype(BF16), tm)
```

```python
import functools

import numpy as np
import jax
import jax.numpy as jnp
from jax import lax
from jax.experimental import pallas as pl
from jax.experimental.pallas import tpu as pltpu

F32 = jnp.float32
BF16 = jnp.bfloat16

GRID_W = 64
M_HEADS = 4
M_DK = 128
M_DV = 128
M_QK = M_HEADS * M_DK
M_WIDTH = M_HEADS * M_DV
M_CHUNK = 128
G_HEADS = 4
G_DK = 64
G_DV = 128
G_QK = G_HEADS * G_DK
G_WIDTH = G_HEADS * G_DV
G_RANK = 16
G_TAU = 16.0
G_CHUNK = 64
EPS = 1e-6
NEG = -1e30
LOG2_E = 1.4426950408889634

LANE = 128
VMEM_LIMIT = 56 * 1024 * 1024


def _params(*sem):
    return pltpu.CompilerParams(dimension_semantics=sem, vmem_limit_bytes=VMEM_LIMIT)


def _sigmoid(x):
    return 1.0 / (1.0 + jnp.exp(-x))


def _log_sigmoid(x):
    return jnp.minimum(x, 0.0) - jnp.log(1.0 + jnp.exp(-jnp.abs(x)))


def _split3(x):
    x1 = x.astype(BF16)
    r1 = x - x1.astype(F32)
    x2 = r1.astype(BF16)
    x3 = (r1 - x2.astype(F32)).astype(BF16)
    return x1, x2, x3


def _dot(a, b):
    return jnp.dot(a, b, preferred_element_type=F32)


def _dot_nt(a, b):
    return lax.dot_general(a, b, (((1,), (1,)), ((), ())), preferred_element_type=F32)


def _dot_tn(a, b):
    return lax.dot_general(a, b, (((0,), (0,)), ((), ())), preferred_element_type=F32)


def _sel_left(m01, x):
    x1, x2, x3 = _split3(x)
    return _dot(m01, x1) + _dot(m01, x2) + _dot(m01, x3)


def _sel_right(x, m01):
    x1, x2, x3 = _split3(x)
    return _dot(x1, m01) + _dot(x2, m01) + _dot(x3, m01)


def _mod_kernel(c_ref, w_ref, b_ref, o_ref):
    cs = c_ref[...]
    a = (cs * _sigmoid(cs)).astype(BF16)
    o_ref[...] = _dot(a, w_ref[...].astype(BF16)) + b_ref[...]


def _modulation(c_all, w_ada, b_ada):
    rows, d = c_all.shape
    n = w_ada.shape[1]
    bn = n // 6
    return pl.pallas_call(
        _mod_kernel,
        grid=(n // bn,),
        in_specs=[pl.BlockSpec((rows, d), lambda j: (0, 0)),
                  pl.BlockSpec((d, bn), lambda j: (0, j)),
                  pl.BlockSpec((1, bn), lambda j: (0, j))],
        out_specs=pl.BlockSpec((rows, bn), lambda j: (0, j)),
        out_shape=jax.ShapeDtypeStruct((rows, n), F32),
        compiler_params=_params("arbitrary"),
        name="adaln_mod",
    )(c_all, w_ada, b_ada.reshape(1, n))


_TOK_SIZES = (("mqk", 2 * M_QK), ("gq", G_QK), ("gk", G_QK), ("gv", G_WIDTH), ("gr", G_WIDTH))
_FEAT_SIZES = (("mv", M_WIDTH), ("mo", M_WIDTH))
N_GATES = 4 * M_HEADS
SMALL_COLS = LANE


CONV_CB = 256


def _conv_silu(ext, w9, cols, nrow):
    tot = (nrow + 2) * cols
    col = lax.broadcasted_iota(jnp.int32, ext.shape, 0) % cols
    left = jnp.where(col == 0, 0.0, pltpu.roll(ext, 1, 0))
    right = jnp.where(col == cols - 1, 0.0, pltpu.roll(ext, tot - 1, 0))
    srcs = (left, ext, right)
    acc = None
    for di in range(3):
        for dj in range(3):
            term = srcs[dj][di * cols:(di + nrow) * cols] * w9[di * 3 + dj:di * 3 + dj + 1]
            acc = term if acc is None else acc + term
    return acc * _sigmoid(acc)


def _feat_kernel(cols, x_ref, xn_ref, sh_ref, sc_ref, g_ref, wtok_ref, wfeat_ref, wsmall_ref, wgt_ref,
                 bsm_ref, bgt_ref, cw_ref, ksc_ref,
                 mqk_ref, gq_ref, gk_ref, gv_ref, gr_ref, mvt_ref, mot_ref, small_ref, gt_ref, prev_s):
    j = pl.program_id(1)
    last = pl.num_programs(1) - 1
    tm = x_ref.shape[1]

    def norm_mod(x):
        h = x * lax.rsqrt(jnp.mean(x * x, -1, keepdims=True) + EPS) * g_ref[...]
        return (h * (1.0 + sc_ref[0]) + sh_ref[0]).astype(BF16)

    hb = norm_mod(x_ref[0])
    hext = jnp.concatenate([hb, norm_mod(xn_ref[0])], 0)

    @pl.when(j == 0)
    def _():
        prev_s[...] = jnp.zeros_like(prev_s)

    outs = dict(gq=gq_ref, gk=gk_ref, gv=gv_ref, gr=gr_ref, mv=mvt_ref, mo=mot_ref)
    tasks = []

    def tok_task(name, off, size):
        def run():
            z = _dot(hb, wtok_ref[:, off:off + size])
            if name == "gq":
                z = z * (G_DK ** -0.5)
            outs[name][0] = z.astype(BF16)
        return run

    def feat_task(name, off, size):
        def run():
            outs[name][0] = _dot_nt(wfeat_ref[off:off + size, :], hb).astype(BF16)
        return run

    off = 2 * M_QK
    for name, size in _TOK_SIZES[1:]:
        tasks.append(tok_task(name, off, size))
        off += size
    off = 0
    for name, size in _FEAT_SIZES:
        tasks.append(feat_task(name, off, size))
        off += size

    nblk = 2 * M_QK // CONV_CB
    for i in range(nblk):
        cs = slice(i * CONV_CB, (i + 1) * CONV_CB)
        z = _dot(hext, wtok_ref[:, cs])
        for task in tasks[i * len(tasks) // nblk:(i + 1) * len(tasks) // nblk]:
            task()
        below = jnp.where(j < last, z[tm:], 0.0)
        ext = jnp.concatenate([prev_s[:, cs], z[:tm], below], 0)
        prev_s[:, cs] = z[tm - cols:tm]
        mqk_ref[0, :, cs] = (_conv_silu(ext, cw_ref[:, cs], cols, tm // cols) * ksc_ref[:, cs]).astype(BF16)
    small_ref[0] = _dot(hb, wsmall_ref[...]) + bsm_ref[...]
    gt_ref[0] = _dot_nt(wgt_ref[...], hb) + bgt_ref[...]


def _features(x, shift, scale, consts, cols, tm):
    b, t, d = x.shape
    nt = t // tm
    per = tm // cols
    tok = lambda w: pl.BlockSpec((1, tm, w), lambda i, j: (i, j, 0))
    feat = lambda w: pl.BlockSpec((1, w, tm), lambda i, j: (i, 0, j))
    below = pl.BlockSpec((1, cols, d), lambda i, j: (i, jnp.minimum((j + 1) * per, t // cols - 1), 0))
    const = lambda a: pl.BlockSpec(a.shape, lambda i, j: (0,) * a.ndim)
    vec = pl.BlockSpec((1, 1, d), lambda i, j: (i, 0, 0))
    out_shape = [jax.ShapeDtypeStruct((b, t, s), BF16) for _, s in _TOK_SIZES]
    out_specs = [tok(s) for _, s in _TOK_SIZES]
    out_shape += [jax.ShapeDtypeStruct((b, s, t), BF16) for _, s in _FEAT_SIZES]
    out_specs += [feat(s) for _, s in _FEAT_SIZES]
    out_shape += [jax.ShapeDtypeStruct((b, t, SMALL_COLS), F32), jax.ShapeDtypeStruct((b, N_GATES, t), F32)]
    out_specs += [tok(SMALL_COLS), feat(N_GATES)]
    return pl.pallas_call(
        functools.partial(_feat_kernel, cols),
        grid=(b, nt),
        in_specs=[tok(d), below, vec, vec] + [const(a) for a in consts],
        out_specs=out_specs,
        out_shape=out_shape,
        scratch_shapes=[pltpu.VMEM((cols, 2 * M_QK), F32)],
        compiler_params=_params("parallel", "arbitrary"),
        name="features",
    )(x, x, shift, scale, *consts)


M_AUG = M_DV + 16


def _mlstm_cumsum_matrices(reverse):
    L = M_CHUNK
    low = np.tril(np.ones((L, L), np.float32))
    tri = low.T if reverse else low
    col = np.concatenate([tri, tri, tri], 1)
    row = np.tile(np.concatenate([tri.T, np.ones((L, L), np.float32)], 1), (3, 1))
    return jnp.asarray(col, BF16), jnp.asarray(row, BF16)


def _mlstm_kernel(emit, nch, qf_ref, kf_ref, vf_ref, gcf_ref, grf_ref, qb_ref, kb_ref, vb_ref, gcb_ref, grb_ref,
                  colf_ref, rowf_ref, colb_ref, rowb_ref, c0_ref, m0_ref, *rest):
    if emit:
        hf_ref, hb_ref, cout_ref, mout_ref, c_s, m_s = rest
    else:
        cout_ref, mout_ref, c_s, m_s = rest
        hf_ref = hb_ref = None
    j = pl.program_id(1)
    L = M_CHUNK

    @pl.when(j == 0)
    def _():
        c_s[...] = c0_ref[0]
        m_s[...] = m0_ref[0]

    sid = lax.broadcasted_iota(jnp.int32, (L, L), 0)
    tid = lax.broadcasted_iota(jnp.int32, (L, L), 1)
    ones_rows = jnp.ones((M_AUG - M_DV, L), BF16)

    for d in range(2):
        q_ref, k_ref, v_ref, gc_ref, gr_ref, col_ref, row_ref, h_ref = (
            (qf_ref, kf_ref, vf_ref, gcf_ref, grf_ref, colf_ref, rowf_ref, hf_ref) if d == 0 else
            (qb_ref, kb_ref, vb_ref, gcb_ref, grb_ref, colb_ref, rowb_ref, hb_ref))
        mask = (sid <= tid) if d == 0 else (sid >= tid)
        for c in (range(nch) if d == 0 else reversed(range(nch))):
            tok = slice(c * L, (c + 1) * L)
            gcol = gc_ref[0, tok, :][:, :N_GATES]
            grow = gr_ref[0, :, tok]
            cum_col = _dot(col_ref[...], jnp.concatenate(_split3(_log_sigmoid(gcol) * LOG2_E), 0))
            cum_ext = _dot(jnp.concatenate(_split3(_log_sigmoid(grow) * LOG2_E), 1), row_ref[...])
            gcol = gcol * LOG2_E
            grow = grow * LOG2_E
            for hd in range(M_HEADS):
                ci = 2 * M_HEADS * d + hd
                cf = ci + M_HEADS
                sl = slice(hd * M_DK, (hd + 1) * M_DK)
                r = d * M_HEADS + hd
                q = q_ref[0, tok, sl]
                k = k_ref[0, tok, sl]
                v_aug = jnp.concatenate([v_ref[0, sl, tok], ones_rows], 0)
                br = cum_ext[cf:cf + 1, :L]
                btot = cum_ext[cf:cf + 1, L:]
                ir = grow[ci:ci + 1, :]
                m = m_s[r:r + 1, :]
                ct = c_s[r]
                if emit:
                    cc = gcol[:, ci:ci + 1] - cum_col[:, cf:cf + 1]
                    dmt = jnp.where(mask, cc + br, NEG)
                    mt = jnp.maximum(m + br, jnp.max(dmt, 0, keepdims=True))
                    big = _dot_nt(jnp.concatenate([k, ct.astype(BF16)], 0), q)
                    pt = big[:L] * jnp.exp2(dmt - mt)
                    num = jnp.exp2(m + br - mt) * big[L:] + _dot(v_aug, pt.astype(BF16))
                    inv = 1.0 / jnp.maximum(jnp.abs(num[M_DV:M_DV + 1]), jnp.exp2(-mt))
                    h_ref[0, sl, tok] = num[:M_DV] * inv
                src = btot + ir - br
                m_new = jnp.maximum(m + btot, jnp.max(src, -1, keepdims=True))
                ws = jnp.exp2(src - m_new).astype(BF16)
                wc = jnp.exp2(m + btot - m_new)
                c_s[r] = wc * ct + _dot(v_aug * ws, k)
                m_s[r:r + 1, :] = m_new

    @pl.when(j == pl.num_programs(1) - 1)
    def _():
        cout_ref[0] = c_s[...]
        mout_ref[0] = m_s[...]


def _mlstm_scan(mqk, mvt, small, gt, c0, m0, emit, tb):
    b, _, t = mvt.shape
    nb = t // tb
    consts = _mlstm_cumsum_matrices(False) + _mlstm_cumsum_matrices(True)
    fw = lambda w, cblk: pl.BlockSpec((1, tb, w), lambda i, j: (i, j, cblk))
    bw = lambda w, cblk: pl.BlockSpec((1, tb, w), lambda i, j: (i, nb - 1 - j, cblk))
    ftf = lambda w: pl.BlockSpec((1, w, tb), lambda i, j: (i, 0, j))
    ftb = lambda w: pl.BlockSpec((1, w, tb), lambda i, j: (i, 0, nb - 1 - j))
    const = lambda a: pl.BlockSpec(a.shape, lambda i, j: (0,) * a.ndim)
    cst = pl.BlockSpec((1, 2 * M_HEADS, M_AUG, M_DK), lambda i, j: (i, 0, 0, 0))
    mst = pl.BlockSpec((1, 2 * M_HEADS, LANE), lambda i, j: (i, 0, 0))
    in_specs = [fw(M_QK, 0), fw(M_QK, 1), ftf(M_WIDTH), fw(SMALL_COLS, 0), ftf(N_GATES),
                bw(M_QK, 0), bw(M_QK, 1), ftb(M_WIDTH), bw(SMALL_COLS, 0), ftb(N_GATES)]
    in_specs += [const(a) for a in consts] + [cst, mst]
    out_specs = [cst, mst]
    out_shape = [jax.ShapeDtypeStruct(c0.shape, F32), jax.ShapeDtypeStruct(m0.shape, F32)]
    if emit:
        out_specs = [ftf(M_WIDTH), ftb(M_WIDTH)] + out_specs
        out_shape = [jax.ShapeDtypeStruct((b, M_WIDTH, t), F32)] * 2 + out_shape
    return pl.pallas_call(
        functools.partial(_mlstm_kernel, emit, tb // M_CHUNK),
        grid=(b, nb),
        in_specs=in_specs,
        out_specs=out_specs,
        out_shape=out_shape,
        scratch_shapes=[pltpu.VMEM((2 * M_HEADS, M_AUG, M_DK), F32), pltpu.VMEM((2 * M_HEADS, LANE), F32)],
        compiler_params=_params("parallel", "arbitrary"),
        name="mlstm_scan" if emit else "mlstm_ctx_state",
    )(mqk, mqk, mvt, small, gt, mqk, mqk, mvt, small, gt, *consts, c0, m0)


G_LEVELS = (32, 16, 8, 4, 2, 1)


def _gla_select_matrices(reverse):
    L = G_CHUNK
    t = np.arange(L)
    u = np.arange(L)[None, :]
    tt = t[:, None]
    mats = [(u <= tt), (u > tt)]
    for h in G_LEVELS:
        start = (tt // (2 * h)) * (2 * h)
        second = tt >= start + h
        m = np.where(second, (u >= start + h) & (u <= tt), (u > tt) & (u < start + h))
        mats.append(m)
    m = np.concatenate([x.astype(np.float32) for x in mats], 0)
    if reverse:
        m = m.reshape(len(mats), L, L)[:, ::-1, ::-1].reshape(len(mats) * L, L)
    return jnp.asarray(np.concatenate([m, m, m], 1), BF16)


def _gla_pair_masks(reverse):
    L = G_CHUNK
    s = np.arange(L)[:, None]
    t = np.arange(L)[None, :]
    masks = [s == t]
    for h in G_LEVELS:
        same = (s // (2 * h)) == (t // (2 * h))
        masks.append(same & ((t // h) % 2 == 1) & ((s // h) % 2 == 0))
    m = np.stack(masks).astype(np.float32)
    if reverse:
        m = m[:, ::-1, ::-1]
    return jnp.asarray(np.tile(m, (1, 1, G_HEADS)), F32)


def _gla_head_mask():
    r = np.arange(G_HEADS * G_CHUNK)[:, None] // G_CHUNK
    c = np.arange(G_QK)[None, :] // G_DK
    return jnp.asarray((r == c).astype(np.float32), BF16)


def _gla_kernel(emit, nch, qf_ref, kf_ref, vf_ref, lrf_ref, qb_ref, kb_ref, vb_ref, lrb_ref,
                self_ref, selb_ref, mkf_ref, mkb_ref, hm_ref, gw_ref, gb_ref, s0_ref, *rest):
    if emit:
        of_ref, ob_ref, sout_ref, s_s = rest
    else:
        sout_ref, s_s = rest
        of_ref = ob_ref = None
    j = pl.program_id(1)
    L = G_CHUNK

    @pl.when(j == 0)
    def _():
        s_s[...] = s0_ref[0]

    hm = hm_ref[...]
    stack = lambda a: jnp.concatenate([a] * G_HEADS, 0)

    for d in range(2):
        q_ref, k_ref, v_ref, lr_ref, sel_ref, mk_ref, o_ref = (
            (qf_ref, kf_ref, vf_ref, lrf_ref, self_ref, mkf_ref, of_ref) if d == 0 else
            (qb_ref, kb_ref, vb_ref, lrb_ref, selb_ref, mkb_ref, ob_ref))
        lr = lr_ref[0][:, N_GATES + d * G_RANK:N_GATES + (d + 1) * G_RANK].astype(BF16)
        loga = _log_sigmoid(_dot(lr, gw_ref[d]) + gb_ref[d]) * (LOG2_E / G_TAU)
        pieces = _split3(loga)
        a3 = [jnp.concatenate([p[c * L:(c + 1) * L] for p in pieces], 0) for c in range(nch)]
        expo = jnp.exp2(_dot(sel_ref[...], jnp.concatenate(a3, 1)))
        for c in (range(nch) if d == 0 else reversed(range(nch))):
            rows = slice(c * L, (c + 1) * L)
            cols = slice(c * G_QK, (c + 1) * G_QK)
            k = k_ref[0, rows, :]
            v_st = jnp.concatenate([v_ref[0, rows, hd * G_DV:(hd + 1) * G_DV] for hd in range(G_HEADS)], 0)
            e = lambda i: expo[i * L:(i + 1) * L, cols].astype(BF16)
            kd_st = stack(k * e(1)) * hm
            e_tot = expo[L - 1:L, cols] if d == 0 else expo[0:1, cols]
            st = s_s[d]
            if emit:
                qs = stack(q_ref[0, rows, :]) * hm
                att_t = mk_ref[0] * _dot_nt(k, qs)
                for lv in range(len(G_LEVELS)):
                    ev = e(2 + lv)
                    att_t = att_t + mk_ref[1 + lv] * _dot_nt(k * ev, qs * stack(ev))
                inter_t = _dot_nt(st.astype(BF16), qs * stack(e(0)))
                att_st = stack(att_t.astype(BF16)) * hm
                res = _dot_tn(v_st, jnp.concatenate([att_st, kd_st], 1))
                o = jnp.transpose(inter_t + res[:, :G_HEADS * L])
                for hd in range(G_HEADS):
                    o_ref[0, rows, hd * G_DV:(hd + 1) * G_DV] = o[hd * L:(hd + 1) * L]
                u_t = res[:, G_HEADS * L:]
            else:
                u_t = _dot_tn(v_st, kd_st)
            s_s[d] = e_tot * st + u_t

    @pl.when(j == pl.num_programs(1) - 1)
    def _():
        sout_ref[0] = s_s[...]


def _gla_scan(gq, gk, gv, small, g_gate_w, g_gate_b, s0, emit, tb):
    b, t, _ = gv.shape
    nb = t // tb
    consts = (_gla_select_matrices(False), _gla_select_matrices(True),
              _gla_pair_masks(False), _gla_pair_masks(True), _gla_head_mask(), g_gate_w, g_gate_b)
    fw = lambda w: pl.BlockSpec((1, tb, w), lambda i, j: (i, j, 0))
    bw = lambda w: pl.BlockSpec((1, tb, w), lambda i, j: (i, nb - 1 - j, 0))
    const = lambda a: pl.BlockSpec(a.shape, lambda i, j: (0,) * a.ndim)
    sst = pl.BlockSpec((1, 2, G_DV, G_QK), lambda i, j: (i, 0, 0, 0))
    in_specs = [fw(G_QK), fw(G_QK), fw(G_WIDTH), fw(SMALL_COLS),
                bw(G_QK), bw(G_QK), bw(G_WIDTH), bw(SMALL_COLS)] + [const(a) for a in consts] + [sst]
    out_specs = [sst]
    out_shape = [jax.ShapeDtypeStruct(s0.shape, F32)]
    if emit:
        out_specs = [fw(G_WIDTH), bw(G_WIDTH)] + out_specs
        out_shape = [jax.ShapeDtypeStruct((b, t, G_WIDTH), F32)] * 2 + out_shape
    return pl.pallas_call(
        functools.partial(_gla_kernel, emit, tb // G_CHUNK),
        grid=(b, nb),
        in_specs=in_specs,
        out_specs=out_specs,
        out_shape=out_shape,
        scratch_shapes=[pltpu.VMEM((2, G_DV, G_QK), F32)],
        compiler_params=_params("parallel", "arbitrary"),
        name="gla_scan" if emit else "gla_ctx_state",
    )(gq, gk, gv, small, gq, gk, gv, small, *consts, s0)


FFN_CHUNK = 256


def _head_norm(h, nheads, width):
    parts = []
    for hd in range(nheads):
        blk = h[:, hd * width:(hd + 1) * width]
        parts.append(blk * lax.rsqrt(jnp.mean(blk * blk, -1, keepdims=True) + EPS))
    return jnp.concatenate(parts, 1)


def _out_kernel(d_ff, x_ref, hf_ref, hb_ref, of_ref, ob_ref, mo_ref, gr_ref, mod_ref,
                mng_ref, gng_ref, gffn_ref, gfin_ref, wout_ref, wgu_ref, wdown_ref, o_ref):
    x = x_ref[0]
    mod = mod_ref[0]
    hm_t = hf_ref[0] + hb_ref[0]
    parts = []
    for hd in range(M_HEADS):
        blk = hm_t[hd * M_DV:(hd + 1) * M_DV]
        parts.append(blk * lax.rsqrt(jnp.mean(blk * blk, 0, keepdims=True) + EPS))
    hm_t = jnp.concatenate(parts, 0) * mng_ref[...] * _sigmoid(mo_ref[0].astype(F32))
    gr = gr_ref[0].astype(F32)
    hg = _head_norm(of_ref[0] + ob_ref[0], G_HEADS, G_DV) * gng_ref[...] * (gr * _sigmoid(gr))
    mixed = _dot_tn(hm_t.astype(BF16), wout_ref[:M_WIDTH, :]) + _dot(hg.astype(BF16), wout_ref[M_WIDTH:, :])
    x1 = x + mod[0:1] * mixed
    h2 = x1 * lax.rsqrt(jnp.mean(x1 * x1, -1, keepdims=True) + EPS) * gffn_ref[...]
    h2 = (h2 * (1.0 + mod[2:3]) + mod[1:2]).astype(BF16)
    acc = jnp.zeros_like(x1)
    for c0 in range(0, d_ff, FFN_CHUNK):
        a = _dot(h2, wgu_ref[:, c0:c0 + FFN_CHUNK])
        g = _dot(h2, wgu_ref[:, d_ff + c0:d_ff + c0 + FFN_CHUNK])
        act = (a * _sigmoid(a) * g).astype(BF16)
        acc = acc + _dot(act, wdown_ref[c0:c0 + FFN_CHUNK, :])
    x2 = x1 + mod[3:4] * acc
    o_ref[0] = x2 * lax.rsqrt(jnp.mean(x2 * x2, -1, keepdims=True) + EPS) * gfin_ref[...]


def _out_block(x, hf, hb, of, ob, mo, gr, mod4, mng, gng, gffn, gfin, wout, wgu, wdown, tm):
    b, t, d = x.shape
    d_ff = wdown.shape[0]
    tok = lambda w: pl.BlockSpec((1, tm, w), lambda i, j: (i, j, 0))
    feat = lambda w: pl.BlockSpec((1, w, tm), lambda i, j: (i, 0, j))
    const = lambda a: pl.BlockSpec(a.shape, lambda i, j: (0,) * a.ndim, pipeline_mode=pl.Buffered(1))
    return pl.pallas_call(
        functools.partial(_out_kernel, d_ff),
        grid=(b, t // tm),
        in_specs=[tok(d), feat(M_WIDTH), feat(M_WIDTH), tok(G_WIDTH), tok(G_WIDTH), feat(M_WIDTH), tok(G_WIDTH),
                  pl.BlockSpec((1, 4, d), lambda i, j: (i, 0, 0)),
                  const(mng), const(gng), const(gffn), const(gfin), const(wout), const(wgu), const(wdown)],
        out_specs=tok(d),
        out_shape=jax.ShapeDtypeStruct((b, t, d), F32),
        compiler_params=_params("parallel", "arbitrary"),
        name="out_ffn",
    )(x, hf, hb, of, ob, mo, gr, mod4, mng, gng, gffn, gfin, wout, wgu, wdown)


def _largest_tile(t, cap):
    tm = min(t, cap)
    while t % tm:
        tm //= 2
    return tm


def kernel(x, c, ctx, c_ctx, w_ada, b_ada, g_mix, w_in, conv_w, m_gate_b, m_norm_g, g_gate_w, g_gate_b,
           g_norm_g, w_out, g_ffn, w_gu, w_down, g_final):
    b, t, d = x.shape
    tc = ctx.shape[1]
    assert w_ada.shape[0] == 1, "single layer block"
    assert t % (GRID_W * 2) == 0 and tc % M_CHUNK == 0

    pad = (-(b + 1)) % 8
    c_all = jnp.concatenate([c, c_ctx[None, :], jnp.zeros((pad, d), F32)], 0)
    mod = _modulation(c_all, w_ada[0], b_ada[0])
    mod_x = mod[:b].reshape(b, 6, d)
    mod_c = jnp.broadcast_to(mod[b].reshape(1, 6, d), (b, 6, d))

    w = w_in[0]
    sizes = (2 * M_QK, M_WIDTH, M_WIDTH, N_GATES, G_QK, G_QK, G_WIDTH, G_WIDTH, 2 * G_RANK)
    offs = np.concatenate([[0], np.cumsum(sizes)])
    seg = lambda i: w[:, offs[i]:offs[i + 1]]
    wtok = jnp.concatenate([seg(0), seg(4), seg(5), seg(6), seg(7)], 1).astype(BF16)
    wfeat = jnp.transpose(jnp.concatenate([seg(1), seg(2)], 1)).astype(BF16)
    n_small = N_GATES + 2 * G_RANK
    wsmall = jnp.concatenate([seg(3), seg(8), jnp.zeros((d, SMALL_COLS - n_small), F32)], 1).astype(BF16)
    wgt = jnp.transpose(seg(3)).astype(BF16)
    bsm = jnp.concatenate([m_gate_b[0], jnp.zeros((SMALL_COLS - N_GATES,), F32)]).reshape(1, SMALL_COLS)
    bgt = m_gate_b[0].reshape(N_GATES, 1)
    g_mix2 = g_mix[0].reshape(1, d)
    conv_w9 = conv_w[0].reshape(9, 2 * M_QK)
    kscale = jnp.concatenate([jnp.ones((M_QK,), F32), jnp.full((M_QK,), M_DK ** -0.5, F32)]).reshape(1, 2 * M_QK)
    ggw = g_gate_w[0].astype(BF16)
    ggb = g_gate_b[0].reshape(2, 1, G_QK)

    feat_consts = (g_mix2, wtok, wfeat, wsmall, wgt, bsm, bgt, conv_w9, kscale)

    def features(inp, m6, cols, tm):
        mqk, gq, gk, gv, gr, mvt, mot, small, gt = _features(inp, m6[:, 0:1], m6[:, 1:2], feat_consts, cols, tm)
        return dict(mqk=mqk, gq=gq, gk=gk, gv=gv, gr=gr, mvt=mvt, mot=mot, small=small, gt=gt)

    fc = features(ctx, mod_c, tc, tc)
    c0 = jnp.zeros((b, 2 * M_HEADS, M_AUG, M_DK), F32)
    m0 = jnp.full((b, 2 * M_HEADS, LANE), NEG, F32)
    c_ctx_state, m_ctx_state = _mlstm_scan(fc["mqk"], fc["mvt"], fc["small"], fc["gt"], c0, m0, False,
                                           _largest_tile(tc, 256))
    s0 = jnp.zeros((b, 2, G_DV, G_QK), F32)
    (s_ctx_state,) = _gla_scan(fc["gq"], fc["gk"], fc["gv"], fc["small"], ggw, ggb, s0, False,
                               _largest_tile(tc, 256))

    fx = features(x, mod_x, GRID_W, _largest_tile(t, 512))
    hf_t, hb_t, _, _ = _mlstm_scan(fx["mqk"], fx["mvt"], fx["small"], fx["gt"], c_ctx_state, m_ctx_state, True,
                                   _largest_tile(t, 256))
    of, ob, _ = _gla_scan(fx["gq"], fx["gk"], fx["gv"], fx["small"], ggw, ggb, s_ctx_state, True,
                          _largest_tile(t, 256))

    tm = _largest_tile(t, 512)
    mod4 =jnp.stack([mod_x[:, 2], mod_x[:, 3], mod_x[:, 4], mod_x[:, 5]], 1)
    mng = jnp.broadcast_to(m_norm_g[0].reshape(M_WIDTH, 1), (M_WIDTH, tm))
    return _out_block(x, hf_t, hb_t, of, ob, fx["mot"], fx["gr"], mod4,
                      mng, g_norm_g[0].reshape(1, -1),
                      g_ffn[0].reshape(1, d), g_final.reshape(1, d),
                      w_out[0].astype(BF16), w_gu[0].astype(BF16), w_down[0].astype(BF16), tm)
```

```python
import functools

import numpy as np
import jax
import jax.numpy as jnp
from jax import lax
from jax.experimental import pallas as pl
from jax.experimental.pallas import tpu as pltpu

F32 = jnp.float32
BF16 = jnp.bfloat16

GRID_W = 64
M_HEADS = 4
M_DK = 128
M_DV = 128
M_QK = M_HEADS * M_DK
M_WIDTH = M_HEADS * M_DV
M_CHUNK = 128
G_HEADS = 4
G_DK = 64
G_DV = 128
G_QK = G_HEADS * G_DK
G_WIDTH = G_HEADS * G_DV
G_RANK = 16
G_TAU = 16.0
G_CHUNK = 64
EPS = 1e-6
NEG = -1e30
LOG2_E = 1.4426950408889634

LANE = 128
VMEM_LIMIT = 56 * 1024 * 1024


def _params(*sem):
    return pltpu.CompilerParams(dimension_semantics=sem, vmem_limit_bytes=VMEM_LIMIT)


def _sigmoid(x):
    return 1.0 / (1.0 + jnp.exp(-x))


def _log_sigmoid(x):
    return jnp.minimum(x, 0.0) - jnp.log(1.0 + jnp.exp(-jnp.abs(x)))


def _split3(x):
    x1 = x.astype(BF16)
    r1 = x - x1.astype(F32)
    x2 = r1.astype(BF16)
    x3 = (r1 - x2.astype(F32)).astype(BF16)
    return x1, x2, x3


def _dot(a, b):
    return jnp.dot(a, b, preferred_element_type=F32)


def _dot_nt(a, b):
    return lax.dot_general(a, b, (((1,), (1,)), ((), ())), preferred_element_type=F32)


def _dot_tn(a, b):
    return lax.dot_general(a, b, (((0,), (0,)), ((), ())), preferred_element_type=F32)


def _sel_left(m01, x):
    x1, x2, x3 = _split3(x)
    return _dot(m01, x1) + _dot(m01, x2) + _dot(m01, x3)


def _sel_right(x, m01):
    x1, x2, x3 = _split3(x)
    return _dot(x1, m01) + _dot(x2, m01) + _dot(x3, m01)


def _mod_kernel(c_ref, w_ref, b_ref, o_ref):
    cs = c_ref[...]
    a = (cs * _sigmoid(cs)).astype(BF16)
    o_ref[...] = _dot(a, w_ref[...].astype(BF16)) + b_ref[...]


def _modulation(c_all, w_ada, b_ada):
    rows, d = c_all.shape
    n = w_ada.shape[1]
    bn = n // 6
    return pl.pallas_call(
        _mod_kernel,
        grid=(n // bn,),
        in_specs=[pl.BlockSpec((rows, d), lambda j: (0, 0)),
                  pl.BlockSpec((d, bn), lambda j: (0, j)),
                  pl.BlockSpec((1, bn), lambda j: (0, j))],
        out_specs=pl.BlockSpec((rows, bn), lambda j: (0, j)),
        out_shape=jax.ShapeDtypeStruct((rows, n), F32),
        compiler_params=_params("arbitrary"),
        name="adaln_mod",
    )(c_all, w_ada, b_ada.reshape(1, n))


_TOK_SIZES = (("mqk", 2 * M_QK), ("gq", G_QK), ("gk", G_QK), ("gv", G_WIDTH), ("gr", G_WIDTH))
_FEAT_SIZES = (("mv", M_WIDTH), ("mo", M_WIDTH))
N_GATES = 4 * M_HEADS
SMALL_COLS = LANE


CONV_CB = 256


def _conv_silu(ext, w9, cols, nrow):
    tot = (nrow + 2) * cols
    col = lax.broadcasted_iota(jnp.int32, ext.shape, 0) % cols
    left = jnp.where(col == 0, 0.0, pltpu.roll(ext, 1, 0))
    right = jnp.where(col == cols - 1, 0.0, pltpu.roll(ext, tot - 1, 0))
    srcs = (left, ext, right)
    acc = None
    for di in range(3):
        for dj in range(3):
            term = srcs[dj][di * cols:(di + nrow) * cols] * w9[di * 3 + dj:di * 3 + dj + 1]
            acc = term if acc is None else acc + term
    return acc * _sigmoid(acc)


def _feat_kernel(cols, x_ref, xn_ref, sh_ref, sc_ref, g_ref, wtok_ref, wfeat_ref, wsmall_ref, wgt_ref,
                 bsm_ref, bgt_ref, cw_ref, ksc_ref,
                 mqk_ref, gq_ref, gk_ref, gv_ref, gr_ref, mvt_ref, mot_ref, small_ref, gt_ref, prev_s):
    j = pl.program_id(1)
    last = pl.num_programs(1) - 1
    tm = x_ref.shape[1]

    def norm_mod(x):
        h = x * lax.rsqrt(jnp.mean(x * x, -1, keepdims=True) + EPS) * g_ref[...]
        return (h * (1.0 + sc_ref[0]) + sh_ref[0]).astype(BF16)

    hb = norm_mod(x_ref[0])
    hext = jnp.concatenate([hb, norm_mod(xn_ref[0])], 0)

    @pl.when(j == 0)
    def _():
        prev_s[...] = jnp.zeros_like(prev_s)

    outs = dict(gq=gq_ref, gk=gk_ref, gv=gv_ref, gr=gr_ref, mv=mvt_ref, mo=mot_ref)
    tasks = []

    def tok_task(name, off, size):
        def run():
            z = _dot(hb, wtok_ref[:, off:off + size])
            if name == "gq":
                z = z * (G_DK ** -0.5)
            outs[name][0] = z.astype(BF16)
        return run

    def feat_task(name, off, size):
        def run():
            outs[name][0] = _dot_nt(wfeat_ref[off:off + size, :], hb).astype(BF16)
        return run

    off = 2 * M_QK
    for name, size in _TOK_SIZES[1:]:
        tasks.append(tok_task(name, off, size))
        off += size
    off = 0
    for name, size in _FEAT_SIZES:
        tasks.append(feat_task(name, off, size))
        off += size

    nblk = 2 * M_QK // CONV_CB
    for i in range(nblk):
        cs = slice(i * CONV_CB, (i + 1) * CONV_CB)
        z = _dot(hext, wtok_ref[:, cs])
        for task in tasks[i * len(tasks) // nblk:(i + 1) * len(tasks) // nblk]:
            task()
        below = jnp.where(j < last, z[tm:], 0.0)
        ext = jnp.concatenate([prev_s[:, cs], z[:tm], below], 0)
        prev_s[:, cs] = z[tm - cols:tm]
        mqk_ref[0, :, cs] = (_conv_silu(ext, cw_ref[:, cs], cols, tm // cols) * ksc_ref[:, cs]).astype(BF16)
    small_ref[0] = _dot(hb, wsmall_ref[...]) + bsm_ref[...]
    gt_ref[0] = _dot_nt(wgt_ref[...], hb) + bgt_ref[...]


def _features(x, shift, scale, consts, cols, tm):
    b, t, d = x.shape
    nt = t // tm
    per = tm // cols
    tok = lambda w: pl.BlockSpec((1, tm, w), lambda i, j: (i, j, 0))
    feat = lambda w: pl.BlockSpec((1, w, tm), lambda i, j: (i, 0, j))
    below = pl.BlockSpec((1, cols, d), lambda i, j: (i, jnp.minimum((j + 1) * per, t // cols - 1), 0))
    const = lambda a: pl.BlockSpec(a.shape, lambda i, j: (0,) * a.ndim)
    vec = pl.BlockSpec((1, 1, d), lambda i, j: (i, 0, 0))
    out_shape = [jax.ShapeDtypeStruct((b, t, s), BF16) for _, s in _TOK_SIZES]
    out_specs = [tok(s) for _, s in _TOK_SIZES]
    out_shape += [jax.ShapeDtypeStruct((b, s, t), BF16) for _, s in _FEAT_SIZES]
    out_specs += [feat(s) for _, s in _FEAT_SIZES]
    out_shape += [jax.ShapeDtypeStruct((b, t, SMALL_COLS), F32), jax.ShapeDtypeStruct((b, N_GATES, t), F32)]
    out_specs += [tok(SMALL_COLS), feat(N_GATES)]
    return pl.pallas_call(
        functools.partial(_feat_kernel, cols),
        grid=(b, nt),
        in_specs=[tok(d), below, vec, vec] + [const(a) for a in consts],
        out_specs=out_specs,
        out_shape=out_shape,
        scratch_shapes=[pltpu.VMEM((cols, 2 * M_QK), F32)],
        compiler_params=_params("parallel", "arbitrary"),
        name="features",
    )(x, x, shift, scale, *consts)


M_AUG = M_DV + 16


def _mlstm_cumsum_matrices(reverse):
    L = M_CHUNK
    low = np.tril(np.ones((L, L), np.float32))
    tri = low.T if reverse else low
    col = np.concatenate([tri, tri, tri], 1)
    row = np.tile(np.concatenate([tri.T, np.ones((L, L), np.float32)], 1), (3, 1))
    return jnp.asarray(col, BF16), jnp.asarray(row, BF16)


def _mlstm_kernel(emit, nch, qf_ref, kf_ref, vf_ref, gcf_ref, grf_ref, qb_ref, kb_ref, vb_ref, gcb_ref, grb_ref,
                  colf_ref, rowf_ref, colb_ref, rowb_ref, c0_ref, m0_ref, *rest):
    if emit:
        hf_ref, hb_ref, cout_ref, mout_ref, c_s, m_s = rest
    else:
        cout_ref, mout_ref, c_s, m_s = rest
        hf_ref = hb_ref = None
    j = pl.program_id(1)
    L = M_CHUNK

    @pl.when(j == 0)
    def _():
        c_s[...] = c0_ref[0]
        m_s[...] = m0_ref[0]

    sid = lax.broadcasted_iota(jnp.int32, (L, L), 0)
    tid = lax.broadcasted_iota(jnp.int32, (L, L), 1)
    ones_rows = jnp.ones((M_AUG - M_DV, L), BF16)

    for d in range(2):
        q_ref, k_ref, v_ref, gc_ref, gr_ref, col_ref, row_ref, h_ref = (
            (qf_ref, kf_ref, vf_ref, gcf_ref, grf_ref, colf_ref, rowf_ref, hf_ref) if d == 0 else
            (qb_ref, kb_ref, vb_ref, gcb_ref, grb_ref, colb_ref, rowb_ref, hb_ref))
        mask = (sid <= tid) if d == 0 else (sid >= tid)
        for c in (range(nch) if d == 0 else reversed(range(nch))):
            tok = slice(c * L, (c + 1) * L)
            gcol = gc_ref[0, tok, :][:, :N_GATES]
            grow = gr_ref[0, :, tok]
            cum_col = _dot(col_ref[...], jnp.concatenate(_split3(_log_sigmoid(gcol) * LOG2_E), 0))
            cum_ext = _dot(jnp.concatenate(_split3(_log_sigmoid(grow) * LOG2_E), 1), row_ref[...])
            gcol = gcol * LOG2_E
            grow = grow * LOG2_E
            for hd in range(M_HEADS):
                ci = 2 * M_HEADS * d + hd
                cf = ci + M_HEADS
                sl = slice(hd * M_DK, (hd + 1) * M_DK)
                r = d * M_HEADS + hd
                q = q_ref[0, tok, sl]
                k = k_ref[0, tok, sl]
                v_aug = jnp.concatenate([v_ref[0, sl, tok], ones_rows], 0)
                br = cum_ext[cf:cf + 1, :L]
                btot = cum_ext[cf:cf + 1, L:]
                ir = grow[ci:ci + 1, :]
                m = m_s[r:r + 1, :]
                ct = c_s[r]
                if emit:
                    cc = gcol[:, ci:ci + 1] - cum_col[:, cf:cf + 1]
                    dmt = jnp.where(mask, cc + br, NEG)
                    mt = jnp.maximum(m + br, jnp.max(dmt, 0, keepdims=True))
                    big = _dot_nt(jnp.concatenate([k, ct.astype(BF16)], 0), q)
                    pt = big[:L] * jnp.exp2(dmt - mt)
                    num = jnp.exp2(m + br - mt) * big[L:] + _dot(v_aug, pt.astype(BF16))
                    inv = 1.0 / jnp.maximum(jnp.abs(num[M_DV:M_DV + 1]), jnp.exp2(-mt))
                    h_ref[0, sl, tok] = num[:M_DV] * inv
                src = btot + ir - br
                m_new = jnp.maximum(m + btot, jnp.max(src, -1, keepdims=True))
                ws = jnp.exp2(src - m_new).astype(BF16)
                wc = jnp.exp2(m + btot - m_new)
                c_s[r] = wc * ct + _dot(v_aug * ws, k)
                m_s[r:r + 1, :] = m_new

    @pl.when(j == pl.num_programs(1) - 1)
    def _():
        cout_ref[0] = c_s[...]
        mout_ref[0] = m_s[...]


def _mlstm_scan(mqk, mvt, small, gt, c0, m0, emit, tb):
    b, _, t = mvt.shape
    nb = t // tb
    consts = _mlstm_cumsum_matrices(False) + _mlstm_cumsum_matrices(True)
    fw = lambda w, cblk: pl.BlockSpec((1, tb, w), lambda i, j: (i, j, cblk))
    bw = lambda w, cblk: pl.BlockSpec((1, tb, w), lambda i, j: (i, nb - 1 - j, cblk))
    ftf = lambda w: pl.BlockSpec((1, w, tb), lambda i, j: (i, 0, j))
    ftb = lambda w: pl.BlockSpec((1, w, tb), lambda i, j: (i, 0, nb - 1 - j))
    const = lambda a: pl.BlockSpec(a.shape, lambda i, j: (0,) * a.ndim)
    cst = pl.BlockSpec((1, 2 * M_HEADS, M_AUG, M_DK), lambda i, j: (i, 0, 0, 0))
    mst = pl.BlockSpec((1, 2 * M_HEADS, LANE), lambda i, j: (i, 0, 0))
    in_specs = [fw(M_QK, 0), fw(M_QK, 1), ftf(M_WIDTH), fw(SMALL_COLS, 0), ftf(N_GATES),
                bw(M_QK, 0), bw(M_QK, 1), ftb(M_WIDTH), bw(SMALL_COLS, 0), ftb(N_GATES)]
    in_specs += [const(a) for a in consts] + [cst, mst]
    out_specs = [cst, mst]
    out_shape = [jax.ShapeDtypeStruct(c0.shape, F32), jax.ShapeDtypeStruct(m0.shape, F32)]
    if emit:
        out_specs = [ftf(M_WIDTH), ftb(M_WIDTH)] + out_specs
        out_shape = [jax.ShapeDtypeStruct((b, M_WIDTH, t), F32)] * 2 + out_shape
    return pl.pallas_call(
        functools.partial(_mlstm_kernel, emit, tb // M_CHUNK),
        grid=(b, nb),
        in_specs=in_specs,
        out_specs=out_specs,
        out_shape=out_shape,
        scratch_shapes=[pltpu.VMEM((2 * M_HEADS, M_AUG, M_DK), F32), pltpu.VMEM((2 * M_HEADS, LANE), F32)],
        compiler_params=_params("parallel", "arbitrary"),
        name="mlstm_scan" if emit else "mlstm_ctx_state",
    )(mqk, mqk, mvt, small, gt, mqk, mqk, mvt, small, gt, *consts, c0, m0)


G_LEVELS = (32, 16, 8, 4, 2, 1)


def _gla_select_matrices(reverse):
    L = G_CHUNK
    t = np.arange(L)
    u = np.arange(L)[None, :]
    tt = t[:, None]
    mats = [(u <= tt), (u > tt)]
    for h in G_LEVELS:
        start = (tt // (2 * h)) * (2 * h)
        second = tt >= start + h
        m = np.where(second, (u >= start + h) & (u <= tt), (u > tt) & (u < start + h))
        mats.append(m)
    m = np.concatenate([x.astype(np.float32) for x in mats], 0)
    if reverse:
        m = m.reshape(len(mats), L, L)[:, ::-1, ::-1].reshape(len(mats) * L, L)
    return jnp.asarray(np.concatenate([m, m, m], 1), BF16)


def _gla_pair_masks(reverse):
    L = G_CHUNK
    s = np.arange(L)[:, None]
    t = np.arange(L)[None, :]
    masks = [s == t]
    for h in G_LEVELS:
        same = (s // (2 * h)) == (t // (2 * h))
        masks.append(same & ((t // h) % 2 == 1) & ((s // h) % 2 == 0))
    m = np.stack(masks).astype(np.float32)
    if reverse:
        m = m[:, ::-1, ::-1]
    return jnp.asarray(np.tile(m, (1, 1, G_HEADS)), F32)


def _gla_fast_matrices(reverse):
    L = G_CHUNK
    h = L // 2
    tt = np.arange(L)[:, None]
    u = np.arange(L)[None, :]
    mid = np.where(tt >= h, ((u >= h) & (u <= tt)).astype(np.float32), -((u > tt) & (u < h)).astype(np.float32))
    m = np.stack([(u <= tt).astype(np.float32), (u > tt).astype(np.float32), mid])
    if reverse:
        m = m[:, ::-1, ::-1]
    m = m.reshape(3 * L, L)
    return jnp.asarray(np.concatenate([m, m, m], 1), BF16)


def _gla_causal_mask(reverse):
    s = np.arange(G_CHUNK)[:, None]
    t = np.arange(G_CHUNK)[None, :]
    m = (s >= t) if reverse else (s <= t)
    return jnp.asarray(np.tile(m.astype(np.float32), (1, G_HEADS)), F32)


def _gla_head_mask():
    r = np.arange(G_HEADS * G_CHUNK)[:, None] // G_CHUNK
    c = np.arange(G_QK)[None, :] // G_DK
    return jnp.asarray((r == c).astype(np.float32), BF16)


GLA_FAST_LIMIT = 40.0


def _gla_kernel(emit, nch, pipelined, qf_ref, kf_ref, vf_ref, lrf_ref, qb_ref, kb_ref, vb_ref, lrb_ref,
                lrnf_ref, lrnb_ref, lvf_ref, lvb_ref, fastf_ref, fastb_ref, mkf_ref, mkb_ref, cmf_ref, cmb_ref,
                hm_ref, gw_ref, gb_ref, s0_ref, *rest):
    if emit:
        of_ref, ob_ref, sout_ref, s_s, a3_s, lo_s = rest
    else:
        sout_ref, s_s, a3_s, lo_s = rest
        of_ref = ob_ref = None
    j = pl.program_id(1)
    L = G_CHUNK
    slot = j % 2

    def stage_decays(lr_refs, dst):
        lo = None
        for d, lr_ref in enumerate(lr_refs):
            lr = lr_ref[0][:, N_GATES + d * G_RANK:N_GATES + (d + 1) * G_RANK].astype(BF16)
            loga = _log_sigmoid(_dot(lr, gw_ref[d]) + gb_ref[d]) * (LOG2_E / G_TAU)
            pieces = _split3(loga)
            a3_s[dst, d] = jnp.concatenate(
                [jnp.concatenate([p[c * L:(c + 1) * L] for p in pieces], 0) for c in range(nch)], 1)
            m = jnp.min(loga)
            lo = m if lo is None else jnp.minimum(lo, m)
        lo_s[dst] = lo * L

    @pl.when(j == 0)
    def _():
        s_s[...] = s0_ref[0]
        stage_decays((lrf_ref, lrb_ref), 0)

    hm = hm_ref[...]
    stack = lambda a: jnp.concatenate([a] * G_HEADS, 0)

    def run(fast):
        for d in range(2):
            q_ref, k_ref, v_ref, mk_ref, cm_ref, o_ref = (
                (qf_ref, kf_ref, vf_ref, mkf_ref, cmf_ref, of_ref) if d == 0 else
                (qb_ref, kb_ref, vb_ref, mkb_ref, cmb_ref, ob_ref))
            sel_ref = ((fastf_ref, fastb_ref) if fast else (lvf_ref, lvb_ref))[d]
            g = _dot(sel_ref[...], a3_s[slot, d])
            expo = jnp.exp2(g[:2 * L] if fast else g)
            for c in (range(nch) if d == 0 else reversed(range(nch))):
                rows = slice(c * L, (c + 1) * L)
                cols = slice(c * G_QK, (c + 1) * G_QK)
                k = k_ref[0, rows, :]
                v_st = jnp.concatenate([v_ref[0, rows, hd * G_DV:(hd + 1) * G_DV] for hd in range(G_HEADS)], 0)
                e = lambda i: expo[i * L:(i + 1) * L, cols].astype(BF16)
                kd_st = stack(k * e(1)) * hm
                e_tot = expo[L - 1:L, cols] if d == 0 else expo[0:1, cols]
                st = s_s[d]
                if emit:
                    qs = stack(q_ref[0, rows, :]) * hm
                    if fast:
                        gm = g[2 * L:3 * L, cols]
                        att_t = _dot_nt(k * jnp.exp2(-gm).astype(BF16), qs * stack(jnp.exp2(gm).astype(BF16)))
                        att_t = jnp.where(cm_ref[...] != 0.0, att_t, 0.0)
                    else:
                        att_t = mk_ref[0] * _dot_nt(k, qs)
                        for lv in range(len(G_LEVELS)):
                            ev = e(2 + lv)
                            att_t = att_t + mk_ref[1 + lv] * _dot_nt(k * ev, qs * stack(ev))
                    inter_t = _dot_nt(st.astype(BF16), qs * stack(e(0)))
                    att_st = stack(att_t.astype(BF16)) * hm
                    res = _dot_tn(v_st, jnp.concatenate([att_st, kd_st], 1))
                    o = jnp.transpose(inter_t + res[:, :G_HEADS * L])
                    for hd in range(G_HEADS):
                        o_ref[0, rows, hd * G_DV:(hd + 1) * G_DV] = o[hd * L:(hd + 1) * L]
                    u_t = res[:, G_HEADS * L:]
                else:
                    u_t = _dot_tn(v_st, kd_st)
                s_s[d] = e_tot * st + u_t
        if pipelined:
            stage_decays((lrnf_ref, lrnb_ref), 1 - slot)

    if emit:
        mild = lo_s[slot] >= -GLA_FAST_LIMIT
        pl.when(mild)(lambda: run(True))
        pl.when(jnp.logical_not(mild))(lambda: run(False))
    else:
        run(True)

    @pl.when(j == pl.num_programs(1) - 1)
    def _():
        sout_ref[0] = s_s[...]


def _gla_scan(gq, gk, gv, small, g_gate_w, g_gate_b, s0, emit, tb):
    b, t, _ = gv.shape
    nb = t // tb
    consts = (_gla_select_matrices(False), _gla_select_matrices(True),
              _gla_fast_matrices(False), _gla_fast_matrices(True),
              _gla_pair_masks(False), _gla_pair_masks(True),
              _gla_causal_mask(False), _gla_causal_mask(True), _gla_head_mask(), g_gate_w, g_gate_b)
    fw = lambda w: pl.BlockSpec((1, tb, w), lambda i, j: (i, j, 0))
    bw = lambda w: pl.BlockSpec((1, tb, w), lambda i, j: (i, nb - 1 - j, 0))
    const = lambda a: pl.BlockSpec(a.shape, lambda i, j: (0,) * a.ndim)
    sst = pl.BlockSpec((1, 2, G_DV, G_QK), lambda i, j: (i, 0, 0, 0))
    fw_next = pl.BlockSpec((1, tb, SMALL_COLS), lambda i, j: (i, jnp.minimum(j + 1, nb - 1), 0))
    bw_next = pl.BlockSpec((1, tb, SMALL_COLS), lambda i, j: (i, jnp.maximum(nb - 2 - j, 0), 0))
    in_specs = [fw(G_QK), fw(G_QK), fw(G_WIDTH), fw(SMALL_COLS),
                bw(G_QK), bw(G_QK), bw(G_WIDTH), bw(SMALL_COLS), fw_next, bw_next]
    in_specs += [const(a) for a in consts] + [sst]
    out_specs = [sst]
    out_shape = [jax.ShapeDtypeStruct(s0.shape, F32)]
    if emit:
        out_specs = [fw(G_WIDTH), bw(G_WIDTH)] + out_specs
        out_shape = [jax.ShapeDtypeStruct((b, t, G_WIDTH), F32)] * 2 + out_shape
    nch = tb // G_CHUNK
    return pl.pallas_call(
        functools.partial(_gla_kernel, emit, nch, nb > 1),
        grid=(b, nb),
        in_specs=in_specs,
        out_specs=out_specs,
        out_shape=out_shape,
        scratch_shapes=[pltpu.VMEM((2, G_DV, G_QK), F32),
                        pltpu.VMEM((2, 2, 3 * G_CHUNK, nch * G_QK), BF16),
                        pltpu.SMEM((2,), F32)],
        compiler_params=_params("parallel", "arbitrary"),
        name="gla_scan" if emit else "gla_ctx_state",
    )(gq, gk, gv, small, gq, gk, gv, small, small, small, *consts, s0)


FFN_CHUNK = 256


def _head_norm(h, nheads, width):
    parts = []
    for hd in range(nheads):
        blk = h[:, hd * width:(hd + 1) * width]
        parts.append(blk * lax.rsqrt(jnp.mean(blk * blk, -1, keepdims=True) + EPS))
    return jnp.concatenate(parts, 1)


def _out_kernel(d_ff, x_ref, hf_ref, hb_ref, of_ref, ob_ref, mo_ref, gr_ref, mod_ref,
                mng_ref, gng_ref, gffn_ref, gfin_ref, wout_ref, wgu_ref, wdown_ref, o_ref):
    x = x_ref[0]
    mod = mod_ref[0]
    hm_t = hf_ref[0] + hb_ref[0]
    parts = []
    for hd in range(M_HEADS):
        blk = hm_t[hd * M_DV:(hd + 1) * M_DV]
        parts.append(blk * lax.rsqrt(jnp.mean(blk * blk, 0, keepdims=True) + EPS))
    hm_t = jnp.concatenate(parts, 0) * mng_ref[...] * _sigmoid(mo_ref[0].astype(F32))
    gr = gr_ref[0].astype(F32)
    hg = _head_norm(of_ref[0] + ob_ref[0], G_HEADS, G_DV) * gng_ref[...] * (gr * _sigmoid(gr))
    mixed = _dot_tn(hm_t.astype(BF16), wout_ref[:M_WIDTH, :]) + _dot(hg.astype(BF16), wout_ref[M_WIDTH:, :])
    x1 = x + mod[0:1] * mixed
    h2 = x1 * lax.rsqrt(jnp.mean(x1 * x1, -1, keepdims=True) + EPS) * gffn_ref[...]
    h2 = (h2 * (1.0 + mod[2:3]) + mod[1:2]).astype(BF16)
    acc = jnp.zeros_like(x1)
    for c0 in range(0, d_ff, FFN_CHUNK):
        a = _dot(h2, wgu_ref[:, c0:c0 + FFN_CHUNK])
        g = _dot(h2, wgu_ref[:, d_ff + c0:d_ff + c0 + FFN_CHUNK])
        act = (a * _sigmoid(a) * g).astype(BF16)
        acc = acc + _dot(act, wdown_ref[c0:c0 + FFN_CHUNK, :])
    x2 = x1 + mod[3:4] * acc
    o_ref[0] = x2 * lax.rsqrt(jnp.mean(x2 * x2, -1, keepdims=True) + EPS) * gfin_ref[...]


def _out_block(x, hf, hb, of, ob, mo, gr, mod4, mng, gng, gffn, gfin, wout, wgu, wdown, tm):
    b, t, d = x.shape
    d_ff = wdown.shape[0]
    tok = lambda w: pl.BlockSpec((1, tm, w), lambda i, j: (i, j, 0))
    feat = lambda w: pl.BlockSpec((1, w, tm), lambda i, j: (i, 0, j))
    const = lambda a: pl.BlockSpec(a.shape, lambda i, j: (0,) * a.ndim, pipeline_mode=pl.Buffered(1))
    return pl.pallas_call(
        functools.partial(_out_kernel, d_ff),
        grid=(b, t // tm),
        in_specs=[tok(d), feat(M_WIDTH), feat(M_WIDTH), tok(G_WIDTH), tok(G_WIDTH), feat(M_WIDTH), tok(G_WIDTH),
                  pl.BlockSpec((1, 4, d), lambda i, j: (i, 0, 0)),
                  const(mng), const(gng), const(gffn), const(gfin), const(wout), const(wgu), const(wdown)],
        out_specs=tok(d),
        out_shape=jax.ShapeDtypeStruct((b, t, d), F32),
        compiler_params=_params("parallel", "arbitrary"),
        name="out_ffn",
    )(x, hf, hb, of, ob, mo, gr, mod4, mng, gng, gffn, gfin, wout, wgu, wdown)


def _largest_tile(t, cap):
    tm = min(t, cap)
    while t % tm:
        tm //= 2
    return tm


def kernel(x, c, ctx, c_ctx, w_ada, b_ada, g_mix, w_in, conv_w, m_gate_b, m_norm_g, g_gate_w, g_gate_b,
           g_norm_g, w_out, g_ffn, w_gu, w_down, g_final):
    b, t, d = x.shape
    tc = ctx.shape[1]
    assert w_ada.shape[0] == 1, "single layer block"
    assert t % (GRID_W * 2) == 0 and tc % M_CHUNK == 0

    pad = (-(b + 1)) % 8
    c_all = jnp.concatenate([c, c_ctx[None, :], jnp.zeros((pad, d), F32)], 0)
    mod = _modulation(c_all, w_ada[0], b_ada[0])
    mod_x = mod[:b].reshape(b, 6, d)
    mod_c = jnp.broadcast_to(mod[b].reshape(1, 6, d), (b, 6, d))

    w = w_in[0]
    sizes = (2 * M_QK, M_WIDTH, M_WIDTH, N_GATES, G_QK, G_QK, G_WIDTH, G_WIDTH, 2 * G_RANK)
    offs = np.concatenate([[0], np.cumsum(sizes)])
    seg = lambda i: w[:, offs[i]:offs[i + 1]]
    wtok = jnp.concatenate([seg(0), seg(4), seg(5), seg(6), seg(7)], 1).astype(BF16)
    wfeat = jnp.transpose(jnp.concatenate([seg(1), seg(2)], 1)).astype(BF16)
    n_small = N_GATES + 2 * G_RANK
    wsmall = jnp.concatenate([seg(3), seg(8), jnp.zeros((d, SMALL_COLS - n_small), F32)], 1).astype(BF16)
    wgt = jnp.transpose(seg(3)).astype(BF16)
    bsm = jnp.concatenate([m_gate_b[0], jnp.zeros((SMALL_COLS - N_GATES,), F32)]).reshape(1, SMALL_COLS)
    bgt = m_gate_b[0].reshape(N_GATES, 1)
    g_mix2 = g_mix[0].reshape(1, d)
    conv_w9 = conv_w[0].reshape(9, 2 * M_QK)
    kscale = jnp.concatenate([jnp.ones((M_QK,), F32), jnp.full((M_QK,), M_DK ** -0.5, F32)]).reshape(1, 2 * M_QK)
    ggw = g_gate_w[0].astype(BF16)
    ggb = g_gate_b[0].reshape(2, 1, G_QK)

    feat_consts = (g_mix2, wtok, wfeat, wsmall, wgt, bsm, bgt, conv_w9, kscale)

    def features(inp, m6, cols, tm):
        mqk, gq, gk, gv, gr, mvt, mot, small, gt = _features(inp, m6[:, 0:1], m6[:, 1:2], feat_consts, cols, tm)
        return dict(mqk=mqk, gq=gq, gk=gk, gv=gv, gr=gr, mvt=mvt, mot=mot, small=small, gt=gt)

    fc = features(ctx, mod_c, tc, tc)
    c0 = jnp.zeros((b, 2 * M_HEADS, M_AUG, M_DK), F32)
    m0 = jnp.full((b, 2 * M_HEADS, LANE), NEG, F32)
    c_ctx_state, m_ctx_state = _mlstm_scan(fc["mqk"], fc["mvt"], fc["small"], fc["gt"], c0, m0, False,
                                           _largest_tile(tc, 256))
    s0 = jnp.zeros((b, 2, G_DV, G_QK), F32)
    (s_ctx_state,) = _gla_scan(fc["gq"], fc["gk"], fc["gv"], fc["small"], ggw, ggb, s0, False,
                               _largest_tile(tc, 256))

    fx = features(x, mod_x, GRID_W, _largest_tile(t, 512))
    hf_t, hb_t, _, _ = _mlstm_scan(fx["mqk"], fx["mvt"], fx["small"], fx["gt"], c_ctx_state, m_ctx_state, True,
                                   _largest_tile(t, 256))
    of, ob, _ = _gla_scan(fx["gq"], fx["gk"], fx["gv"], fx["small"], ggw, ggb, s_ctx_state, True,
                          _largest_tile(t, 256))

    tm = _largest_tile(t, 512)
    mod4 =jnp.stack([mod_x[:, 2], mod_x[:, 3], mod_x[:, 4], mod_x[:, 5]], 1)
    mng = jnp.broadcast_to(m_norm_g[0].reshape(M_WIDTH, 1), (M_WIDTH, tm))
    return _out_block(x, hf_t, hb_t, of, ob, fx["mot"], fx["gr"], mod4,
                      mng, g_norm_g[0].reshape(1, -1),
                      g_ffn[0].reshape(1, d), g_final.reshape(1, d),
                      w_out[0].astype(BF16), w_gu[0].astype(BF16), w_down[0].astype(BF16), tm)
```

```python
import functools

import numpy as np
import jax
import jax.numpy as jnp
from jax import lax
from jax.experimental import pallas as pl
from jax.experimental.pallas import tpu as pltpu

F32 = jnp.float32
BF16 = jnp.bfloat16

GRID_W = 64
M_HEADS = 4
M_DK = 128
M_DV = 128
M_QK = M_HEADS * M_DK
M_WIDTH = M_HEADS * M_DV
M_CHUNK = 128
G_HEADS = 4
G_DK = 64
G_DV = 128
G_QK = G_HEADS * G_DK
G_WIDTH = G_HEADS * G_DV
G_RANK = 16
G_TAU = 16.0
G_CHUNK = 64
EPS = 1e-6
NEG = -1e30
LOG2_E = 1.4426950408889634

LANE = 128
VMEM_LIMIT = 56 * 1024 * 1024


def _params(*sem):
    return pltpu.CompilerParams(dimension_semantics=sem, vmem_limit_bytes=VMEM_LIMIT)


def _sigmoid(x):
    return 1.0 / (1.0 + jnp.exp(-x))


def _log_sigmoid(x):
    return jnp.minimum(x, 0.0) - jnp.log(1.0 + jnp.exp(-jnp.abs(x)))


def _split3(x):
    x1 = x.astype(BF16)
    r1 = x - x1.astype(F32)
    x2 = r1.astype(BF16)
    x3 = (r1 - x2.astype(F32)).astype(BF16)
    return x1, x2, x3


def _dot(a, b):
    return jnp.dot(a, b, preferred_element_type=F32)


def _dot_nt(a, b):
    return lax.dot_general(a, b, (((1,), (1,)), ((), ())), preferred_element_type=F32)


def _dot_tn(a, b):
    return lax.dot_general(a, b, (((0,), (0,)), ((), ())), preferred_element_type=F32)


def _sel_left(m01, x):
    x1, x2, x3 = _split3(x)
    return _dot(m01, x1) + _dot(m01, x2) + _dot(m01, x3)


def _sel_right(x, m01):
    x1, x2, x3 = _split3(x)
    return _dot(x1, m01) + _dot(x2, m01) + _dot(x3, m01)


def _mod_kernel(c_ref, w_ref, b_ref, o_ref):
    cs = c_ref[...]
    a = (cs * _sigmoid(cs)).astype(BF16)
    o_ref[...] = _dot(a, w_ref[...].astype(BF16)) + b_ref[...]


def _modulation(c_all, w_ada, b_ada):
    rows, d = c_all.shape
    n = w_ada.shape[1]
    bn = n // 6
    return pl.pallas_call(
        _mod_kernel,
        grid=(n // bn,),
        in_specs=[pl.BlockSpec((rows, d), lambda j: (0, 0)),
                  pl.BlockSpec((d, bn), lambda j: (0, j)),
                  pl.BlockSpec((1, bn), lambda j: (0, j))],
        out_specs=pl.BlockSpec((rows, bn), lambda j: (0, j)),
        out_shape=jax.ShapeDtypeStruct((rows, n), F32),
        compiler_params=_params("arbitrary"),
        name="adaln_mod",
    )(c_all, w_ada, b_ada.reshape(1, n))


_TOK_SIZES = (("mqk", 2 * M_QK), ("gq", G_QK), ("gk", G_QK), ("gv", G_WIDTH), ("gr", G_WIDTH))
_FEAT_SIZES = (("mv", M_WIDTH), ("mo", M_WIDTH))
N_GATES = 4 * M_HEADS
SMALL_COLS = LANE


CONV_CB = 256


def _conv_silu(ext, w9, cols, nrow):
    tot = (nrow + 2) * cols
    col = lax.broadcasted_iota(jnp.int32, ext.shape, 0) % cols
    left = jnp.where(col == 0, 0.0, pltpu.roll(ext, 1, 0))
    right = jnp.where(col == cols - 1, 0.0, pltpu.roll(ext, tot - 1, 0))
    srcs = (left, ext, right)
    acc = None
    for di in range(3):
        for dj in range(3):
            term = srcs[dj][di * cols:(di + nrow) * cols] * w9[di * 3 + dj:di * 3 + dj + 1]
            acc = term if acc is None else acc + term
    return acc * _sigmoid(acc)


def _feat_kernel(cols, x_ref, xn_ref, sh_ref, sc_ref, g_ref, wtok_ref, wfeat_ref, wsmall_ref, wgt_ref,
                 bsm_ref, bgt_ref, cw_ref, ksc_ref,
                 mqk_ref, gq_ref, gk_ref, gv_ref, gr_ref, mvt_ref, mot_ref, small_ref, gt_ref, prev_s):
    j = pl.program_id(1)
    last = pl.num_programs(1) - 1
    tm = x_ref.shape[1]

    def norm_mod(x):
        h = x * lax.rsqrt(jnp.mean(x * x, -1, keepdims=True) + EPS) * g_ref[...]
        return (h * (1.0 + sc_ref[0]) + sh_ref[0]).astype(BF16)

    hb = norm_mod(x_ref[0])
    hext = jnp.concatenate([hb, norm_mod(xn_ref[0])], 0)

    @pl.when(j == 0)
    def _():
        prev_s[...] = jnp.zeros_like(prev_s)

    outs = dict(gq=gq_ref, gk=gk_ref, gv=gv_ref, gr=gr_ref, mv=mvt_ref, mo=mot_ref)
    tasks = []

    def tok_task(name, off, c0):
        def run():
            z = _dot(hb, wtok_ref[:, off + c0:off + c0 + CONV_CB])
            if name == "gq":
                z = z * (G_DK ** -0.5)
            outs[name][0, :, c0:c0 + CONV_CB] = z.astype(BF16)
        return run

    def feat_task(name, off, c0):
        def run():
            z = _dot_nt(wfeat_ref[off + c0:off + c0 + CONV_CB, :], hb)
            outs[name][0, c0:c0 + CONV_CB, :] = z.astype(BF16)
        return run

    off = 2 * M_QK
    for name, size in _TOK_SIZES[1:]:
        tasks += [tok_task(name, off, c0) for c0 in range(0, size, CONV_CB)]
        off += size
    off = 0
    for name, size in _FEAT_SIZES:
        tasks += [feat_task(name, off, c0) for c0 in range(0, size, CONV_CB)]
        off += size

    nblk = 2 * M_QK // CONV_CB
    for i in range(nblk):
        cs = slice(i * CONV_CB, (i + 1) * CONV_CB)
        z = _dot(hext, wtok_ref[:, cs])
        for task in tasks[i * len(tasks) // nblk:(i + 1) * len(tasks) // nblk]:
            task()
        below = jnp.where(j < last, z[tm:], 0.0)
        ext = jnp.concatenate([prev_s[:, cs], z[:tm], below], 0)
        prev_s[:, cs] = z[tm - cols:tm]
        mqk_ref[0, :, cs] = (_conv_silu(ext, cw_ref[:, cs], cols, tm // cols) * ksc_ref[:, cs]).astype(BF16)
    small_ref[0] = _dot(hb, wsmall_ref[...]) + bsm_ref[...]
    gt_ref[0] = _dot_nt(wgt_ref[...], hb) + bgt_ref[...]


def _features(x, shift, scale, consts, cols, tm):
    b, t, d = x.shape
    nt = t // tm
    per = tm // cols
    tok = lambda w: pl.BlockSpec((1, tm, w), lambda i, j: (i, j, 0))
    feat = lambda w: pl.BlockSpec((1, w, tm), lambda i, j: (i, 0, j))
    below = pl.BlockSpec((1, cols, d), lambda i, j: (i, jnp.minimum((j + 1) * per, t // cols - 1), 0))
    const = lambda a: pl.BlockSpec(a.shape, lambda i, j: (0,) * a.ndim)
    vec = pl.BlockSpec((1, 1, d), lambda i, j: (i, 0, 0))
    out_shape = [jax.ShapeDtypeStruct((b, t, s), BF16) for _, s in _TOK_SIZES]
    out_specs = [tok(s) for _, s in _TOK_SIZES]
    out_shape += [jax.ShapeDtypeStruct((b, s, t), BF16) for _, s in _FEAT_SIZES]
    out_specs += [feat(s) for _, s in _FEAT_SIZES]
    out_shape += [jax.ShapeDtypeStruct((b, t, SMALL_COLS), F32), jax.ShapeDtypeStruct((b, N_GATES, t), F32)]
    out_specs += [tok(SMALL_COLS), feat(N_GATES)]
    return pl.pallas_call(
        functools.partial(_feat_kernel, cols),
        grid=(b, nt),
        in_specs=[tok(d), below, vec, vec] + [const(a) for a in consts],
        out_specs=out_specs,
        out_shape=out_shape,
        scratch_shapes=[pltpu.VMEM((cols, 2 * M_QK), F32)],
        compiler_params=_params("parallel", "arbitrary"),
        name="features",
    )(x, x, shift, scale, *consts)


M_AUG = M_DV + 16


def _mlstm_cumsum_matrices(reverse):
    L = M_CHUNK
    low = np.tril(np.ones((L, L), np.float32))
    tri = low.T if reverse else low
    col = np.concatenate([tri, tri, tri], 1)
    row = np.tile(np.concatenate([tri.T, np.ones((L, L), np.float32)], 1), (3, 1))
    return jnp.asarray(col, BF16), jnp.asarray(row, BF16)


def _mlstm_kernel(emit, nch, qf_ref, kf_ref, vf_ref, gcf_ref, grf_ref, qb_ref, kb_ref, vb_ref, gcb_ref, grb_ref,
                  colf_ref, rowf_ref, colb_ref, rowb_ref, c0_ref, m0_ref, *rest):
    if emit:
        hf_ref, hb_ref, cout_ref, mout_ref, c_s, m_s = rest
    else:
        cout_ref, mout_ref, c_s, m_s = rest
        hf_ref = hb_ref = None
    j = pl.program_id(1)
    L = M_CHUNK

    @pl.when(j == 0)
    def _():
        c_s[...] = c0_ref[0]
        m_s[...] = m0_ref[0]

    sid = lax.broadcasted_iota(jnp.int32, (L, L), 0)
    tid = lax.broadcasted_iota(jnp.int32, (L, L), 1)
    ones_rows = jnp.ones((M_AUG - M_DV, L), BF16)

    refs = ((qf_ref, kf_ref, vf_ref, gcf_ref, grf_ref, colf_ref, rowf_ref, hf_ref),
            (qb_ref, kb_ref, vb_ref, gcb_ref, grb_ref, colb_ref, rowb_ref, hb_ref))
    masks = (sid <= tid, sid >= tid)

    for i in range(nch):
        chains = []
        for d in range(2):
            q_ref, k_ref, v_ref, gc_ref, gr_ref, col_ref, row_ref, h_ref = refs[d]
            c = i if d == 0 else nch - 1 - i
            tok = slice(c * L, (c + 1) * L)
            gcol = gc_ref[0, tok, :][:, :N_GATES]
            grow = gr_ref[0, :, tok]
            cum_col = _dot(col_ref[...], jnp.concatenate(_split3(_log_sigmoid(gcol) * LOG2_E), 0))
            cum_ext = _dot(jnp.concatenate(_split3(_log_sigmoid(grow) * LOG2_E), 1), row_ref[...])
            gcol = gcol * LOG2_E
            grow = grow * LOG2_E
            for hd in range(M_HEADS):
                ci = 2 * M_HEADS * d + hd
                cf = ci + M_HEADS
                sl = slice(hd * M_DK, (hd + 1) * M_DK)
                chains.append(dict(
                    d=d, r=d * M_HEADS + hd, sl=sl, tok=tok,
                    q=q_ref[0, tok, sl], k=k_ref[0, tok, sl],
                    v_aug=jnp.concatenate([v_ref[0, sl, tok], ones_rows], 0),
                    br=cum_ext[cf:cf + 1, :L],
                    btot=cum_ext[cf:cf + 1, L:],
                    ir=grow[ci:ci + 1, :],
                    cc=gcol[:, ci:ci + 1] - cum_col[:, cf:cf + 1]))

        for ch in chains:
            r = ch["r"]
            m = m_s[r:r + 1, :]
            ct = c_s[r]
            ch["m"] = m
            if emit:
                ch["big"] = _dot_nt(jnp.concatenate([ch["k"], ct.astype(BF16)], 0), ch["q"])
            src = ch["btot"] + ch["ir"] - ch["br"]
            m_new = jnp.maximum(m + ch["btot"], jnp.max(src, -1, keepdims=True))
            ws = jnp.exp2(src - m_new).astype(BF16)
            wc = jnp.exp2(m + ch["btot"] - m_new)
            c_s[r] = wc * ct + _dot(ch["v_aug"] * ws, ch["k"])
            m_s[r:r + 1, :] = m_new

        if emit:
            for ch in chains:
                m, br, big = ch["m"], ch["br"], ch["big"]
                dmt = jnp.where(masks[ch["d"]], ch["cc"] + br, NEG)
                mt = jnp.maximum(m + br, jnp.max(dmt, 0, keepdims=True))
                pt = big[:L] * jnp.exp2(dmt - mt)
                num = jnp.exp2(m + br - mt) * big[L:] + _dot(ch["v_aug"], pt.astype(BF16))
                inv = 1.0 / jnp.maximum(jnp.abs(num[M_DV:M_DV + 1]), jnp.exp2(-mt))
                refs[ch["d"]][7][0, ch["sl"], ch["tok"]] = num[:M_DV] * inv

    @pl.when(j == pl.num_programs(1) - 1)
    def _():
        cout_ref[0] = c_s[...]
        mout_ref[0] = m_s[...]


def _mlstm_scan(mqk, mvt, small, gt, c0, m0, emit, tb):
    b, _, t = mvt.shape
    nb = t // tb
    consts = _mlstm_cumsum_matrices(False) + _mlstm_cumsum_matrices(True)
    fw = lambda w, cblk: pl.BlockSpec((1, tb, w), lambda i, j: (i, j, cblk))
    bw = lambda w, cblk: pl.BlockSpec((1, tb, w), lambda i, j: (i, nb - 1 - j, cblk))
    ftf = lambda w: pl.BlockSpec((1, w, tb), lambda i, j: (i, 0, j))
    ftb = lambda w: pl.BlockSpec((1, w, tb), lambda i, j: (i, 0, nb - 1 - j))
    const = lambda a: pl.BlockSpec(a.shape, lambda i, j: (0,) * a.ndim)
    cst = pl.BlockSpec((1, 2 * M_HEADS, M_AUG, M_DK), lambda i, j: (i, 0, 0, 0))
    mst = pl.BlockSpec((1, 2 * M_HEADS, LANE), lambda i, j: (i, 0, 0))
    in_specs = [fw(M_QK, 0), fw(M_QK, 1), ftf(M_WIDTH), fw(SMALL_COLS, 0), ftf(N_GATES),
                bw(M_QK, 0), bw(M_QK, 1), ftb(M_WIDTH), bw(SMALL_COLS, 0), ftb(N_GATES)]
    in_specs += [const(a) for a in consts] + [cst, mst]
    out_specs = [cst, mst]
    out_shape = [jax.ShapeDtypeStruct(c0.shape, F32), jax.ShapeDtypeStruct(m0.shape, F32)]
    if emit:
        out_specs = [ftf(M_WIDTH), ftb(M_WIDTH)] + out_specs
        out_shape = [jax.ShapeDtypeStruct((b, M_WIDTH, t), F32)] * 2 + out_shape
    return pl.pallas_call(
        functools.partial(_mlstm_kernel, emit, tb // M_CHUNK),
        grid=(b, nb),
        in_specs=in_specs,
        out_specs=out_specs,
        out_shape=out_shape,
        scratch_shapes=[pltpu.VMEM((2 * M_HEADS, M_AUG, M_DK), F32), pltpu.VMEM((2 * M_HEADS, LANE), F32)],
        compiler_params=_params("parallel", "arbitrary"),
        name="mlstm_scan" if emit else "mlstm_ctx_state",
    )(mqk, mqk, mvt, small, gt, mqk, mqk, mvt, small, gt, *consts, c0, m0)


G_LEVELS = (32, 16, 8, 4, 2, 1)


def _gla_select_matrices(reverse):
    L = G_CHUNK
    t = np.arange(L)
    u = np.arange(L)[None, :]
    tt = t[:, None]
    mats = [(u <= tt), (u > tt)]
    for h in G_LEVELS:
        start = (tt // (2 * h)) * (2 * h)
        second = tt >= start + h
        m = np.where(second, (u >= start + h) & (u <= tt), (u > tt) & (u < start + h))
        mats.append(m)
    m = np.concatenate([x.astype(np.float32) for x in mats], 0)
    if reverse:
        m = m.reshape(len(mats), L, L)[:, ::-1, ::-1].reshape(len(mats) * L, L)
    return jnp.asarray(np.concatenate([m, m, m], 1), BF16)


def _gla_pair_masks(reverse):
    L = G_CHUNK
    s = np.arange(L)[:, None]
    t = np.arange(L)[None, :]
    masks = [s == t]
    for h in G_LEVELS:
        same = (s // (2 * h)) == (t // (2 * h))
        masks.append(same & ((t // h) % 2 == 1) & ((s // h) % 2 == 0))
    m = np.stack(masks).astype(np.float32)
    if reverse:
        m = m[:, ::-1, ::-1]
    return jnp.asarray(np.tile(m, (1, 1, G_HEADS)), F32)


def _gla_fast_matrices(reverse):
    L = G_CHUNK
    h = L // 2
    tt = np.arange(L)[:, None]
    u = np.arange(L)[None, :]
    mid = np.where(tt >= h, ((u >= h) & (u <= tt)).astype(np.float32), -((u > tt) & (u < h)).astype(np.float32))
    m = np.stack([(u <= tt).astype(np.float32), (u > tt).astype(np.float32), mid])
    if reverse:
        m = m[:, ::-1, ::-1]
    m = m.reshape(3 * L, L)
    return jnp.asarray(np.concatenate([m, m, m], 1), BF16)


def _gla_causal_mask(reverse):
    s = np.arange(G_CHUNK)[:, None]
    t = np.arange(G_CHUNK)[None, :]
    m = (s >= t) if reverse else (s <= t)
    return jnp.asarray(np.tile(m.astype(np.float32), (1, G_HEADS)), F32)


def _gla_head_mask():
    r = np.arange(G_HEADS * G_CHUNK)[:, None] // G_CHUNK
    c = np.arange(G_QK)[None, :] // G_DK
    return jnp.asarray((r == c).astype(np.float32), BF16)


GLA_FAST_LIMIT = 40.0


def _gla_kernel(emit, nch, pipelined, qf_ref, kf_ref, vf_ref, lrf_ref, qb_ref, kb_ref, vb_ref, lrb_ref,
                lrnf_ref, lrnb_ref, lvf_ref, lvb_ref, fastf_ref, fastb_ref, mkf_ref, mkb_ref, cmf_ref, cmb_ref,
                hm_ref, gw_ref, gb_ref, s0_ref, *rest):
    if emit:
        of_ref, ob_ref, sout_ref, s_s, a3_s, lo_s = rest
    else:
        sout_ref, s_s, a3_s, lo_s = rest
        of_ref = ob_ref = None
    j = pl.program_id(1)
    L = G_CHUNK
    slot = j % 2

    def stage_decays(lr_refs, dst):
        lo = None
        for d, lr_ref in enumerate(lr_refs):
            lr = lr_ref[0][:, N_GATES + d * G_RANK:N_GATES + (d + 1) * G_RANK].astype(BF16)
            loga = _log_sigmoid(_dot(lr, gw_ref[d]) + gb_ref[d]) * (LOG2_E / G_TAU)
            pieces = _split3(loga)
            a3_s[dst, d] = jnp.concatenate(
                [jnp.concatenate([p[c * L:(c + 1) * L] for p in pieces], 0) for c in range(nch)], 1)
            m = jnp.min(loga)
            lo = m if lo is None else jnp.minimum(lo, m)
        lo_s[dst] = lo * L

    @pl.when(j == 0)
    def _():
        s_s[...] = s0_ref[0]
        stage_decays((lrf_ref, lrb_ref), 0)

    hm = hm_ref[...]
    stack = lambda a: jnp.concatenate([a] * G_HEADS, 0)

    refs = ((qf_ref, kf_ref, vf_ref, mkf_ref, cmf_ref, of_ref), (qb_ref, kb_ref, vb_ref, mkb_ref, cmb_ref, ob_ref))

    def run(fast):
        chunks = []
        for d in range(2):
            q_ref, k_ref, v_ref, mk_ref, cm_ref, o_ref = refs[d]
            sel_ref = ((fastf_ref, fastb_ref) if fast else (lvf_ref, lvb_ref))[d]
            g = _dot(sel_ref[...], a3_s[slot, d])
            expo = jnp.exp2(g[:2 * L] if fast else g)
            for c in (range(nch) if d == 0 else reversed(range(nch))):
                rows = slice(c * L, (c + 1) * L)
                cols = slice(c * G_QK, (c + 1) * G_QK)
                chunks.append(dict(
                    d=d, rows=rows, cols=cols, g=g, expo=expo, k=k_ref[0, rows, :],
                    v_st=jnp.concatenate([v_ref[0, rows, hd * G_DV:(hd + 1) * G_DV] for hd in range(G_HEADS)], 0),
                    e_tot=expo[L - 1:L, cols] if d == 0 else expo[0:1, cols]))

        def e(ch, i):
            return ch["expo"][i * L:(i + 1) * L, ch["cols"]].astype(BF16)

        for ch in chunks:
            k = ch["k"]
            ch["kd_st"] = stack(k * e(ch, 1)) * hm
            if emit:
                q_ref, mk_ref, cm_ref = refs[ch["d"]][0], refs[ch["d"]][3], refs[ch["d"]][4]
                qs = stack(q_ref[0, ch["rows"], :]) * hm
                ch["qs"] = qs
                if fast:
                    gm = ch["g"][2 * L:3 * L, ch["cols"]]
                    att_t = _dot_nt(k * jnp.exp2(-gm).astype(BF16), qs * stack(jnp.exp2(gm).astype(BF16)))
                    att_t = jnp.where(cm_ref[...] != 0.0, att_t, 0.0)
                else:
                    att_t = mk_ref[0] * _dot_nt(k, qs)
                    for lv in range(len(G_LEVELS)):
                        ev = e(ch, 2 + lv)
                        att_t = att_t + mk_ref[1 + lv] * _dot_nt(k * ev, qs * stack(ev))
                ch["att_t"] = att_t

        for ch in chunks:
            if emit:
                att_st = stack(ch["att_t"].astype(BF16)) * hm
                ch["res"] = _dot_tn(ch["v_st"], jnp.concatenate([att_st, ch["kd_st"]], 1))
            else:
                ch["res"] = _dot_tn(ch["v_st"], ch["kd_st"])

        order = [chunks[dd * nch + i] for i in range(nch) for dd in range(2)]
        for ch in order:
            d = ch["d"]
            st = s_s[d]
            res = ch["res"]
            if emit:
                inter_t = _dot_nt(st.astype(BF16), ch["qs"] * stack(e(ch, 0)))
                o = jnp.transpose(inter_t + res[:, :G_HEADS * L])
                o_ref = refs[d][5]
                for hd in range(G_HEADS):
                    o_ref[0, ch["rows"], hd * G_DV:(hd + 1) * G_DV] = o[hd * L:(hd + 1) * L]
                res = res[:, G_HEADS * L:]
            s_s[d] = ch["e_tot"] * st + res
        if pipelined:
            stage_decays((lrnf_ref, lrnb_ref), 1 - slot)

    if emit:
        mild = lo_s[slot] >= -GLA_FAST_LIMIT
        pl.when(mild)(lambda: run(True))
        pl.when(jnp.logical_not(mild))(lambda: run(False))
    else:
        run(True)

    @pl.when(j == pl.num_programs(1) - 1)
    def _():
        sout_ref[0] = s_s[...]


def _gla_scan(gq, gk, gv, small, g_gate_w, g_gate_b, s0, emit, tb):
    b, t, _ = gv.shape
    nb = t // tb
    consts = (_gla_select_matrices(False), _gla_select_matrices(True),
              _gla_fast_matrices(False), _gla_fast_matrices(True),
              _gla_pair_masks(False), _gla_pair_masks(True),
              _gla_causal_mask(False), _gla_causal_mask(True), _gla_head_mask(), g_gate_w, g_gate_b)
    fw = lambda w: pl.BlockSpec((1, tb, w), lambda i, j: (i, j, 0))
    bw = lambda w: pl.BlockSpec((1, tb, w), lambda i, j: (i, nb - 1 - j, 0))
    const = lambda a: pl.BlockSpec(a.shape, lambda i, j: (0,) * a.ndim)
    sst = pl.BlockSpec((1, 2, G_DV, G_QK), lambda i, j: (i, 0, 0, 0))
    fw_next = pl.BlockSpec((1, tb, SMALL_COLS), lambda i, j: (i, jnp.minimum(j + 1, nb - 1), 0))
    bw_next = pl.BlockSpec((1, tb, SMALL_COLS), lambda i, j: (i, jnp.maximum(nb - 2 - j, 0), 0))
    in_specs = [fw(G_QK), fw(G_QK), fw(G_WIDTH), fw(SMALL_COLS),
                bw(G_QK), bw(G_QK), bw(G_WIDTH), bw(SMALL_COLS), fw_next, bw_next]
    in_specs += [const(a) for a in consts] + [sst]
    out_specs = [sst]
    out_shape = [jax.ShapeDtypeStruct(s0.shape, F32)]
    if emit:
        out_specs = [fw(G_WIDTH), bw(G_WIDTH)] + out_specs
        out_shape = [jax.ShapeDtypeStruct((b, t, G_WIDTH), F32)] * 2 + out_shape
    nch = tb // G_CHUNK
    return pl.pallas_call(
        functools.partial(_gla_kernel, emit, nch, nb > 1),
        grid=(b, nb),
        in_specs=in_specs,
        out_specs=out_specs,
        out_shape=out_shape,
        scratch_shapes=[pltpu.VMEM((2, G_DV, G_QK), F32),
                        pltpu.VMEM((2, 2, 3 * G_CHUNK, nch * G_QK), BF16),
                        pltpu.SMEM((2,), F32)],
        compiler_params=_params("parallel", "arbitrary"),
        name="gla_scan" if emit else "gla_ctx_state",
    )(gq, gk, gv, small, gq, gk, gv, small, small, small, *consts, s0)


FFN_CHUNK = 256


def _head_norm(h, nheads, width):
    parts = []
    for hd in range(nheads):
        blk = h[:, hd * width:(hd + 1) * width]
        parts.append(blk * lax.rsqrt(jnp.mean(blk * blk, -1, keepdims=True) + EPS))
    return jnp.concatenate(parts, 1)


def _out_kernel(d_ff, x_ref, hf_ref, hb_ref, of_ref, ob_ref, mo_ref, gr_ref, mod_ref,
                mng_ref, gng_ref, gffn_ref, gfin_ref, wout_ref, wgu_ref, wdown_ref, o_ref):
    x = x_ref[0]
    mod = mod_ref[0]
    hm_t = hf_ref[0] + hb_ref[0]
    parts = []
    for hd in range(M_HEADS):
        blk = hm_t[hd * M_DV:(hd + 1) * M_DV]
        parts.append(blk * lax.rsqrt(jnp.mean(blk * blk, 0, keepdims=True) + EPS))
    hm_t = jnp.concatenate(parts, 0) * mng_ref[...] * _sigmoid(mo_ref[0].astype(F32))
    gr = gr_ref[0].astype(F32)
    hg = _head_norm(of_ref[0] + ob_ref[0], G_HEADS, G_DV) * gng_ref[...] * (gr * _sigmoid(gr))
    mixed = _dot_tn(hm_t.astype(BF16), wout_ref[:M_WIDTH, :]) + _dot(hg.astype(BF16), wout_ref[M_WIDTH:, :])
    x1 = x + mod[0:1] * mixed
    h2 = x1 * lax.rsqrt(jnp.mean(x1 * x1, -1, keepdims=True) + EPS) * gffn_ref[...]
    h2 = (h2 * (1.0 + mod[2:3]) + mod[1:2]).astype(BF16)
    def gated(c0):
        a = _dot(h2, wgu_ref[:, c0:c0 + FFN_CHUNK])
        g = _dot(h2, wgu_ref[:, d_ff + c0:d_ff + c0 + FFN_CHUNK])
        return (a * _sigmoid(a) * g).astype(BF16)

    starts = list(range(0, d_ff, FFN_CHUNK))
    act = gated(starts[0])
    acc = None
    for i, c0 in enumerate(starts):
        nxt = gated(starts[i + 1]) if i + 1 < len(starts) else None
        part = _dot(act, wdown_ref[c0:c0 + FFN_CHUNK, :])
        acc = part if acc is None else acc + part
        act = nxt
    x2 = x1 + mod[3:4] * acc
    o_ref[0] = x2 * lax.rsqrt(jnp.mean(x2 * x2, -1, keepdims=True) + EPS) * gfin_ref[...]


def _out_block(x, hf, hb, of, ob, mo, gr, mod4, mng, gng, gffn, gfin, wout, wgu, wdown, tm):
    b, t, d = x.shape
    d_ff = wdown.shape[0]
    tok = lambda w: pl.BlockSpec((1, tm, w), lambda i, j: (i, j, 0))
    feat = lambda w: pl.BlockSpec((1, w, tm), lambda i, j: (i, 0, j))
    const = lambda a: pl.BlockSpec(a.shape, lambda i, j: (0,) * a.ndim, pipeline_mode=pl.Buffered(1))
    return pl.pallas_call(
        functools.partial(_out_kernel, d_ff),
        grid=(b, t // tm),
        in_specs=[tok(d), feat(M_WIDTH), feat(M_WIDTH), tok(G_WIDTH), tok(G_WIDTH), feat(M_WIDTH), tok(G_WIDTH),
                  pl.BlockSpec((1, 4, d), lambda i, j: (i, 0, 0)),
                  const(mng), const(gng), const(gffn), const(gfin), const(wout), const(wgu), const(wdown)],
        out_specs=tok(d),
        out_shape=jax.ShapeDtypeStruct((b, t, d), F32),
        compiler_params=_params("parallel", "arbitrary"),
        name="out_ffn",
    )(x, hf, hb, of, ob, mo, gr, mod4, mng, gng, gffn, gfin, wout, wgu, wdown)


def _largest_tile(t, cap):
    tm = min(t, cap)
    while t % tm:
        tm //= 2
    return tm


def kernel(x, c, ctx, c_ctx, w_ada, b_ada, g_mix, w_in, conv_w, m_gate_b, m_norm_g, g_gate_w, g_gate_b,
           g_norm_g, w_out, g_ffn, w_gu, w_down, g_final):
    b, t, d = x.shape
    tc = ctx.shape[1]
    assert w_ada.shape[0] == 1, "single layer block"
    assert t % (GRID_W * 2) == 0 and tc % M_CHUNK == 0

    pad = (-(b + 1)) % 8
    c_all = jnp.concatenate([c, c_ctx[None, :], jnp.zeros((pad, d), F32)], 0)
    mod = _modulation(c_all, w_ada[0], b_ada[0])
    mod_x = mod[:b].reshape(b, 6, d)
    mod_c = jnp.broadcast_to(mod[b].reshape(1, 6, d), (b, 6, d))

    w = w_in[0]
    sizes = (2 * M_QK, M_WIDTH, M_WIDTH, N_GATES, G_QK, G_QK, G_WIDTH, G_WIDTH, 2 * G_RANK)
    offs = np.concatenate([[0], np.cumsum(sizes)])
    seg = lambda i: w[:, offs[i]:offs[i + 1]]
    wtok = jnp.concatenate([seg(0), seg(4), seg(5), seg(6), seg(7)], 1).astype(BF16)
    wfeat = jnp.transpose(jnp.concatenate([seg(1), seg(2)], 1)).astype(BF16)
    n_small = N_GATES + 2 * G_RANK
    wsmall = jnp.concatenate([seg(3), seg(8), jnp.zeros((d, SMALL_COLS - n_small), F32)], 1).astype(BF16)
    wgt = jnp.transpose(seg(3)).astype(BF16)
    bsm = jnp.concatenate([m_gate_b[0], jnp.zeros((SMALL_COLS - N_GATES,), F32)]).reshape(1, SMALL_COLS)
    bgt = m_gate_b[0].reshape(N_GATES, 1)
    g_mix2 = g_mix[0].reshape(1, d)
    conv_w9 = conv_w[0].reshape(9, 2 * M_QK)
    kscale = jnp.concatenate([jnp.ones((M_QK,), F32), jnp.full((M_QK,), M_DK ** -0.5, F32)]).reshape(1, 2 * M_QK)
    ggw = g_gate_w[0].astype(BF16)
    ggb = g_gate_b[0].reshape(2, 1, G_QK)

    feat_consts = (g_mix2, wtok, wfeat, wsmall, wgt, bsm, bgt, conv_w9, kscale)

    def features(inp, m6, cols, tm):
        mqk, gq, gk, gv, gr, mvt, mot, small, gt = _features(inp, m6[:, 0:1], m6[:, 1:2], feat_consts, cols, tm)
        return dict(mqk=mqk, gq=gq, gk=gk, gv=gv, gr=gr, mvt=mvt, mot=mot, small=small, gt=gt)

    fc = features(ctx, mod_c, tc, tc)
    c0 = jnp.zeros((b, 2 * M_HEADS, M_AUG, M_DK), F32)
    m0 = jnp.full((b, 2 * M_HEADS, LANE), NEG, F32)
    c_ctx_state, m_ctx_state = _mlstm_scan(fc["mqk"], fc["mvt"], fc["small"], fc["gt"], c0, m0, False,
                                           _largest_tile(tc, 256))
    s0 = jnp.zeros((b, 2, G_DV, G_QK), F32)
    (s_ctx_state,) = _gla_scan(fc["gq"], fc["gk"], fc["gv"], fc["small"], ggw, ggb, s0, False,
                               _largest_tile(tc, 256))

    fx = features(x, mod_x, GRID_W, _largest_tile(t, 512))
    hf_t, hb_t, _, _ = _mlstm_scan(fx["mqk"], fx["mvt"], fx["small"], fx["gt"], c_ctx_state, m_ctx_state, True,
                                   _largest_tile(t, 256))
    of, ob, _ = _gla_scan(fx["gq"], fx["gk"], fx["gv"], fx["small"], ggw, ggb, s_ctx_state, True,
                          _largest_tile(t, 256))

    tm = _largest_tile(t, 512)
    mod4 =jnp.stack([mod_x[:, 2], mod_x[:, 3], mod_x[:, 4], mod_x[:, 5]], 1)
    mng = jnp.broadcast_to(m_norm_g[0].reshape(M_WIDTH, 1), (M_WIDTH, tm))
    return _out_block(x, hf_t, hb_t, of, ob, fx["mot"], fx["gr"], mod4,
                      mng, g_norm_g[0].reshape(1, -1),
                      g_ffn[0].reshape(1, d), g_final.reshape(1, d),
                      w_out[0].astype(BF16), w_gu[0].astype(BF16), w_down[0].astype(BF16), tm)
```

```python
import functools

import numpy as np
import jax
import jax.numpy as jnp
from jax import lax
from jax.experimental import pallas as pl
from jax.experimental.pallas import tpu as pltpu

F32 = jnp.float32
BF16 = jnp.bfloat16

GRID_W = 64
M_HEADS = 4
M_DK = 128
M_DV = 128
M_QK = M_HEADS * M_DK
M_WIDTH = M_HEADS * M_DV
M_CHUNK = 256
G_HEADS = 4
G_DK = 64
G_DV = 128
G_QK = G_HEADS * G_DK
G_WIDTH = G_HEADS * G_DV
G_RANK = 16
G_TAU = 16.0
G_CHUNK = 64
EPS = 1e-6
NEG = -1e30
LOG2_E = 1.4426950408889634

LANE = 128
VMEM_LIMIT = 56 * 1024 * 1024


def _params(*sem):
    return pltpu.CompilerParams(dimension_semantics=sem, vmem_limit_bytes=VMEM_LIMIT)


def _sigmoid(x):
    return 1.0 / (1.0 + jnp.exp(-x))


def _log_sigmoid(x):
    return jnp.minimum(x, 0.0) - jnp.log(1.0 + jnp.exp(-jnp.abs(x)))


def _split3(x):
    x1 = x.astype(BF16)
    r1 = x - x1.astype(F32)
    x2 = r1.astype(BF16)
    x3 = (r1 - x2.astype(F32)).astype(BF16)
    return x1, x2, x3


def _dot(a, b):
    return jnp.dot(a, b, preferred_element_type=F32)


def _dot_nt(a, b):
    return lax.dot_general(a, b, (((1,), (1,)), ((), ())), preferred_element_type=F32)


def _dot_tn(a, b):
    return lax.dot_general(a, b, (((0,), (0,)), ((), ())), preferred_element_type=F32)


def _sel_left(m01, x):
    x1, x2, x3 = _split3(x)
    return _dot(m01, x1) + _dot(m01, x2) + _dot(m01, x3)


def _sel_right(x, m01):
    x1, x2, x3 = _split3(x)
    return _dot(x1, m01) + _dot(x2, m01) + _dot(x3, m01)


def _mod_kernel(c_ref, w_ref, b_ref, o_ref):
    cs = c_ref[...]
    a = (cs * _sigmoid(cs)).astype(BF16)
    o_ref[...] = _dot(a, w_ref[...].astype(BF16)) + b_ref[...]


def _modulation(c_all, w_ada, b_ada):
    rows, d = c_all.shape
    n = w_ada.shape[1]
    bn = n // 6
    return pl.pallas_call(
        _mod_kernel,
        grid=(n // bn,),
        in_specs=[pl.BlockSpec((rows, d), lambda j: (0, 0)),
                  pl.BlockSpec((d, bn), lambda j: (0, j)),
                  pl.BlockSpec((1, bn), lambda j: (0, j))],
        out_specs=pl.BlockSpec((rows, bn), lambda j: (0, j)),
        out_shape=jax.ShapeDtypeStruct((rows, n), F32),
        compiler_params=_params("arbitrary"),
        name="adaln_mod",
    )(c_all, w_ada, b_ada.reshape(1, n))


_TOK_SIZES = (("mqk", 2 * M_QK), ("gq", G_QK), ("gk", G_QK), ("gv", G_WIDTH), ("gr", G_WIDTH))
_FEAT_SIZES = (("mv", M_WIDTH), ("mo", M_WIDTH))
N_GATES = 4 * M_HEADS
SMALL_COLS = LANE


CONV_CB = 256


def _conv_silu(ext, w9, cols, nrow):
    tot = (nrow + 2) * cols
    col = lax.broadcasted_iota(jnp.int32, ext.shape, 0) % cols
    left = jnp.where(col == 0, 0.0, pltpu.roll(ext, 1, 0))
    right = jnp.where(col == cols - 1, 0.0, pltpu.roll(ext, tot - 1, 0))
    srcs = (left, ext, right)
    acc = None
    for di in range(3):
        for dj in range(3):
            term = srcs[dj][di * cols:(di + nrow) * cols] * w9[di * 3 + dj:di * 3 + dj + 1]
            acc = term if acc is None else acc + term
    return acc * _sigmoid(acc)


def _feat_kernel(cols, x_ref, xn_ref, sh_ref, sc_ref, g_ref, wtok_ref, wfeat_ref, wsmall_ref, wgt_ref,
                 bsm_ref, bgt_ref, cw_ref, ksc_ref,
                 mqk_ref, gq_ref, gk_ref, gv_ref, gr_ref, mvt_ref, mot_ref, small_ref, gt_ref, prev_s):
    j = pl.program_id(1)
    last = pl.num_programs(1) - 1
    tm = x_ref.shape[1]

    def norm_mod(x):
        h = x * lax.rsqrt(jnp.mean(x * x, -1, keepdims=True) + EPS) * g_ref[...]
        return (h * (1.0 + sc_ref[0]) + sh_ref[0]).astype(BF16)

    hb = norm_mod(x_ref[0])
    hext = jnp.concatenate([hb, norm_mod(xn_ref[0])], 0)

    @pl.when(j == 0)
    def _():
        prev_s[...] = jnp.zeros_like(prev_s)

    outs = dict(gq=gq_ref, gk=gk_ref, gv=gv_ref, gr=gr_ref, mv=mvt_ref, mo=mot_ref)
    tasks = []

    def tok_task(name, off, c0):
        def run():
            z = _dot(hb, wtok_ref[:, off + c0:off + c0 + CONV_CB])
            if name == "gq":
                z = z * (G_DK ** -0.5)
            outs[name][0, :, c0:c0 + CONV_CB] = z.astype(BF16)
        return run

    def feat_task(name, off, c0):
        def run():
            z = _dot_nt(wfeat_ref[off + c0:off + c0 + CONV_CB, :], hb)
            outs[name][0, c0:c0 + CONV_CB, :] = z.astype(BF16)
        return run

    off = 2 * M_QK
    for name, size in _TOK_SIZES[1:]:
        tasks += [tok_task(name, off, c0) for c0 in range(0, size, CONV_CB)]
        off += size
    off = 0
    for name, size in _FEAT_SIZES:
        tasks += [feat_task(name, off, c0) for c0 in range(0, size, CONV_CB)]
        off += size

    nblk = 2 * M_QK // CONV_CB
    for i in range(nblk):
        cs = slice(i * CONV_CB, (i + 1) * CONV_CB)
        z = _dot(hext, wtok_ref[:, cs])
        for task in tasks[i * len(tasks) // nblk:(i + 1) * len(tasks) // nblk]:
            task()
        below = jnp.where(j < last, z[tm:], 0.0)
        ext = jnp.concatenate([prev_s[:, cs], z[:tm], below], 0)
        prev_s[:, cs] = z[tm - cols:tm]
        mqk_ref[0, :, cs] = (_conv_silu(ext, cw_ref[:, cs], cols, tm // cols) * ksc_ref[:, cs]).astype(BF16)
    small_ref[0] = _dot(hb, wsmall_ref[...]) + bsm_ref[...]
    gt_ref[0] = _dot_nt(wgt_ref[...], hb) + bgt_ref[...]


def _features(x, shift, scale, consts, cols, tm):
    b, t, d = x.shape
    nt = t // tm
    per = tm // cols
    tok = lambda w: pl.BlockSpec((1, tm, w), lambda i, j: (i, j, 0))
    feat = lambda w: pl.BlockSpec((1, w, tm), lambda i, j: (i, 0, j))
    below = pl.BlockSpec((1, cols, d), lambda i, j: (i, jnp.minimum((j + 1) * per, t // cols - 1), 0))
    const = lambda a: pl.BlockSpec(a.shape, lambda i, j: (0,) * a.ndim)
    vec = pl.BlockSpec((1, 1, d), lambda i, j: (i, 0, 0))
    out_shape = [jax.ShapeDtypeStruct((b, t, s), BF16) for _, s in _TOK_SIZES]
    out_specs = [tok(s) for _, s in _TOK_SIZES]
    out_shape += [jax.ShapeDtypeStruct((b, s, t), BF16) for _, s in _FEAT_SIZES]
    out_specs += [feat(s) for _, s in _FEAT_SIZES]
    out_shape += [jax.ShapeDtypeStruct((b, t, SMALL_COLS), F32), jax.ShapeDtypeStruct((b, N_GATES, t), F32)]
    out_specs += [tok(SMALL_COLS), feat(N_GATES)]
    return pl.pallas_call(
        functools.partial(_feat_kernel, cols),
        grid=(b, nt),
        in_specs=[tok(d), below, vec, vec] + [const(a) for a in consts],
        out_specs=out_specs,
        out_shape=out_shape,
        scratch_shapes=[pltpu.VMEM((cols, 2 * M_QK), F32)],
        compiler_params=_params("parallel", "arbitrary"),
        name="features",
    )(x, x, shift, scale, *consts)


M_AUG = M_DV + 16


def _mlstm_cumsum_matrix(reverse):
    L = M_CHUNK
    low = np.tril(np.ones((L, L), np.float32))
    tri = low.T if reverse else low
    return jnp.asarray(np.tile(np.concatenate([tri.T, np.ones((L, L), np.float32)], 1), (3, 1)), BF16)


def _mlstm_kernel(emit, nch, qf_ref, kf_ref, vf_ref, grf_ref, qb_ref, kb_ref, vb_ref, grb_ref,
                  rowf_ref, rowb_ref, c0_ref, m0_ref, *rest):
    if emit:
        hf_ref, hb_ref, cout_ref, mout_ref, c_s, m_s = rest
    else:
        cout_ref, mout_ref, c_s, m_s = rest
        hf_ref = hb_ref = None
    j = pl.program_id(1)
    L = M_CHUNK

    @pl.when(j == 0)
    def _():
        c_s[...] = c0_ref[0]
        m_s[...] = m0_ref[0]

    sid = lax.broadcasted_iota(jnp.int32, (L, L), 0)
    tid = lax.broadcasted_iota(jnp.int32, (L, L), 1)
    ones_rows = jnp.ones((M_AUG - M_DV, L), BF16)

    refs = ((qf_ref, kf_ref, vf_ref, grf_ref, rowf_ref, hf_ref),
            (qb_ref, kb_ref, vb_ref, grb_ref, rowb_ref, hb_ref))
    masks = (sid <= tid, sid >= tid)

    def gate_stage(i):
        chains = []
        for d in range(2):
            q_ref, k_ref, v_ref, gr_ref, row_ref, h_ref = refs[d]
            c = i if d == 0 else nch - 1 - i
            tok = slice(c * L, (c + 1) * L)
            grow = gr_ref[0, :, tok]
            cum_ext = _dot(jnp.concatenate(_split3(_log_sigmoid(grow) * LOG2_E), 1), row_ref[...])
            grow = grow * LOG2_E
            cum_col = jnp.transpose(cum_ext[:, :L])
            gcol = jnp.transpose(grow)
            for hd in range(M_HEADS):
                ci = 2 * M_HEADS * d + hd
                cf = ci + M_HEADS
                sl = slice(hd * M_DK, (hd + 1) * M_DK)
                chains.append(dict(
                    d=d, r=d * M_HEADS + hd, sl=sl, tok=tok,
                    q=q_ref[0, tok, sl], k=k_ref[0, tok, sl],
                    v_aug=jnp.concatenate([v_ref[0, sl, tok], ones_rows], 0),
                    br=cum_ext[cf:cf + 1, :L],
                    btot=cum_ext[cf:cf + 1, L:],
                    ir=grow[ci:ci + 1, :],
                    cc=gcol[:, ci:ci + 1] - cum_col[:, cf:cf + 1]))
        return chains

    for chains in [gate_stage(i) for i in range(nch)]:
        for ch in chains:
            r = ch["r"]
            m = jnp.concatenate([m_s[r:r + 1, :]] * (L // LANE), 1)
            ct = c_s[r]
            ch["m"] = m
            if emit:
                ch["big"] = _dot_nt(jnp.concatenate([ch["k"], ct.astype(BF16)], 0), ch["q"])
            src = ch["btot"] + ch["ir"] - ch["br"]
            m_new = jnp.maximum(m + ch["btot"], jnp.max(src, -1, keepdims=True))
            ws = jnp.exp2(src - m_new).astype(BF16)
            wc = jnp.exp2((m + ch["btot"] - m_new)[:, :M_DK])
            c_s[r] = wc * ct + _dot(ch["v_aug"] * ws, ch["k"])
            m_s[r:r + 1, :] = m_new[:, :LANE]

        if emit:
            for ch in chains:
                m, br, big = ch["m"], ch["br"], ch["big"]
                dmt = jnp.where(masks[ch["d"]], ch["cc"] + br, NEG)
                mt = jnp.maximum(m + br, jnp.max(dmt, 0, keepdims=True))
                pt = big[:L] * jnp.exp2(dmt - mt)
                num = jnp.exp2(m + br - mt) * big[L:] + _dot(ch["v_aug"], pt.astype(BF16))
                inv = 1.0 / jnp.maximum(jnp.abs(num[M_DV:M_DV + 1]), jnp.exp2(-mt))
                refs[ch["d"]][5][0, ch["sl"], ch["tok"]] = num[:M_DV] * inv

    @pl.when(j == pl.num_programs(1) - 1)
    def _():
        cout_ref[0] = c_s[...]
        mout_ref[0] = m_s[...]


def _mlstm_scan(mqk, mvt, gt, c0, m0, emit, tb):
    b, _, t = mvt.shape
    nb = t // tb
    consts = (_mlstm_cumsum_matrix(False), _mlstm_cumsum_matrix(True))
    fw = lambda w, cblk: pl.BlockSpec((1, tb, w), lambda i, j: (i, j, cblk))
    bw = lambda w, cblk: pl.BlockSpec((1, tb, w), lambda i, j: (i, nb - 1 - j, cblk))
    ftf = lambda w: pl.BlockSpec((1, w, tb), lambda i, j: (i, 0, j))
    ftb = lambda w: pl.BlockSpec((1, w, tb), lambda i, j: (i, 0, nb - 1 - j))
    const = lambda a: pl.BlockSpec(a.shape, lambda i, j: (0,) * a.ndim)
    cst = pl.BlockSpec((1, 2 * M_HEADS, M_AUG, M_DK), lambda i, j: (i, 0, 0, 0))
    mst = pl.BlockSpec((1, 2 * M_HEADS, LANE), lambda i, j: (i, 0, 0))
    in_specs = [fw(M_QK, 0), fw(M_QK, 1), ftf(M_WIDTH), ftf(N_GATES),
                bw(M_QK, 0), bw(M_QK, 1), ftb(M_WIDTH), ftb(N_GATES)]
    in_specs += [const(a) for a in consts] + [cst, mst]
    out_specs = [cst, mst]
    out_shape = [jax.ShapeDtypeStruct(c0.shape, F32), jax.ShapeDtypeStruct(m0.shape, F32)]
    if emit:
        out_specs = [ftf(M_WIDTH), ftb(M_WIDTH)] + out_specs
        out_shape = [jax.ShapeDtypeStruct((b, M_WIDTH, t), F32)] * 2 + out_shape
    return pl.pallas_call(
        functools.partial(_mlstm_kernel, emit, tb // M_CHUNK),
        grid=(b, nb),
        in_specs=in_specs,
        out_specs=out_specs,
        out_shape=out_shape,
        scratch_shapes=[pltpu.VMEM((2 * M_HEADS, M_AUG, M_DK), F32), pltpu.VMEM((2 * M_HEADS, LANE), F32)],
        compiler_params=_params("parallel", "arbitrary"),
        name="mlstm_scan" if emit else "mlstm_ctx_state",
    )(mqk, mqk, mvt, gt, mqk, mqk, mvt, gt, *consts, c0, m0)


G_LEVELS = (32, 16, 8, 4, 2, 1)


def _gla_select_matrices(reverse):
    L = G_CHUNK
    t = np.arange(L)
    u = np.arange(L)[None, :]
    tt = t[:, None]
    mats = [(u <= tt), (u > tt)]
    for h in G_LEVELS:
        start = (tt // (2 * h)) * (2 * h)
        second = tt >= start + h
        m = np.where(second, (u >= start + h) & (u <= tt), (u > tt) & (u < start + h))
        mats.append(m)
    m = np.concatenate([x.astype(np.float32) for x in mats], 0)
    if reverse:
        m = m.reshape(len(mats), L, L)[:, ::-1, ::-1].reshape(len(mats) * L, L)
    return jnp.asarray(np.concatenate([m, m, m], 1), BF16)


def _gla_pair_masks(reverse):
    L = G_CHUNK
    s = np.arange(L)[:, None]
    t = np.arange(L)[None, :]
    masks = [s == t]
    for h in G_LEVELS:
        same = (s // (2 * h)) == (t // (2 * h))
        masks.append(same & ((t // h) % 2 == 1) & ((s // h) % 2 == 0))
    m = np.stack(masks).astype(np.float32)
    if reverse:
        m = m[:, ::-1, ::-1]
    return jnp.asarray(np.tile(m, (1, 1, G_HEADS)), F32)


def _gla_fast_matrices(reverse):
    L = G_CHUNK
    h = L // 2
    tt = np.arange(L)[:, None]
    u = np.arange(L)[None, :]
    mid = np.where(tt >= h, ((u >= h) & (u <= tt)).astype(np.float32), -((u > tt) & (u < h)).astype(np.float32))
    m = np.stack([(u <= tt).astype(np.float32), (u > tt).astype(np.float32), mid])
    if reverse:
        m = m[:, ::-1, ::-1]
    m = m.reshape(3 * L, L)
    return jnp.asarray(np.concatenate([m, m, m], 1), BF16)


def _gla_causal_mask(reverse):
    s = np.arange(G_CHUNK)[:, None]
    t = np.arange(G_CHUNK)[None, :]
    m = (s >= t) if reverse else (s <= t)
    return jnp.asarray(np.tile(m.astype(np.float32), (1, G_HEADS)), F32)


def _gla_head_mask():
    r = np.arange(G_HEADS * G_CHUNK)[:, None] // G_CHUNK
    c = np.arange(G_QK)[None, :] // G_DK
    return jnp.asarray((r == c).astype(np.float32), BF16)


GLA_FAST_LIMIT = 40.0


def _gla_kernel(emit, nch, pipelined, qf_ref, kf_ref, vf_ref, lrf_ref, qb_ref, kb_ref, vb_ref, lrb_ref,
                lrnf_ref, lrnb_ref, lvf_ref, lvb_ref, fastf_ref, fastb_ref, mkf_ref, mkb_ref, cmf_ref, cmb_ref,
                hm_ref, gw_ref, gb_ref, s0_ref, *rest):
    if emit:
        of_ref, ob_ref, sout_ref, s_s, a3_s, lo_s = rest
    else:
        sout_ref, s_s, a3_s, lo_s = rest
        of_ref = ob_ref = None
    j = pl.program_id(1)
    L = G_CHUNK
    slot = j % 2

    def stage_decays(lr_refs, dst):
        lo = None
        for d, lr_ref in enumerate(lr_refs):
            lr = lr_ref[0][:, N_GATES + d * G_RANK:N_GATES + (d + 1) * G_RANK].astype(BF16)
            loga = _log_sigmoid(_dot(lr, gw_ref[d]) + gb_ref[d]) * (LOG2_E / G_TAU)
            pieces = _split3(loga)
            a3_s[dst, d] = jnp.concatenate(
                [jnp.concatenate([p[c * L:(c + 1) * L] for p in pieces], 0) for c in range(nch)], 1)
            m = jnp.min(loga)
            lo = m if lo is None else jnp.minimum(lo, m)
        lo_s[dst] = lo * L

    @pl.when(j == 0)
    def _():
        s_s[...] = s0_ref[0]
        stage_decays((lrf_ref, lrb_ref), 0)

    hm = hm_ref[...]
    stack = lambda a: jnp.concatenate([a] * G_HEADS, 0)

    refs = ((qf_ref, kf_ref, vf_ref, mkf_ref, cmf_ref, of_ref), (qb_ref, kb_ref, vb_ref, mkb_ref, cmb_ref, ob_ref))

    def run(fast):
        chunks = []
        for d in range(2):
            q_ref, k_ref, v_ref, mk_ref, cm_ref, o_ref = refs[d]
            sel_ref = ((fastf_ref, fastb_ref) if fast else (lvf_ref, lvb_ref))[d]
            g = _dot(sel_ref[...], a3_s[slot, d])
            expo = jnp.exp2(g[:2 * L] if fast else g)
            for c in (range(nch) if d == 0 else reversed(range(nch))):
                rows = slice(c * L, (c + 1) * L)
                cols = slice(c * G_QK, (c + 1) * G_QK)
                chunks.append(dict(
                    d=d, rows=rows, cols=cols, g=g, expo=expo, k=k_ref[0, rows, :],
                    v_st=jnp.concatenate([v_ref[0, rows, hd * G_DV:(hd + 1) * G_DV] for hd in range(G_HEADS)], 0),
                    e_tot=expo[L - 1:L, cols] if d == 0 else expo[0:1, cols]))

        def e(ch, i):
            return ch["expo"][i * L:(i + 1) * L, ch["cols"]].astype(BF16)

        for ch in chunks:
            k = ch["k"]
            ch["kd_st"] = stack(k * e(ch, 1)) * hm
            if emit:
                q_ref, mk_ref, cm_ref = refs[ch["d"]][0], refs[ch["d"]][3], refs[ch["d"]][4]
                qs = stack(q_ref[0, ch["rows"], :]) * hm
                ch["qs"] = qs
                if fast:
                    gm = ch["g"][2 * L:3 * L, ch["cols"]]
                    att_t = _dot_nt(k * jnp.exp2(-gm).astype(BF16), qs * stack(jnp.exp2(gm).astype(BF16)))
                    att_t = jnp.where(cm_ref[...] != 0.0, att_t, 0.0)
                else:
                    att_t = mk_ref[0] * _dot_nt(k, qs)
                    for lv in range(len(G_LEVELS)):
                        ev = e(ch, 2 + lv)
                        att_t = att_t + mk_ref[1 + lv] * _dot_nt(k * ev, qs * stack(ev))
                ch["att_t"] = att_t

        if pipelined:
            stage_decays((lrnf_ref, lrnb_ref), 1 - slot)

        for ch in chunks:
            if emit:
                att_st = stack(ch["att_t"].astype(BF16)) * hm
                ch["res"] = _dot_tn(ch["v_st"], jnp.concatenate([att_st, ch["kd_st"]], 1))
            else:
                ch["res"] = _dot_tn(ch["v_st"], ch["kd_st"])

        order = [chunks[dd * nch + i] for i in range(nch) for dd in range(2)]
        for ch in order:
            d = ch["d"]
            st = s_s[d]
            res = ch["res"]
            if emit:
                inter_t = _dot_nt(st.astype(BF16), ch["qs"] * stack(e(ch, 0)))
                o = jnp.transpose(inter_t + res[:, :G_HEADS * L])
                o_ref = refs[d][5]
                for hd in range(G_HEADS):
                    o_ref[0, ch["rows"], hd * G_DV:(hd + 1) * G_DV] = o[hd * L:(hd + 1) * L]
                res = res[:, G_HEADS * L:]
            s_s[d] = ch["e_tot"] * st + res

    if emit:
        mild = lo_s[slot] >= -GLA_FAST_LIMIT
        pl.when(mild)(lambda: run(True))
        pl.when(jnp.logical_not(mild))(lambda: run(False))
    else:
        run(True)

    @pl.when(j == pl.num_programs(1) - 1)
    def _():
        sout_ref[0] = s_s[...]


def _gla_scan(gq, gk, gv, small, g_gate_w, g_gate_b, s0, emit, tb):
    b, t, _ = gv.shape
    nb = t // tb
    consts = (_gla_select_matrices(False), _gla_select_matrices(True),
              _gla_fast_matrices(False), _gla_fast_matrices(True),
              _gla_pair_masks(False), _gla_pair_masks(True),
              _gla_causal_mask(False), _gla_causal_mask(True), _gla_head_mask(), g_gate_w, g_gate_b)
    fw = lambda w: pl.BlockSpec((1, tb, w), lambda i, j: (i, j, 0))
    bw = lambda w: pl.BlockSpec((1, tb, w), lambda i, j: (i, nb - 1 - j, 0))
    const = lambda a: pl.BlockSpec(a.shape, lambda i, j: (0,) * a.ndim)
    sst = pl.BlockSpec((1, 2, G_DV, G_QK), lambda i, j: (i, 0, 0, 0))
    fw_next = pl.BlockSpec((1, tb, SMALL_COLS), lambda i, j: (i, jnp.minimum(j + 1, nb - 1), 0))
    bw_next = pl.BlockSpec((1, tb, SMALL_COLS), lambda i, j: (i, jnp.maximum(nb - 2 - j, 0), 0))
    in_specs = [fw(G_QK), fw(G_QK), fw(G_WIDTH), fw(SMALL_COLS),
                bw(G_QK), bw(G_QK), bw(G_WIDTH), bw(SMALL_COLS), fw_next, bw_next]
    in_specs += [const(a) for a in consts] + [sst]
    out_specs = [sst]
    out_shape = [jax.ShapeDtypeStruct(s0.shape, F32)]
    if emit:
        out_specs = [fw(G_WIDTH), bw(G_WIDTH)] + out_specs
        out_shape = [jax.ShapeDtypeStruct((b, t, G_WIDTH), F32)] * 2 + out_shape
    nch = tb // G_CHUNK
    return pl.pallas_call(
        functools.partial(_gla_kernel, emit, nch, nb > 1),
        grid=(b, nb),
        in_specs=in_specs,
        out_specs=out_specs,
        out_shape=out_shape,
        scratch_shapes=[pltpu.VMEM((2, G_DV, G_QK), F32),
                        pltpu.VMEM((2, 2, 3 * G_CHUNK, nch * G_QK), BF16),
                        pltpu.SMEM((2,), F32)],
        compiler_params=_params("parallel", "arbitrary"),
        name="gla_scan" if emit else "gla_ctx_state",
    )(gq, gk, gv, small, gq, gk, gv, small, small, small, *consts, s0)


FFN_CHUNK = 256


def _head_norm(h, nheads, width):
    parts = []
    for hd in range(nheads):
        blk = h[:, hd * width:(hd + 1) * width]
        parts.append(blk * lax.rsqrt(jnp.mean(blk * blk, -1, keepdims=True) + EPS))
    return jnp.concatenate(parts, 1)


def _out_kernel(d_ff, x_ref, hf_ref, hb_ref, of_ref, ob_ref, mo_ref, gr_ref, mod_ref,
                mng_ref, gng_ref, gffn_ref, gfin_ref, wout_ref, wgu_ref, wdown_ref, o_ref):
    x = x_ref[0]
    mod = mod_ref[0]
    hm_t = hf_ref[0] + hb_ref[0]
    parts = []
    for hd in range(M_HEADS):
        blk = hm_t[hd * M_DV:(hd + 1) * M_DV]
        parts.append(blk * lax.rsqrt(jnp.mean(blk * blk, 0, keepdims=True) + EPS))
    hm_t = jnp.concatenate(parts, 0) * mng_ref[...] * _sigmoid(mo_ref[0].astype(F32))
    gr = gr_ref[0].astype(F32)
    hg = _head_norm(of_ref[0] + ob_ref[0], G_HEADS, G_DV) * gng_ref[...] * (gr * _sigmoid(gr))
    mixed = _dot_tn(hm_t.astype(BF16), wout_ref[:M_WIDTH, :]) + _dot(hg.astype(BF16), wout_ref[M_WIDTH:, :])
    x1 = x + mod[0:1] * mixed
    h2 = x1 * lax.rsqrt(jnp.mean(x1 * x1, -1, keepdims=True) + EPS) * gffn_ref[...]
    h2 = (h2 * (1.0 + mod[2:3]) + mod[1:2]).astype(BF16)
    def gated(c0):
        a = _dot(h2, wgu_ref[:, c0:c0 + FFN_CHUNK])
        g = _dot(h2, wgu_ref[:, d_ff + c0:d_ff + c0 + FFN_CHUNK])
        return (a * _sigmoid(a) * g).astype(BF16)

    starts = list(range(0, d_ff, FFN_CHUNK))
    act = gated(starts[0])
    acc = None
    for i, c0 in enumerate(starts):
        nxt = gated(starts[i + 1]) if i + 1 < len(starts) else None
        part = _dot(act, wdown_ref[c0:c0 + FFN_CHUNK, :])
        acc = part if acc is None else acc + part
        act = nxt
    x2 = x1 + mod[3:4] * acc
    o_ref[0] = x2 * lax.rsqrt(jnp.mean(x2 * x2, -1, keepdims=True) + EPS) * gfin_ref[...]


def _out_block(x, hf, hb, of, ob, mo, gr, mod4, mng, gng, gffn, gfin, wout, wgu, wdown, tm):
    b, t, d = x.shape
    d_ff = wdown.shape[0]
    tok = lambda w: pl.BlockSpec((1, tm, w), lambda i, j: (i, j, 0))
    feat = lambda w: pl.BlockSpec((1, w, tm), lambda i, j: (i, 0, j))
    const = lambda a: pl.BlockSpec(a.shape, lambda i, j: (0,) * a.ndim, pipeline_mode=pl.Buffered(1))
    return pl.pallas_call(
        functools.partial(_out_kernel, d_ff),
        grid=(b, t // tm),
        in_specs=[tok(d), feat(M_WIDTH), feat(M_WIDTH), tok(G_WIDTH), tok(G_WIDTH), feat(M_WIDTH), tok(G_WIDTH),
                  pl.BlockSpec((1, 4, d), lambda i, j: (i, 0, 0)),
                  const(mng), const(gng), const(gffn), const(gfin), const(wout), const(wgu), const(wdown)],
        out_specs=tok(d),
        out_shape=jax.ShapeDtypeStruct((b, t, d), F32),
        compiler_params=_params("parallel", "arbitrary"),
        name="out_ffn",
    )(x, hf, hb, of, ob, mo, gr, mod4, mng, gng, gffn, gfin, wout, wgu, wdown)


def _largest_tile(t, cap):
    tm = min(t, cap)
    while t % tm:
        tm //= 2
    return tm


def kernel(x, c, ctx, c_ctx, w_ada, b_ada, g_mix, w_in, conv_w, m_gate_b, m_norm_g, g_gate_w, g_gate_b,
           g_norm_g, w_out, g_ffn, w_gu, w_down, g_final):
    b, t, d = x.shape
    tc = ctx.shape[1]
    assert w_ada.shape[0] == 1, "single layer block"
    assert t % (GRID_W * 2) == 0 and tc % M_CHUNK == 0

    pad = (-(b + 1)) % 8
    c_all = jnp.concatenate([c, c_ctx[None, :], jnp.zeros((pad, d), F32)], 0)
    mod = _modulation(c_all, w_ada[0], b_ada[0])
    mod_x = mod[:b].reshape(b, 6, d)
    mod_c = jnp.broadcast_to(mod[b].reshape(1, 6, d), (b, 6, d))

    w = w_in[0]
    sizes = (2 * M_QK, M_WIDTH, M_WIDTH, N_GATES, G_QK, G_QK, G_WIDTH, G_WIDTH, 2 * G_RANK)
    offs = np.concatenate([[0], np.cumsum(sizes)])
    seg = lambda i: w[:, offs[i]:offs[i + 1]]
    wtok = jnp.concatenate([seg(0), seg(4), seg(5), seg(6), seg(7)], 1).astype(BF16)
    wfeat = jnp.transpose(jnp.concatenate([seg(1), seg(2)], 1)).astype(BF16)
    n_small = N_GATES + 2 * G_RANK
    wsmall = jnp.concatenate([seg(3), seg(8), jnp.zeros((d, SMALL_COLS - n_small), F32)], 1).astype(BF16)
    wgt = jnp.transpose(seg(3)).astype(BF16)
    bsm = jnp.concatenate([m_gate_b[0], jnp.zeros((SMALL_COLS - N_GATES,), F32)]).reshape(1, SMALL_COLS)
    bgt = m_gate_b[0].reshape(N_GATES, 1)
    g_mix2 = g_mix[0].reshape(1, d)
    conv_w9 = conv_w[0].reshape(9, 2 * M_QK)
    kscale = jnp.concatenate([jnp.ones((M_QK,), F32), jnp.full((M_QK,), M_DK ** -0.5, F32)]).reshape(1, 2 * M_QK)
    ggw = g_gate_w[0].astype(BF16)
    ggb = g_gate_b[0].reshape(2, 1, G_QK)

    feat_consts = (g_mix2, wtok, wfeat, wsmall, wgt, bsm, bgt, conv_w9, kscale)

    def features(inp, m6, cols, tm):
        mqk, gq, gk, gv, gr, mvt, mot, small, gt = _features(inp, m6[:, 0:1], m6[:, 1:2], feat_consts, cols, tm)
        return dict(mqk=mqk, gq=gq, gk=gk, gv=gv, gr=gr, mvt=mvt, mot=mot, small=small, gt=gt)

    fc = features(ctx, mod_c, tc, tc)
    c0 = jnp.zeros((b, 2 * M_HEADS, M_AUG, M_DK), F32)
    m0 = jnp.full((b, 2 * M_HEADS, LANE), NEG, F32)
    c_ctx_state, m_ctx_state = _mlstm_scan(fc["mqk"], fc["mvt"], fc["gt"], c0, m0, False,
                                           _largest_tile(tc, M_CHUNK))
    s0 = jnp.zeros((b, 2, G_DV, G_QK), F32)
    (s_ctx_state,) = _gla_scan(fc["gq"], fc["gk"], fc["gv"], fc["small"], ggw, ggb, s0, False,
                               _largest_tile(tc, 256))

    fx = features(x, mod_x, GRID_W, _largest_tile(t, 512))
    hf_t, hb_t, _, _ = _mlstm_scan(fx["mqk"], fx["mvt"], fx["gt"], c_ctx_state, m_ctx_state, True,
                                   _largest_tile(t, 1024))
    of, ob, _ = _gla_scan(fx["gq"], fx["gk"], fx["gv"], fx["small"], ggw, ggb, s_ctx_state, True,
                          _largest_tile(t, 256))

    tm = _largest_tile(t, 512)
    mod4 =jnp.stack([mod_x[:, 2], mod_x[:, 3], mod_x[:, 4], mod_x[:, 5]], 1)
    mng = jnp.broadcast_to(m_norm_g[0].reshape(M_WIDTH, 1), (M_WIDTH, tm))
    return _out_block(x, hf_t, hb_t, of, ob, fx["mot"], fx["gr"], mod4,
                      mng, g_norm_g[0].reshape(1, -1),
                      g_ffn[0].reshape(1, d), g_final.reshape(1, d),
                      w_out[0].astype(BF16), w_gu[0].astype(BF16), w_down[0].astype(BF16), tm)
```

```python
import functools

import numpy as np
import jax
import jax.numpy as jnp
from jax import lax
from jax.experimental import pallas as pl
from jax.experimental.pallas import tpu as pltpu

F32 = jnp.float32
BF16 = jnp.bfloat16

GRID_W = 64
M_HEADS = 4
M_DK = 128
M_DV = 128
M_QK = M_HEADS * M_DK
M_WIDTH = M_HEADS * M_DV
M_CHUNK = 256
G_HEADS = 4
G_DK = 64
G_DV = 128
G_QK = G_HEADS * G_DK
G_WIDTH = G_HEADS * G_DV
G_RANK = 16
G_TAU = 16.0
G_CHUNK = 64
EPS = 1e-6
NEG = -1e30
LOG2_E = 1.4426950408889634

LANE = 128
VMEM_LIMIT = 56 * 1024 * 1024


def _params(*sem):
    return pltpu.CompilerParams(dimension_semantics=sem, vmem_limit_bytes=VMEM_LIMIT)


def _sigmoid(x):
    return 1.0 / (1.0 + jnp.exp(-x))


def _log_sigmoid(x):
    return jnp.minimum(x, 0.0) - jnp.log(1.0 + jnp.exp(-jnp.abs(x)))


def _split(x, n):
    pieces = []
    for _ in range(n - 1):
        p = x.astype(BF16)
        pieces.append(p)
        x = x - p.astype(F32)
    pieces.append(x.astype(BF16))
    return pieces


def _split3(x):
    return _split(x, 3)


G_SPLIT = 2


def _dot(a, b):
    return jnp.dot(a, b, preferred_element_type=F32)


def _dot_nt(a, b):
    return lax.dot_general(a, b, (((1,), (1,)), ((), ())), preferred_element_type=F32)


def _dot_tn(a, b):
    return lax.dot_general(a, b, (((0,), (0,)), ((), ())), preferred_element_type=F32)


def _mod_kernel(c_ref, w_ref, b_ref, o_ref):
    cs = c_ref[...]
    a = (cs * _sigmoid(cs)).astype(BF16)
    o_ref[...] = _dot(a, w_ref[...].astype(BF16)) + b_ref[...]


def _modulation(c_all, w_ada, b_ada):
    rows, d = c_all.shape
    n = w_ada.shape[1]
    bn = n // 6
    return pl.pallas_call(
        _mod_kernel,
        grid=(n // bn,),
        in_specs=[pl.BlockSpec((rows, d), lambda j: (0, 0)),
                  pl.BlockSpec((d, bn), lambda j: (0, j)),
                  pl.BlockSpec((1, bn), lambda j: (0, j))],
        out_specs=pl.BlockSpec((rows, bn), lambda j: (0, j)),
        out_shape=jax.ShapeDtypeStruct((rows, n), F32),
        compiler_params=_params("arbitrary"),
        name="adaln_mod",
    )(c_all, w_ada, b_ada.reshape(1, n))


_TOK_SIZES = (("mqk", 2 * M_QK), ("gq", G_QK), ("gk", G_QK), ("gv", G_WIDTH), ("gr", G_WIDTH))
_FEAT_SIZES = (("mv", M_WIDTH), ("mo", M_WIDTH))
N_GATES = 4 * M_HEADS
SMALL_COLS = LANE


CONV_CB = 256


def _conv_silu(ext, w9, cols, nrow):
    tot = (nrow + 2) * cols
    col = lax.broadcasted_iota(jnp.int32, ext.shape, 0) % cols
    left = jnp.where(col == 0, 0.0, pltpu.roll(ext, 1, 0))
    right = jnp.where(col == cols - 1, 0.0, pltpu.roll(ext, tot - 1, 0))
    srcs = (left, ext, right)
    acc = None
    for di in range(3):
        for dj in range(3):
            term = srcs[dj][di * cols:(di + nrow) * cols] * w9[di * 3 + dj:di * 3 + dj + 1]
            acc = term if acc is None else acc + term
    return acc * _sigmoid(acc)


def _feat_kernel(cols, x_ref, xn_ref, sh_ref, sc_ref, g_ref, wtok_ref, wfeat_ref, wsmall_ref, wgt_ref,
                 bsm_ref, bgt_ref, cw_ref, ksc_ref,
                 mqk_ref, gq_ref, gk_ref, gv_ref, gr_ref, mvt_ref, mot_ref, small_ref, gt_ref, prev_s):
    j = pl.program_id(1)
    last = pl.num_programs(1) - 1
    tm = x_ref.shape[1]

    def norm_mod(x):
        h = x * lax.rsqrt(jnp.mean(x * x, -1, keepdims=True) + EPS) * g_ref[...]
        return (h * (1.0 + sc_ref[0]) + sh_ref[0]).astype(BF16)

    hb = norm_mod(x_ref[0])
    hext = jnp.concatenate([hb, norm_mod(xn_ref[0])], 0)

    @pl.when(j == 0)
    def _():
        prev_s[...] = jnp.zeros_like(prev_s)

    outs = dict(gq=gq_ref, gk=gk_ref, gv=gv_ref, gr=gr_ref, mv=mvt_ref, mo=mot_ref)
    tasks = []

    def tok_task(name, off, c0):
        def run():
            z = _dot(hb, wtok_ref[:, off + c0:off + c0 + CONV_CB])
            if name == "gq":
                z = z * (G_DK ** -0.5)
            outs[name][0, :, c0:c0 + CONV_CB] = z.astype(BF16)
        return run

    def feat_task(name, off, c0):
        def run():
            z = _dot_nt(wfeat_ref[off + c0:off + c0 + CONV_CB, :], hb)
            outs[name][0, c0:c0 + CONV_CB, :] = z.astype(BF16)
        return run

    off = 2 * M_QK
    for name, size in _TOK_SIZES[1:]:
        tasks += [tok_task(name, off, c0) for c0 in range(0, size, CONV_CB)]
        off += size
    off = 0
    for name, size in _FEAT_SIZES:
        tasks += [feat_task(name, off, c0) for c0 in range(0, size, CONV_CB)]
        off += size

    nblk = 2 * M_QK // CONV_CB
    for i in range(nblk):
        cs = slice(i * CONV_CB, (i + 1) * CONV_CB)
        z = _dot(hext, wtok_ref[:, cs])
        for task in tasks[i * len(tasks) // nblk:(i + 1) * len(tasks) // nblk]:
            task()
        below = jnp.where(j < last, z[tm:], 0.0)
        ext = jnp.concatenate([prev_s[:, cs], z[:tm], below], 0)
        prev_s[:, cs] = z[tm - cols:tm]
        mqk_ref[0, :, cs] = (_conv_silu(ext, cw_ref[:, cs], cols, tm // cols) * ksc_ref[:, cs]).astype(BF16)
    small_ref[0] = _dot(hb, wsmall_ref[...]) + bsm_ref[...]
    gt_ref[0] = _dot_nt(wgt_ref[...], hb) + bgt_ref[...]


def _features(x, shift, scale, consts, cols, tm):
    b, t, d = x.shape
    nt = t // tm
    per = tm // cols
    tok = lambda w: pl.BlockSpec((1, tm, w), lambda i, j: (i, j, 0))
    feat = lambda w: pl.BlockSpec((1, w, tm), lambda i, j: (i, 0, j))
    below = pl.BlockSpec((1, cols, d), lambda i, j: (i, jnp.minimum((j + 1) * per, t // cols - 1), 0))
    const = lambda a: pl.BlockSpec(a.shape, lambda i, j: (0,) * a.ndim)
    vec = pl.BlockSpec((1, 1, d), lambda i, j: (i, 0, 0))
    out_shape = [jax.ShapeDtypeStruct((b, t, s), BF16) for _, s in _TOK_SIZES]
    out_specs = [tok(s) for _, s in _TOK_SIZES]
    out_shape += [jax.ShapeDtypeStruct((b, s, t), BF16) for _, s in _FEAT_SIZES]
    out_specs += [feat(s) for _, s in _FEAT_SIZES]
    out_shape += [jax.ShapeDtypeStruct((b, t, SMALL_COLS), F32), jax.ShapeDtypeStruct((b, N_GATES, t), F32)]
    out_specs += [tok(SMALL_COLS), feat(N_GATES)]
    return pl.pallas_call(
        functools.partial(_feat_kernel, cols),
        grid=(b, nt),
        in_specs=[tok(d), below, vec, vec] + [const(a) for a in consts],
        out_specs=out_specs,
        out_shape=out_shape,
        scratch_shapes=[pltpu.VMEM((cols, 2 * M_QK), F32)],
        compiler_params=_params("parallel", "arbitrary"),
        name="features",
    )(x, x, shift, scale, *consts)


M_AUG = M_DV + 16


def _mlstm_cumsum_matrix(reverse):
    L = M_CHUNK
    low = np.tril(np.ones((L, L), np.float32))
    tri = low.T if reverse else low
    return jnp.asarray(np.tile(np.concatenate([tri.T, np.ones((L, L), np.float32)], 1), (3, 1)), BF16)


def _mlstm_kernel(emit, nch, qf_ref, kf_ref, vf_ref, grf_ref, qb_ref, kb_ref, vb_ref, grb_ref,
                  rowf_ref, rowb_ref, c0_ref, m0_ref, *rest):
    if emit:
        hf_ref, hb_ref, cout_ref, mout_ref, c_s, m_s = rest
    else:
        cout_ref, mout_ref, c_s, m_s = rest
        hf_ref = hb_ref = None
    j = pl.program_id(1)
    L = M_CHUNK

    @pl.when(j == 0)
    def _():
        c_s[...] = c0_ref[0]
        m_s[...] = m0_ref[0]

    sid = lax.broadcasted_iota(jnp.int32, (L, L), 0)
    tid = lax.broadcasted_iota(jnp.int32, (L, L), 1)
    ones_rows = jnp.ones((M_AUG - M_DV, L), BF16)

    refs = ((qf_ref, kf_ref, vf_ref, grf_ref, rowf_ref, hf_ref),
            (qb_ref, kb_ref, vb_ref, grb_ref, rowb_ref, hb_ref))
    masks = (sid <= tid, sid >= tid)

    def gate_stage(i):
        chains = []
        for d in range(2):
            q_ref, k_ref, v_ref, gr_ref, row_ref, h_ref = refs[d]
            c = i if d == 0 else nch - 1 - i
            tok = slice(c * L, (c + 1) * L)
            grow = gr_ref[0, :, tok]
            cum_ext = _dot(jnp.concatenate(_split3(_log_sigmoid(grow) * LOG2_E), 1), row_ref[...])
            grow = grow * LOG2_E
            cum_col = jnp.transpose(cum_ext[:, :L])
            gcol = jnp.transpose(grow)
            for hd in range(M_HEADS):
                ci = 2 * M_HEADS * d + hd
                cf = ci + M_HEADS
                sl = slice(hd * M_DK, (hd + 1) * M_DK)
                chains.append(dict(
                    d=d, r=d * M_HEADS + hd, sl=sl, tok=tok,
                    q=q_ref[0, tok, sl], k=k_ref[0, tok, sl],
                    v_aug=jnp.concatenate([v_ref[0, sl, tok], ones_rows], 0),
                    br=cum_ext[cf:cf + 1, :L],
                    btot=cum_ext[cf:cf + 1, L:],
                    ir=grow[ci:ci + 1, :],
                    cc=gcol[:, ci:ci + 1] - cum_col[:, cf:cf + 1]))
        return chains

    for chains in [gate_stage(i) for i in range(nch)]:
        for ch in chains:
            r = ch["r"]
            m = jnp.concatenate([m_s[r:r + 1, :]] * (L // LANE), 1)
            ct = c_s[r]
            ch["m"] = m
            if emit:
                ch["big"] = _dot_nt(jnp.concatenate([ch["k"], ct.astype(BF16)], 0), ch["q"])
            src = ch["btot"] + ch["ir"] - ch["br"]
            m_new = jnp.maximum(m + ch["btot"], jnp.max(src, -1, keepdims=True))
            ws = jnp.exp2(src - m_new).astype(BF16)
            wc = jnp.exp2((m + ch["btot"] - m_new)[:, :M_DK])
            c_s[r] = wc * ct + _dot(ch["v_aug"] * ws, ch["k"])
            m_s[r:r + 1, :] = m_new[:, :LANE]

        if emit:
            for ch in chains:
                m, br, big = ch["m"], ch["br"], ch["big"]
                dmt = jnp.where(masks[ch["d"]], ch["cc"] + br, NEG)
                mt = jnp.maximum(m + br, jnp.max(dmt, 0, keepdims=True))
                pt = big[:L] * jnp.exp2(dmt - mt)
                num = jnp.exp2(m + br - mt) * big[L:] + _dot(ch["v_aug"], pt.astype(BF16))
                inv = 1.0 / jnp.maximum(jnp.abs(num[M_DV:M_DV + 1]), jnp.exp2(-mt))
                refs[ch["d"]][5][0, ch["sl"], ch["tok"]] = num[:M_DV] * inv

    @pl.when(j == pl.num_programs(1) - 1)
    def _():
        cout_ref[0] = c_s[...]
        mout_ref[0] = m_s[...]


def _mlstm_scan(mqk, mvt, gt, c0, m0, emit, tb):
    b, _, t = mvt.shape
    nb = t // tb
    consts = (_mlstm_cumsum_matrix(False), _mlstm_cumsum_matrix(True))
    fw = lambda w, cblk: pl.BlockSpec((1, tb, w), lambda i, j: (i, j, cblk))
    bw = lambda w, cblk: pl.BlockSpec((1, tb, w), lambda i, j: (i, nb - 1 - j, cblk))
    ftf = lambda w: pl.BlockSpec((1, w, tb), lambda i, j: (i, 0, j))
    ftb = lambda w: pl.BlockSpec((1, w, tb), lambda i, j: (i, 0, nb - 1 - j))
    const = lambda a: pl.BlockSpec(a.shape, lambda i, j: (0,) * a.ndim)
    cst = pl.BlockSpec((1, 2 * M_HEADS, M_AUG, M_DK), lambda i, j: (i, 0, 0, 0))
    mst = pl.BlockSpec((1, 2 * M_HEADS, LANE), lambda i, j: (i, 0, 0))
    in_specs = [fw(M_QK, 0), fw(M_QK, 1), ftf(M_WIDTH), ftf(N_GATES),
                bw(M_QK, 0), bw(M_QK, 1), ftb(M_WIDTH), ftb(N_GATES)]
    in_specs += [const(a) for a in consts] + [cst, mst]
    out_specs = [cst, mst]
    out_shape = [jax.ShapeDtypeStruct(c0.shape, F32), jax.ShapeDtypeStruct(m0.shape, F32)]
    if emit:
        out_specs = [ftf(M_WIDTH), ftb(M_WIDTH)] + out_specs
        out_shape = [jax.ShapeDtypeStruct((b, M_WIDTH, t), F32)] * 2 + out_shape
    return pl.pallas_call(
        functools.partial(_mlstm_kernel, emit, tb // M_CHUNK),
        grid=(b, nb),
        in_specs=in_specs,
        out_specs=out_specs,
        out_shape=out_shape,
        scratch_shapes=[pltpu.VMEM((2 * M_HEADS, M_AUG, M_DK), F32), pltpu.VMEM((2 * M_HEADS, LANE), F32)],
        compiler_params=_params("parallel", "arbitrary"),
        name="mlstm_scan" if emit else "mlstm_ctx_state",
    )(mqk, mqk, mvt, gt, mqk, mqk, mvt, gt, *consts, c0, m0)


G_LEVELS = (32, 16, 8, 4, 2, 1)


def _gla_select_matrices(reverse):
    L = G_CHUNK
    t = np.arange(L)
    u = np.arange(L)[None, :]
    tt = t[:, None]
    mats = [(u <= tt), (u > tt)]
    for h in G_LEVELS:
        start = (tt // (2 * h)) * (2 * h)
        second = tt >= start + h
        m = np.where(second, (u >= start + h) & (u <= tt), (u > tt) & (u < start + h))
        mats.append(m)
    m = np.concatenate([x.astype(np.float32) for x in mats], 0)
    if reverse:
        m = m.reshape(len(mats), L, L)[:, ::-1, ::-1].reshape(len(mats) * L, L)
    return jnp.asarray(np.concatenate([m] * G_SPLIT, 1), BF16)


def _gla_pair_masks(reverse):
    L = G_CHUNK
    s = np.arange(L)[:, None]
    t = np.arange(L)[None, :]
    masks = [s == t]
    for h in G_LEVELS:
        same = (s // (2 * h)) == (t // (2 * h))
        masks.append(same & ((t // h) % 2 == 1) & ((s // h) % 2 == 0))
    m = np.stack(masks).astype(np.float32)
    if reverse:
        m = m[:, ::-1, ::-1]
    return jnp.asarray(np.tile(m, (1, 1, G_HEADS)), F32)


def _gla_fast_matrices(reverse):
    L = G_CHUNK
    h = L // 2
    tt = np.arange(L)[:, None]
    u = np.arange(L)[None, :]
    mid = np.where(tt >= h, ((u >= h) & (u <= tt)).astype(np.float32), -((u > tt) & (u < h)).astype(np.float32))
    m = np.stack([(u <= tt).astype(np.float32), (u > tt).astype(np.float32), mid])
    if reverse:
        m = m[:, ::-1, ::-1]
    m = m.reshape(3 * L, L)
    return jnp.asarray(np.concatenate([m] * G_SPLIT, 1), BF16)


def _gla_causal_mask(reverse):
    s = np.arange(G_CHUNK)[:, None]
    t = np.arange(G_CHUNK)[None, :]
    m = (s >= t) if reverse else (s <= t)
    return jnp.asarray(np.tile(m.astype(np.float32), (1, G_HEADS)), F32)


def _gla_head_mask():
    r = np.arange(G_HEADS * G_CHUNK)[:, None] // G_CHUNK
    c = np.arange(G_QK)[None, :] // G_DK
    return jnp.asarray((r == c).astype(np.float32), BF16)


GLA_FAST_LIMIT = 40.0


def _gla_kernel(emit, nch, pipelined, qf_ref, kf_ref, vf_ref, lrf_ref, qb_ref, kb_ref, vb_ref, lrb_ref,
                lrnf_ref, lrnb_ref, lvf_ref, lvb_ref, fastf_ref, fastb_ref, mkf_ref, mkb_ref, cmf_ref, cmb_ref,
                hm_ref, gw_ref, gb_ref, s0_ref, *rest):
    if emit:
        of_ref, ob_ref, sout_ref, s_s, a3_s, lo_s = rest
    else:
        sout_ref, s_s, a3_s, lo_s = rest
        of_ref = ob_ref = None
    j = pl.program_id(1)
    L = G_CHUNK
    slot = j % 2

    def stage_decays(lr_refs, dst):
        lo = None
        for d, lr_ref in enumerate(lr_refs):
            lr = lr_ref[0][:, N_GATES + d * G_RANK:N_GATES + (d + 1) * G_RANK].astype(BF16)
            loga = _log_sigmoid(_dot(lr, gw_ref[d]) + gb_ref[d]) * (LOG2_E / G_TAU)
            pieces = _split(loga, G_SPLIT)
            a3_s[dst, d] = jnp.concatenate(
                [jnp.concatenate([p[c * L:(c + 1) * L] for p in pieces], 0) for c in range(nch)], 1)
            m = jnp.min(loga)
            lo = m if lo is None else jnp.minimum(lo, m)
        lo_s[dst] = lo * L

    @pl.when(j == 0)
    def _():
        s_s[...] = s0_ref[0]
        stage_decays((lrf_ref, lrb_ref), 0)

    hm = hm_ref[...]
    stack = lambda a: jnp.concatenate([a] * G_HEADS, 0)

    refs = ((qf_ref, kf_ref, vf_ref, mkf_ref, cmf_ref, of_ref), (qb_ref, kb_ref, vb_ref, mkb_ref, cmb_ref, ob_ref))

    def run(fast):
        chunks = []
        for d in range(2):
            q_ref, k_ref, v_ref, mk_ref, cm_ref, o_ref = refs[d]
            sel_ref = ((fastf_ref, fastb_ref) if fast else (lvf_ref, lvb_ref))[d]
            g = _dot(sel_ref[...], a3_s[slot, d])
            expo = jnp.exp2(g[:2 * L] if fast else g)
            for c in (range(nch) if d == 0 else reversed(range(nch))):
                rows = slice(c * L, (c + 1) * L)
                cols = slice(c * G_QK, (c + 1) * G_QK)
                chunks.append(dict(
                    d=d, rows=rows, cols=cols, g=g, expo=expo, k=k_ref[0, rows, :],
                    v_st=jnp.concatenate([v_ref[0, rows, hd * G_DV:(hd + 1) * G_DV] for hd in range(G_HEADS)], 0),
                    e_tot=expo[L - 1:L, cols] if d == 0 else expo[0:1, cols]))

        def e(ch, i):
            return ch["expo"][i * L:(i + 1) * L, ch["cols"]].astype(BF16)

        for ch in chunks:
            k = ch["k"]
            ch["kd_st"] = stack(k * e(ch, 1)) * hm
            if emit:
                q_ref, mk_ref, cm_ref = refs[ch["d"]][0], refs[ch["d"]][3], refs[ch["d"]][4]
                qs = stack(q_ref[0, ch["rows"], :]) * hm
                ch["qs"] = qs
                if fast:
                    gm = ch["g"][2 * L:3 * L, ch["cols"]]
                    att_t = _dot_nt(k * jnp.exp2(-gm).astype(BF16), qs * stack(jnp.exp2(gm).astype(BF16)))
                    att_t = jnp.where(cm_ref[...] != 0.0, att_t, 0.0)
                else:
                    att_t = mk_ref[0] * _dot_nt(k, qs)
                    for lv in range(len(G_LEVELS)):
                        ev = e(ch, 2 + lv)
                        att_t = att_t + mk_ref[1 + lv] * _dot_nt(k * ev, qs * stack(ev))
                ch["att_t"] = att_t

        if pipelined:
            stage_decays((lrnf_ref, lrnb_ref), 1 - slot)

        for ch in chunks:
            if emit:
                att_st = stack(ch["att_t"].astype(BF16)) * hm
                ch["res"] = _dot_tn(ch["v_st"], jnp.concatenate([att_st, ch["kd_st"]], 1))
            else:
                ch["res"] = _dot_tn(ch["v_st"], ch["kd_st"])

        order = [chunks[dd * nch + i] for i in range(nch) for dd in range(2)]
        for ch in order:
            d = ch["d"]
            st = s_s[d]
            res = ch["res"]
            if emit:
                inter_t = _dot_nt(st.astype(BF16), ch["qs"] * stack(e(ch, 0)))
                o = jnp.transpose(inter_t + res[:, :G_HEADS * L])
                o_ref = refs[d][5]
                for hd in range(G_HEADS):
                    o_ref[0, ch["rows"], hd * G_DV:(hd + 1) * G_DV] = o[hd * L:(hd + 1) * L]
                res = res[:, G_HEADS * L:]
            s_s[d] = ch["e_tot"] * st + res

    if emit:
        mild = lo_s[slot] >= -GLA_FAST_LIMIT
        pl.when(mild)(lambda: run(True))
        pl.when(jnp.logical_not(mild))(lambda: run(False))
    else:
        run(True)

    @pl.when(j == pl.num_programs(1) - 1)
    def _():
        sout_ref[0] = s_s[...]


def _gla_scan(gq, gk, gv, small, g_gate_w, g_gate_b, s0, emit, tb):
    b, t, _ = gv.shape
    nb = t // tb
    consts = (_gla_select_matrices(False), _gla_select_matrices(True),
              _gla_fast_matrices(False), _gla_fast_matrices(True),
              _gla_pair_masks(False), _gla_pair_masks(True),
              _gla_causal_mask(False), _gla_causal_mask(True), _gla_head_mask(), g_gate_w, g_gate_b)
    fw = lambda w: pl.BlockSpec((1, tb, w), lambda i, j: (i, j, 0))
    bw = lambda w: pl.BlockSpec((1, tb, w), lambda i, j: (i, nb - 1 - j, 0))
    const = lambda a: pl.BlockSpec(a.shape, lambda i, j: (0,) * a.ndim)
    sst = pl.BlockSpec((1, 2, G_DV, G_QK), lambda i, j: (i, 0, 0, 0))
    fw_next = pl.BlockSpec((1, tb, SMALL_COLS), lambda i, j: (i, jnp.minimum(j + 1, nb - 1), 0))
    bw_next = pl.BlockSpec((1, tb, SMALL_COLS), lambda i, j: (i, jnp.maximum(nb - 2 - j, 0), 0))
    in_specs = [fw(G_QK), fw(G_QK), fw(G_WIDTH), fw(SMALL_COLS),
                bw(G_QK), bw(G_QK), bw(G_WIDTH), bw(SMALL_COLS), fw_next, bw_next]
    in_specs += [const(a) for a in consts] + [sst]
    out_specs = [sst]
    out_shape = [jax.ShapeDtypeStruct(s0.shape, F32)]
    if emit:
        out_specs = [fw(G_WIDTH), bw(G_WIDTH)] + out_specs
        out_shape = [jax.ShapeDtypeStruct((b, t, G_WIDTH), F32)] * 2 + out_shape
    nch = tb // G_CHUNK
    return pl.pallas_call(
        functools.partial(_gla_kernel, emit, nch, nb > 1),
        grid=(b, nb),
        in_specs=in_specs,
        out_specs=out_specs,
        out_shape=out_shape,
        scratch_shapes=[pltpu.VMEM((2, G_DV, G_QK), F32),
                        pltpu.VMEM((2, 2, G_SPLIT * G_CHUNK, nch * G_QK), BF16),
                        pltpu.SMEM((2,), F32)],
        compiler_params=_params("parallel", "arbitrary"),
        name="gla_scan" if emit else "gla_ctx_state",
    )(gq, gk, gv, small, gq, gk, gv, small, small, small, *consts, s0)


FFN_CHUNK = 256


def _head_norm(h, nheads, width):
    parts = []
    for hd in range(nheads):
        blk = h[:, hd * width:(hd + 1) * width]
        parts.append(blk * lax.rsqrt(jnp.mean(blk * blk, -1, keepdims=True) + EPS))
    return jnp.concatenate(parts, 1)


def _out_kernel(d_ff, x_ref, hf_ref, hb_ref, of_ref, ob_ref, mo_ref, gr_ref, mod_ref,
                mng_ref, gng_ref, gffn_ref, gfin_ref, wout_ref, wgu_ref, wdown_ref, o_ref):
    x = x_ref[0]
    mod = mod_ref[0]
    hm_t = hf_ref[0] + hb_ref[0]
    parts = []
    for hd in range(M_HEADS):
        blk = hm_t[hd * M_DV:(hd + 1) * M_DV]
        parts.append(blk * lax.rsqrt(jnp.mean(blk * blk, 0, keepdims=True) + EPS))
    hm_t = jnp.concatenate(parts, 0) * mng_ref[...] * _sigmoid(mo_ref[0].astype(F32))
    gr = gr_ref[0].astype(F32)
    hg = _head_norm(of_ref[0] + ob_ref[0], G_HEADS, G_DV) * gng_ref[...] * (gr * _sigmoid(gr))
    mixed = _dot_tn(hm_t.astype(BF16), wout_ref[:M_WIDTH, :]) + _dot(hg.astype(BF16), wout_ref[M_WIDTH:, :])
    x1 = x + mod[0:1] * mixed
    h2 = x1 * lax.rsqrt(jnp.mean(x1 * x1, -1, keepdims=True) + EPS) * gffn_ref[...]
    h2 = (h2 * (1.0 + mod[2:3]) + mod[1:2]).astype(BF16)
    def gated(c0):
        a = _dot(h2, wgu_ref[:, c0:c0 + FFN_CHUNK])
        g = _dot(h2, wgu_ref[:, d_ff + c0:d_ff + c0 + FFN_CHUNK])
        return (a * _sigmoid(a) * g).astype(BF16)

    starts = list(range(0, d_ff, FFN_CHUNK))
    act = gated(starts[0])
    acc = None
    for i, c0 in enumerate(starts):
        nxt = gated(starts[i + 1]) if i + 1 < len(starts) else None
        part = _dot(act, wdown_ref[c0:c0 + FFN_CHUNK, :])
        acc = part if acc is None else acc + part
        act = nxt
    x2 = x1 + mod[3:4] * acc
    o_ref[0] = x2 * lax.rsqrt(jnp.mean(x2 * x2, -1, keepdims=True) + EPS) * gfin_ref[...]


def _out_block(x, hf, hb, of, ob, mo, gr, mod4, mng, gng, gffn, gfin, wout, wgu, wdown, tm):
    b, t, d = x.shape
    d_ff = wdown.shape[0]
    tok = lambda w: pl.BlockSpec((1, tm, w), lambda i, j: (i, j, 0))
    feat = lambda w: pl.BlockSpec((1, w, tm), lambda i, j: (i, 0, j))
    const = lambda a: pl.BlockSpec(a.shape, lambda i, j: (0,) * a.ndim, pipeline_mode=pl.Buffered(1))
    return pl.pallas_call(
        functools.partial(_out_kernel, d_ff),
        grid=(b, t // tm),
        in_specs=[tok(d), feat(M_WIDTH), feat(M_WIDTH), tok(G_WIDTH), tok(G_WIDTH), feat(M_WIDTH), tok(G_WIDTH),
                  pl.BlockSpec((1, 4, d), lambda i, j: (i, 0, 0)),
                  const(mng), const(gng), const(gffn), const(gfin), const(wout), const(wgu), const(wdown)],
        out_specs=tok(d),
        out_shape=jax.ShapeDtypeStruct((b, t, d), F32),
        compiler_params=_params("parallel", "arbitrary"),
        name="out_ffn",
    )(x, hf, hb, of, ob, mo, gr, mod4, mng, gng, gffn, gfin, wout, wgu, wdown)


def _largest_tile(t, cap):
    tm = min(t, cap)
    while t % tm:
        tm //= 2
    return tm


def kernel(x, c, ctx, c_ctx, w_ada, b_ada, g_mix, w_in, conv_w, m_gate_b, m_norm_g, g_gate_w, g_gate_b,
           g_norm_g, w_out, g_ffn, w_gu, w_down, g_final):
    b, t, d = x.shape
    tc = ctx.shape[1]
    assert w_ada.shape[0] == 1, "single layer block"
    assert t % (GRID_W * 2) == 0 and tc % M_CHUNK == 0

    pad = (-(b + 1)) % 8
    c_all = jnp.concatenate([c, c_ctx[None, :], jnp.zeros((pad, d), F32)], 0)
    mod = _modulation(c_all, w_ada[0], b_ada[0])
    mod_x = mod[:b].reshape(b, 6, d)
    mod_c = jnp.broadcast_to(mod[b].reshape(1, 6, d), (b, 6, d))

    w = w_in[0]
    sizes = (2 * M_QK, M_WIDTH, M_WIDTH, N_GATES, G_QK, G_QK, G_WIDTH, G_WIDTH, 2 * G_RANK)
    offs = np.concatenate([[0], np.cumsum(sizes)])
    seg = lambda i: w[:, offs[i]:offs[i + 1]]
    wtok = jnp.concatenate([seg(0), seg(4), seg(5), seg(6), seg(7)], 1).astype(BF16)
    wfeat = jnp.transpose(jnp.concatenate([seg(1), seg(2)], 1)).astype(BF16)
    n_small = N_GATES + 2 * G_RANK
    wsmall = jnp.concatenate([seg(3), seg(8), jnp.zeros((d, SMALL_COLS - n_small), F32)], 1).astype(BF16)
    wgt = jnp.transpose(seg(3)).astype(BF16)
    bsm = jnp.concatenate([m_gate_b[0], jnp.zeros((SMALL_COLS - N_GATES,), F32)]).reshape(1, SMALL_COLS)
    bgt = m_gate_b[0].reshape(N_GATES, 1)
    g_mix2 = g_mix[0].reshape(1, d)
    conv_w9 = conv_w[0].reshape(9, 2 * M_QK)
    kscale = jnp.concatenate([jnp.ones((M_QK,), F32), jnp.full((M_QK,), M_DK ** -0.5, F32)]).reshape(1, 2 * M_QK)
    ggw = g_gate_w[0].astype(BF16)
    ggb = g_gate_b[0].reshape(2, 1, G_QK)

    feat_consts = (g_mix2, wtok, wfeat, wsmall, wgt, bsm, bgt, conv_w9, kscale)

    def features(inp, m6, cols, tm):
        mqk, gq, gk, gv, gr, mvt, mot, small, gt = _features(inp, m6[:, 0:1], m6[:, 1:2], feat_consts, cols, tm)
        return dict(mqk=mqk, gq=gq, gk=gk, gv=gv, gr=gr, mvt=mvt, mot=mot, small=small, gt=gt)

    fc = features(ctx, mod_c, tc, tc)
    c0 = jnp.zeros((b, 2 * M_HEADS, M_AUG, M_DK), F32)
    m0 = jnp.full((b, 2 * M_HEADS, LANE), NEG, F32)
    c_ctx_state, m_ctx_state = _mlstm_scan(fc["mqk"], fc["mvt"], fc["gt"], c0, m0, False,
                                           _largest_tile(tc, M_CHUNK))
    s0 = jnp.zeros((b, 2, G_DV, G_QK), F32)
    (s_ctx_state,) = _gla_scan(fc["gq"], fc["gk"], fc["gv"], fc["small"], ggw, ggb, s0, False,
                               _largest_tile(tc, 256))

    fx = features(x, mod_x, GRID_W, _largest_tile(t, 1024))
    hf_t, hb_t, _, _ = _mlstm_scan(fx["mqk"], fx["mvt"], fx["gt"], c_ctx_state, m_ctx_state, True,
                                   _largest_tile(t, 1024))
    of, ob, _ = _gla_scan(fx["gq"], fx["gk"], fx["gv"], fx["small"], ggw, ggb, s_ctx_state, True,
                          _largest_tile(t, 512))

    tm = _largest_tile(t, 512)
    mod4 =jnp.stack([mod_x[:, 2], mod_x[:, 3], mod_x[:, 4], mod_x[:, 5]], 1)
    mng = jnp.broadcast_to(m_norm_g[0].reshape(M_WIDTH, 1), (M_WIDTH, tm))
    return _out_block(x, hf_t, hb_t, of, ob, fx["mot"], fx["gr"], mod4,
                      mng, g_norm_g[0].reshape(1, -1),
                      g_ffn[0].reshape(1, d), g_final.reshape(1, d),
                      w_out[0].astype(BF16), w_gu[0].astype(BF16), w_down[0].astype(BF16), tm)
```

```python
import functools

import numpy as np
import jax
import jax.numpy as jnp
from jax import lax
from jax.experimental import pallas as pl
from jax.experimental.pallas import tpu as pltpu

F32 = jnp.float32
BF16 = jnp.bfloat16

GRID_W = 64
M_HEADS = 4
M_DK = 128
M_DV = 128
M_QK = M_HEADS * M_DK
M_WIDTH = M_HEADS * M_DV
M_CHUNK = 256
G_HEADS = 4
G_DK = 64
G_DV = 128
G_QK = G_HEADS * G_DK
G_WIDTH = G_HEADS * G_DV
G_RANK = 16
G_TAU = 16.0
G_CHUNK = 64
EPS = 1e-6
NEG = -1e30
LOG2_E = 1.4426950408889634

LANE = 128
VMEM_LIMIT = 56 * 1024 * 1024


def _params(*sem):
    return pltpu.CompilerParams(dimension_semantics=sem, vmem_limit_bytes=VMEM_LIMIT)


def _sigmoid(x):
    return 1.0 / (1.0 + jnp.exp(-x))


def _log_sigmoid(x):
    return jnp.minimum(x, 0.0) - jnp.log(1.0 + jnp.exp(-jnp.abs(x)))


def _split(x, n):
    pieces = []
    for _ in range(n - 1):
        p = x.astype(BF16)
        pieces.append(p)
        x = x - p.astype(F32)
    pieces.append(x.astype(BF16))
    return pieces


def _split3(x):
    return _split(x, 3)


G_SPLIT = 2


def _dot(a, b):
    return jnp.dot(a, b, preferred_element_type=F32)


def _dot_nt(a, b):
    return lax.dot_general(a, b, (((1,), (1,)), ((), ())), preferred_element_type=F32)


def _dot_tn(a, b):
    return lax.dot_general(a, b, (((0,), (0,)), ((), ())), preferred_element_type=F32)


def _mod_kernel(c_ref, w_ref, b_ref, o_ref):
    cs = c_ref[...]
    a = (cs * _sigmoid(cs)).astype(BF16)
    o_ref[...] = _dot(a, w_ref[...].astype(BF16)) + b_ref[...]


def _modulation(c_all, w_ada, b_ada):
    rows, d = c_all.shape
    n = w_ada.shape[1]
    bn = n // 6
    return pl.pallas_call(
        _mod_kernel,
        grid=(n // bn,),
        in_specs=[pl.BlockSpec((rows, d), lambda j: (0, 0)),
                  pl.BlockSpec((d, bn), lambda j: (0, j)),
                  pl.BlockSpec((1, bn), lambda j: (0, j))],
        out_specs=pl.BlockSpec((rows, bn), lambda j: (0, j)),
        out_shape=jax.ShapeDtypeStruct((rows, n), F32),
        compiler_params=_params("arbitrary"),
        name="adaln_mod",
    )(c_all, w_ada, b_ada.reshape(1, n))


_TOK_SIZES = (("mqk", 2 * M_QK), ("gq", G_QK), ("gk", G_QK), ("gv", G_WIDTH), ("gr", G_WIDTH))
_FEAT_SIZES = (("mv", M_WIDTH), ("mo", M_WIDTH))
N_GATES = 4 * M_HEADS
SMALL_COLS = LANE


CONV_CB = 256


def _conv_silu(ext, w9, cols, nrow):
    tot = (nrow + 2) * cols
    col = lax.broadcasted_iota(jnp.int32, ext.shape, 0) % cols
    left = jnp.where(col == 0, 0.0, pltpu.roll(ext, 1, 0))
    right = jnp.where(col == cols - 1, 0.0, pltpu.roll(ext, tot - 1, 0))
    srcs = (left, ext, right)
    acc = None
    for di in range(3):
        for dj in range(3):
            term = srcs[dj][di * cols:(di + nrow) * cols] * w9[di * 3 + dj:di * 3 + dj + 1]
            acc = term if acc is None else acc + term
    return acc * _sigmoid(acc)


def _feat_kernel(cols, x_ref, xn_ref, sh_ref, sc_ref, g_ref, wtok_ref, wfeat_ref, wsmall_ref, wgt_ref,
                 bsm_ref, bgt_ref, cw_ref, ksc_ref,
                 mqk_ref, gq_ref, gk_ref, gv_ref, gr_ref, mvt_ref, mot_ref, small_ref, gt_ref, prev_s):
    j = pl.program_id(1)
    last = pl.num_programs(1) - 1
    tm = x_ref.shape[1]

    def norm_mod(x):
        h = x * lax.rsqrt(jnp.mean(x * x, -1, keepdims=True) + EPS) * g_ref[...]
        return (h * (1.0 + sc_ref[0]) + sh_ref[0]).astype(BF16)

    hb = norm_mod(x_ref[0])
    hext = jnp.concatenate([hb, norm_mod(xn_ref[0])], 0)

    @pl.when(j == 0)
    def _():
        prev_s[...] = jnp.zeros_like(prev_s)

    outs = dict(gq=gq_ref, gk=gk_ref, gv=gv_ref, gr=gr_ref, mv=mvt_ref, mo=mot_ref)
    tasks = []

    def tok_task(name, off, c0):
        def run():
            z = _dot(hb, wtok_ref[:, off + c0:off + c0 + CONV_CB])
            if name == "gq":
                z = z * (G_DK ** -0.5)
            outs[name][0, :, c0:c0 + CONV_CB] = z.astype(BF16)
        return run

    def feat_task(name, off, c0):
        def run():
            z = _dot_nt(wfeat_ref[off + c0:off + c0 + CONV_CB, :], hb)
            outs[name][0, c0:c0 + CONV_CB, :] = z.astype(BF16)
        return run

    off = 2 * M_QK
    for name, size in _TOK_SIZES[1:]:
        tasks += [tok_task(name, off, c0) for c0 in range(0, size, CONV_CB)]
        off += size
    off = 0
    for name, size in _FEAT_SIZES:
        tasks += [feat_task(name, off, c0) for c0 in range(0, size, CONV_CB)]
        off += size

    nblk = 2 * M_QK // CONV_CB
    for i in range(nblk):
        cs = slice(i * CONV_CB, (i + 1) * CONV_CB)
        z = _dot(hext, wtok_ref[:, cs])
        for task in tasks[i * len(tasks) // nblk:(i + 1) * len(tasks) // nblk]:
            task()
        below = jnp.where(j < last, z[tm:], 0.0)
        ext = jnp.concatenate([prev_s[:, cs], z[:tm], below], 0)
        prev_s[:, cs] = z[tm - cols:tm]
        mqk_ref[0, :, cs] = (_conv_silu(ext, cw_ref[:, cs], cols, tm // cols) * ksc_ref[:, cs]).astype(BF16)
    small_ref[0] = _dot(hb, wsmall_ref[...]) + bsm_ref[...]
    gt_ref[0] = _dot_nt(wgt_ref[...], hb) + bgt_ref[...]


def _features(x, shift, scale, consts, cols, tm):
    b, t, d = x.shape
    nt = t // tm
    per = tm // cols
    tok = lambda w: pl.BlockSpec((1, tm, w), lambda i, j: (i, j, 0))
    feat = lambda w: pl.BlockSpec((1, w, tm), lambda i, j: (i, 0, j))
    below = pl.BlockSpec((1, cols, d), lambda i, j: (i, jnp.minimum((j + 1) * per, t // cols - 1), 0))
    const = lambda a: pl.BlockSpec(a.shape, lambda i, j: (0,) * a.ndim)
    vec = pl.BlockSpec((1, 1, d), lambda i, j: (i, 0, 0))
    out_shape = [jax.ShapeDtypeStruct((b, t, s), BF16) for _, s in _TOK_SIZES]
    out_specs = [tok(s) for _, s in _TOK_SIZES]
    out_shape += [jax.ShapeDtypeStruct((b, s, t), BF16) for _, s in _FEAT_SIZES]
    out_specs += [feat(s) for _, s in _FEAT_SIZES]
    out_shape += [jax.ShapeDtypeStruct((b, t, SMALL_COLS), F32), jax.ShapeDtypeStruct((b, N_GATES, t), F32)]
    out_specs += [tok(SMALL_COLS), feat(N_GATES)]
    return pl.pallas_call(
        functools.partial(_feat_kernel, cols),
        grid=(b, nt),
        in_specs=[tok(d), below, vec, vec] + [const(a) for a in consts],
        out_specs=out_specs,
        out_shape=out_shape,
        scratch_shapes=[pltpu.VMEM((cols, 2 * M_QK), F32)],
        compiler_params=_params("parallel", "arbitrary"),
        name="features",
    )(x, x, shift, scale, *consts)


M_AUG = M_DV + 16


def _mlstm_cumsum_matrix(reverse):
    L = M_CHUNK
    low = np.tril(np.ones((L, L), np.float32))
    tri = low.T if reverse else low
    return jnp.asarray(np.tile(np.concatenate([tri.T, np.ones((L, L), np.float32)], 1), (3, 1)), BF16)


def _mlstm_kernel(emit, nch, qf_ref, kf_ref, vf_ref, grf_ref, qb_ref, kb_ref, vb_ref, grb_ref,
                  rowf_ref, rowb_ref, c0_ref, m0_ref, *rest):
    if emit:
        hf_ref, hb_ref, cout_ref, mout_ref, c_s, m_s = rest
    else:
        cout_ref, mout_ref, c_s, m_s = rest
        hf_ref = hb_ref = None
    j = pl.program_id(1)
    L = M_CHUNK

    @pl.when(j == 0)
    def _():
        c_s[...] = c0_ref[0]
        m_s[...] = m0_ref[0]

    sid = lax.broadcasted_iota(jnp.int32, (L, L), 0)
    tid = lax.broadcasted_iota(jnp.int32, (L, L), 1)
    ones_rows = jnp.ones((M_AUG - M_DV, L), BF16)
    lane = lax.broadcasted_iota(jnp.int32, (M_HEADS, L), 1)

    refs = ((qf_ref, kf_ref, vf_ref, grf_ref, rowf_ref, hf_ref),
            (qb_ref, kb_ref, vb_ref, grb_ref, rowb_ref, hb_ref))
    masks = (sid <= tid, sid >= tid)

    def gate_stage(i):
        chains = []
        for d in range(2):
            q_ref, k_ref, v_ref, gr_ref, row_ref, h_ref = refs[d]
            c = i if d == 0 else nch - 1 - i
            tok = slice(c * L, (c + 1) * L)
            grow = gr_ref[0, :, tok]
            cum_ext = _dot(jnp.concatenate(_split3(_log_sigmoid(grow) * LOG2_E), 1), row_ref[...])
            grow = grow * LOG2_E
            i0 = 2 * M_HEADS * d
            cc_rows = grow[i0:i0 + M_HEADS] - cum_ext[i0 + M_HEADS:i0 + 2 * M_HEADS, :L]
            cmax = cc_rows
            step = 1
            while step < L:
                if d == 0:
                    moved = jnp.where(lane >= step, pltpu.roll(cmax, step, 1), NEG)
                else:
                    moved = jnp.where(lane < L - step, pltpu.roll(cmax, L - step, 1), NEG)
                cmax = jnp.maximum(cmax, moved)
                step *= 2
            cc_cols = jnp.transpose(jnp.concatenate([cc_rows, cc_rows], 0))
            for hd in range(M_HEADS):
                cf = i0 + M_HEADS + hd
                sl = slice(hd * M_DK, (hd + 1) * M_DK)
                chains.append(dict(
                    d=d, r=d * M_HEADS + hd, sl=sl, tok=tok,
                    q=q_ref[0, tok, sl], k=k_ref[0, tok, sl],
                    v_aug=jnp.concatenate([v_ref[0, sl, tok], ones_rows], 0),
                    br=cum_ext[cf:cf + 1, :L],
                    btot=cum_ext[cf:cf + 1, L:],
                    ir=grow[i0 + hd:i0 + hd + 1, :],
                    cmax=cmax[hd:hd + 1, :],
                    cc=cc_cols[:, hd:hd + 1]))
        return chains

    for chains in [gate_stage(i) for i in range(nch)]:
        for ch in chains:
            r = ch["r"]
            m = jnp.concatenate([m_s[r:r + 1, :]] * (L // LANE), 1)
            ct = c_s[r]
            ch["m"] = m
            if emit:
                ch["big"] = _dot_nt(jnp.concatenate([ch["k"], ct.astype(BF16)], 0), ch["q"])
            src = ch["btot"] + ch["ir"] - ch["br"]
            m_new = jnp.maximum(m + ch["btot"], jnp.max(src, -1, keepdims=True))
            ws = jnp.exp2(src - m_new).astype(BF16)
            wc = jnp.exp2((m + ch["btot"] - m_new)[:, :M_DK])
            c_s[r] = wc * ct + _dot(ch["v_aug"] * ws, ch["k"])
            m_s[r:r + 1, :] = m_new[:, :LANE]

        if emit:
            for ch in chains:
                m, br, big = ch["m"], ch["br"], ch["big"]
                mt = br + jnp.maximum(m, ch["cmax"])
                pt = jnp.where(masks[ch["d"]], big[:L] * jnp.exp2(ch["cc"] + (br - mt)), 0.0)
                num = jnp.exp2(m + br - mt) * big[L:] + _dot(ch["v_aug"], pt.astype(BF16))
                inv = 1.0 / jnp.maximum(jnp.abs(num[M_DV:M_DV + 1]), jnp.exp2(-mt))
                refs[ch["d"]][5][0, ch["sl"], ch["tok"]] = num[:M_DV] * inv

    @pl.when(j == pl.num_programs(1) - 1)
    def _():
        cout_ref[0] = c_s[...]
        mout_ref[0] = m_s[...]


def _mlstm_scan(mqk, mvt, gt, c0, m0, emit, tb):
    b, _, t = mvt.shape
    nb = t // tb
    consts = (_mlstm_cumsum_matrix(False), _mlstm_cumsum_matrix(True))
    fw = lambda w, cblk: pl.BlockSpec((1, tb, w), lambda i, j: (i, j, cblk))
    bw = lambda w, cblk: pl.BlockSpec((1, tb, w), lambda i, j: (i, nb - 1 - j, cblk))
    ftf = lambda w: pl.BlockSpec((1, w, tb), lambda i, j: (i, 0, j))
    ftb = lambda w: pl.BlockSpec((1, w, tb), lambda i, j: (i, 0, nb - 1 - j))
    const = lambda a: pl.BlockSpec(a.shape, lambda i, j: (0,) * a.ndim)
    cst = pl.BlockSpec((1, 2 * M_HEADS, M_AUG, M_DK), lambda i, j: (i, 0, 0, 0))
    mst = pl.BlockSpec((1, 2 * M_HEADS, LANE), lambda i, j: (i, 0, 0))
    in_specs = [fw(M_QK, 0), fw(M_QK, 1), ftf(M_WIDTH), ftf(N_GATES),
                bw(M_QK, 0), bw(M_QK, 1), ftb(M_WIDTH), ftb(N_GATES)]
    in_specs += [const(a) for a in consts] + [cst, mst]
    out_specs = [cst, mst]
    out_shape = [jax.ShapeDtypeStruct(c0.shape, F32), jax.ShapeDtypeStruct(m0.shape, F32)]
    if emit:
        out_specs = [ftf(M_WIDTH), ftb(M_WIDTH)] + out_specs
        out_shape = [jax.ShapeDtypeStruct((b, M_WIDTH, t), F32)] * 2 + out_shape
    return pl.pallas_call(
        functools.partial(_mlstm_kernel, emit, tb // M_CHUNK),
        grid=(b, nb),
        in_specs=in_specs,
        out_specs=out_specs,
        out_shape=out_shape,
        scratch_shapes=[pltpu.VMEM((2 * M_HEADS, M_AUG, M_DK), F32), pltpu.VMEM((2 * M_HEADS, LANE), F32)],
        compiler_params=_params("parallel", "arbitrary"),
        name="mlstm_scan" if emit else "mlstm_ctx_state",
    )(mqk, mqk, mvt, gt, mqk, mqk, mvt, gt, *consts, c0, m0)


G_LEVELS = (32, 16, 8, 4, 2, 1)


def _gla_select_matrices(reverse):
    L = G_CHUNK
    t = np.arange(L)
    u = np.arange(L)[None, :]
    tt = t[:, None]
    mats = [(u <= tt), (u > tt)]
    for h in G_LEVELS:
        start = (tt // (2 * h)) * (2 * h)
        second = tt >= start + h
        m = np.where(second, (u >= start + h) & (u <= tt), (u > tt) & (u < start + h))
        mats.append(m)
    m = np.concatenate([x.astype(np.float32) for x in mats], 0)
    if reverse:
        m = m.reshape(len(mats), L, L)[:, ::-1, ::-1].reshape(len(mats) * L, L)
    return jnp.asarray(np.concatenate([m] * G_SPLIT, 1), BF16)


def _gla_pair_masks(reverse):
    L = G_CHUNK
    s = np.arange(L)[:, None]
    t = np.arange(L)[None, :]
    masks = [s == t]
    for h in G_LEVELS:
        same = (s // (2 * h)) == (t // (2 * h))
        masks.append(same & ((t // h) % 2 == 1) & ((s // h) % 2 == 0))
    m = np.stack(masks).astype(np.float32)
    if reverse:
        m = m[:, ::-1, ::-1]
    return jnp.asarray(np.tile(m, (1, 1, G_HEADS)), F32)


def _gla_fast_matrices(reverse):
    L = G_CHUNK
    h = L // 2
    tt = np.arange(L)[:, None]
    u = np.arange(L)[None, :]
    mid = np.where(tt >= h, ((u >= h) & (u <= tt)).astype(np.float32), -((u > tt) & (u < h)).astype(np.float32))
    m = np.stack([(u <= tt).astype(np.float32), (u > tt).astype(np.float32), mid])
    if reverse:
        m = m[:, ::-1, ::-1]
    m = m.reshape(3 * L, L)
    return jnp.asarray(np.concatenate([m] * G_SPLIT, 1), BF16)


def _gla_causal_mask(reverse):
    s = np.arange(G_CHUNK)[:, None]
    t = np.arange(G_CHUNK)[None, :]
    m = (s >= t) if reverse else (s <= t)
    return jnp.asarray(np.tile(m.astype(np.float32), (1, G_HEADS)), F32)


def _gla_head_mask():
    r = np.arange(G_HEADS * G_CHUNK)[:, None] // G_CHUNK
    c = np.arange(G_QK)[None, :] // G_DK
    return jnp.asarray((r == c).astype(np.float32), BF16)


GLA_FAST_LIMIT = 40.0


def _gla_kernel(emit, nch, pipelined, qf_ref, kf_ref, vf_ref, lrf_ref, qb_ref, kb_ref, vb_ref, lrb_ref,
                lrnf_ref, lrnb_ref, lvf_ref, lvb_ref, fastf_ref, fastb_ref, mkf_ref, mkb_ref, cmf_ref, cmb_ref,
                hm_ref, gw_ref, gb_ref, s0_ref, *rest):
    if emit:
        of_ref, ob_ref, sout_ref, s_s, a3_s, lo_s = rest
    else:
        sout_ref, s_s, a3_s, lo_s = rest
        of_ref = ob_ref = None
    j = pl.program_id(1)
    L = G_CHUNK
    slot = j % 2

    def stage_decays(lr_refs, dst):
        lo = None
        for d, lr_ref in enumerate(lr_refs):
            lr = lr_ref[0][:, N_GATES + d * G_RANK:N_GATES + (d + 1) * G_RANK].astype(BF16)
            loga = _log_sigmoid(_dot(lr, gw_ref[d]) + gb_ref[d]) * (LOG2_E / G_TAU)
            pieces = _split(loga, G_SPLIT)
            a3_s[dst, d] = jnp.concatenate(
                [jnp.concatenate([p[c * L:(c + 1) * L] for p in pieces], 0) for c in range(nch)], 1)
            m = jnp.min(loga)
            lo = m if lo is None else jnp.minimum(lo, m)
        lo_s[dst] = lo * L

    @pl.when(j == 0)
    def _():
        s_s[...] = s0_ref[0]
        stage_decays((lrf_ref, lrb_ref), 0)

    hm = hm_ref[...]
    stack = lambda a: jnp.concatenate([a] * G_HEADS, 0)

    refs = ((qf_ref, kf_ref, vf_ref, mkf_ref, cmf_ref, of_ref), (qb_ref, kb_ref, vb_ref, mkb_ref, cmb_ref, ob_ref))

    def run(fast):
        chunks = []
        for d in range(2):
            q_ref, k_ref, v_ref, mk_ref, cm_ref, o_ref = refs[d]
            sel_ref = ((fastf_ref, fastb_ref) if fast else (lvf_ref, lvb_ref))[d]
            g = _dot(sel_ref[...], a3_s[slot, d])
            expo = jnp.exp2(g[:2 * L] if fast else g)
            for c in (range(nch) if d == 0 else reversed(range(nch))):
                rows = slice(c * L, (c + 1) * L)
                cols = slice(c * G_QK, (c + 1) * G_QK)
                chunks.append(dict(
                    d=d, rows=rows, cols=cols, g=g, expo=expo, k=k_ref[0, rows, :],
                    v_st=jnp.concatenate([v_ref[0, rows, hd * G_DV:(hd + 1) * G_DV] for hd in range(G_HEADS)], 0),
                    e_tot=expo[L - 1:L, cols] if d == 0 else expo[0:1, cols]))

        def e(ch, i):
            return ch["expo"][i * L:(i + 1) * L, ch["cols"]].astype(BF16)

        for ch in chunks:
            k = ch["k"]
            ch["kd_st"] = stack(k * e(ch, 1)) * hm
            if emit:
                q_ref, mk_ref, cm_ref = refs[ch["d"]][0], refs[ch["d"]][3], refs[ch["d"]][4]
                qs = stack(q_ref[0, ch["rows"], :]) * hm
                ch["qs"] = qs
                if fast:
                    gm = ch["g"][2 * L:3 * L, ch["cols"]]
                    att_t = _dot_nt(k * jnp.exp2(-gm).astype(BF16), qs * stack(jnp.exp2(gm).astype(BF16)))
                    att_t = jnp.where(cm_ref[...] != 0.0, att_t, 0.0)
                else:
                    att_t = mk_ref[0] * _dot_nt(k, qs)
                    for lv in range(len(G_LEVELS)):
                        ev = e(ch, 2 + lv)
                        att_t = att_t + mk_ref[1 + lv] * _dot_nt(k * ev, qs * stack(ev))
                ch["att_t"] = att_t

        if pipelined:
            stage_decays((lrnf_ref, lrnb_ref), 1 - slot)

        for ch in chunks:
            if emit:
                att_st = stack(ch["att_t"].astype(BF16)) * hm
                ch["res"] = _dot_tn(ch["v_st"], jnp.concatenate([att_st, ch["kd_st"]], 1))
            else:
                ch["res"] = _dot_tn(ch["v_st"], ch["kd_st"])

        order = [chunks[dd * nch + i] for i in range(nch) for dd in range(2)]
        for ch in order:
            d = ch["d"]
            st = s_s[d]
            res = ch["res"]
            if emit:
                inter_t = _dot_nt(st.astype(BF16), ch["qs"] * stack(e(ch, 0)))
                o = jnp.transpose(inter_t + res[:, :G_HEADS * L])
                o_ref = refs[d][5]
                for hd in range(G_HEADS):
                    o_ref[0, ch["rows"], hd * G_DV:(hd + 1) * G_DV] = o[hd * L:(hd + 1) * L]
                res = res[:, G_HEADS * L:]
            s_s[d] = ch["e_tot"] * st + res

    if emit:
        mild = lo_s[slot] >= -GLA_FAST_LIMIT
        pl.when(mild)(lambda: run(True))
        pl.when(jnp.logical_not(mild))(lambda: run(False))
    else:
        run(True)

    @pl.when(j == pl.num_programs(1) - 1)
    def _():
        sout_ref[0] = s_s[...]


def _gla_scan(gq, gk, gv, small, g_gate_w, g_gate_b, s0, emit, tb):
    b, t, _ = gv.shape
    nb = t // tb
    consts = (_gla_select_matrices(False), _gla_select_matrices(True),
              _gla_fast_matrices(False), _gla_fast_matrices(True),
              _gla_pair_masks(False), _gla_pair_masks(True),
              _gla_causal_mask(False), _gla_causal_mask(True), _gla_head_mask(), g_gate_w, g_gate_b)
    fw = lambda w: pl.BlockSpec((1, tb, w), lambda i, j: (i, j, 0))
    bw = lambda w: pl.BlockSpec((1, tb, w), lambda i, j: (i, nb - 1 - j, 0))
    const = lambda a: pl.BlockSpec(a.shape, lambda i, j: (0,) * a.ndim)
    sst = pl.BlockSpec((1, 2, G_DV, G_QK), lambda i, j: (i, 0, 0, 0))
    fw_next = pl.BlockSpec((1, tb, SMALL_COLS), lambda i, j: (i, jnp.minimum(j + 1, nb - 1), 0))
    bw_next = pl.BlockSpec((1, tb, SMALL_COLS), lambda i, j: (i, jnp.maximum(nb - 2 - j, 0), 0))
    in_specs = [fw(G_QK), fw(G_QK), fw(G_WIDTH), fw(SMALL_COLS),
                bw(G_QK), bw(G_QK), bw(G_WIDTH), bw(SMALL_COLS), fw_next, bw_next]
    in_specs += [const(a) for a in consts] + [sst]
    out_specs = [sst]
    out_shape = [jax.ShapeDtypeStruct(s0.shape, F32)]
    if emit:
        out_specs = [fw(G_WIDTH), bw(G_WIDTH)] + out_specs
        out_shape = [jax.ShapeDtypeStruct((b, t, G_WIDTH), F32)] * 2 + out_shape
    nch = tb // G_CHUNK
    return pl.pallas_call(
        functools.partial(_gla_kernel, emit, nch, nb > 1),
        grid=(b, nb),
        in_specs=in_specs,
        out_specs=out_specs,
        out_shape=out_shape,
        scratch_shapes=[pltpu.VMEM((2, G_DV, G_QK), F32),
                        pltpu.VMEM((2, 2, G_SPLIT * G_CHUNK, nch * G_QK), BF16),
                        pltpu.SMEM((2,), F32)],
        compiler_params=_params("parallel", "arbitrary"),
        name="gla_scan" if emit else "gla_ctx_state",
    )(gq, gk, gv, small, gq, gk, gv, small, small, small, *consts, s0)


FFN_CHUNK = 256


def _head_norm(h, nheads, width):
    parts = []
    for hd in range(nheads):
        blk = h[:, hd * width:(hd + 1) * width]
        parts.append(blk * lax.rsqrt(jnp.mean(blk * blk, -1, keepdims=True) + EPS))
    return jnp.concatenate(parts, 1)


def _out_kernel(d_ff, x_ref, hf_ref, hb_ref, of_ref, ob_ref, mo_ref, gr_ref, mod_ref,
                mng_ref, gng_ref, gffn_ref, gfin_ref, wout_ref, wgu_ref, wdown_ref, o_ref):
    x = x_ref[0]
    mod = mod_ref[0]
    hm_t = hf_ref[0] + hb_ref[0]
    parts = []
    for hd in range(M_HEADS):
        blk = hm_t[hd * M_DV:(hd + 1) * M_DV]
        parts.append(blk * lax.rsqrt(jnp.mean(blk * blk, 0, keepdims=True) + EPS))
    hm_t = jnp.concatenate(parts, 0) * mng_ref[...] * _sigmoid(mo_ref[0].astype(F32))
    gr = gr_ref[0].astype(F32)
    hg = _head_norm(of_ref[0] + ob_ref[0], G_HEADS, G_DV) * gng_ref[...] * (gr * _sigmoid(gr))
    mixed = _dot_tn(hm_t.astype(BF16), wout_ref[:M_WIDTH, :]) + _dot(hg.astype(BF16), wout_ref[M_WIDTH:, :])
    x1 = x + mod[0:1] * mixed
    h2 = x1 * lax.rsqrt(jnp.mean(x1 * x1, -1, keepdims=True) + EPS) * gffn_ref[...]
    h2 = (h2 * (1.0 + mod[2:3]) + mod[1:2]).astype(BF16)
    def gated(c0):
        a = _dot(h2, wgu_ref[:, c0:c0 + FFN_CHUNK])
        g = _dot(h2, wgu_ref[:, d_ff + c0:d_ff + c0 + FFN_CHUNK])
        return (a * _sigmoid(a) * g).astype(BF16)

    starts = list(range(0, d_ff, FFN_CHUNK))
    act = gated(starts[0])
    acc = None
    for i, c0 in enumerate(starts):
        nxt = gated(starts[i + 1]) if i + 1 < len(starts) else None
        part = _dot(act, wdown_ref[c0:c0 + FFN_CHUNK, :])
        acc = part if acc is None else acc + part
        act = nxt
    x2 = x1 + mod[3:4] * acc
    o_ref[0] = x2 * lax.rsqrt(jnp.mean(x2 * x2, -1, keepdims=True) + EPS) * gfin_ref[...]


def _out_block(x, hf, hb, of, ob, mo, gr, mod4, mng, gng, gffn, gfin, wout, wgu, wdown, tm):
    b, t, d = x.shape
    d_ff = wdown.shape[0]
    tok = lambda w: pl.BlockSpec((1, tm, w), lambda i, j: (i, j, 0))
    feat = lambda w: pl.BlockSpec((1, w, tm), lambda i, j: (i, 0, j))
    const = lambda a: pl.BlockSpec(a.shape, lambda i, j: (0,) * a.ndim, pipeline_mode=pl.Buffered(1))
    return pl.pallas_call(
        functools.partial(_out_kernel, d_ff),
        grid=(b, t // tm),
        in_specs=[tok(d), feat(M_WIDTH), feat(M_WIDTH), tok(G_WIDTH), tok(G_WIDTH), feat(M_WIDTH), tok(G_WIDTH),
                  pl.BlockSpec((1, 4, d), lambda i, j: (i, 0, 0)),
                  const(mng), const(gng), const(gffn), const(gfin), const(wout), const(wgu), const(wdown)],
        out_specs=tok(d),
        out_shape=jax.ShapeDtypeStruct((b, t, d), F32),
        compiler_params=_params("parallel", "arbitrary"),
        name="out_ffn",
    )(x, hf, hb, of, ob, mo, gr, mod4, mng, gng, gffn, gfin, wout, wgu, wdown)


def _largest_tile(t, cap):
    tm = min(t, cap)
    while t % tm:
        tm //= 2
    return tm


def kernel(x, c, ctx, c_ctx, w_ada, b_ada, g_mix, w_in, conv_w, m_gate_b, m_norm_g, g_gate_w, g_gate_b,
           g_norm_g, w_out, g_ffn, w_gu, w_down, g_final):
    b, t, d = x.shape
    tc = ctx.shape[1]
    assert w_ada.shape[0] == 1, "single layer block"
    assert t % (GRID_W * 2) == 0 and tc % M_CHUNK == 0

    pad = (-(b + 1)) % 8
    c_all = jnp.concatenate([c, c_ctx[None, :], jnp.zeros((pad, d), F32)], 0)
    mod = _modulation(c_all, w_ada[0], b_ada[0])
    mod_x = mod[:b].reshape(b, 6, d)
    mod_c = jnp.broadcast_to(mod[b].reshape(1, 6, d), (b, 6, d))

    w = w_in[0]
    sizes = (2 * M_QK, M_WIDTH, M_WIDTH, N_GATES, G_QK, G_QK, G_WIDTH, G_WIDTH, 2 * G_RANK)
    offs = np.concatenate([[0], np.cumsum(sizes)])
    seg = lambda i: w[:, offs[i]:offs[i + 1]]
    wtok = jnp.concatenate([seg(0), seg(4), seg(5), seg(6), seg(7)], 1).astype(BF16)
    wfeat = jnp.transpose(jnp.concatenate([seg(1), seg(2)], 1)).astype(BF16)
    n_small = N_GATES + 2 * G_RANK
    wsmall = jnp.concatenate([seg(3), seg(8), jnp.zeros((d, SMALL_COLS - n_small), F32)], 1).astype(BF16)
    wgt = jnp.transpose(seg(3)).astype(BF16)
    bsm = jnp.concatenate([m_gate_b[0], jnp.zeros((SMALL_COLS - N_GATES,), F32)]).reshape(1, SMALL_COLS)
    bgt = m_gate_b[0].reshape(N_GATES, 1)
    g_mix2 = g_mix[0].reshape(1, d)
    conv_w9 = conv_w[0].reshape(9, 2 * M_QK)
    kscale = jnp.concatenate([jnp.ones((M_QK,), F32), jnp.full((M_QK,), M_DK ** -0.5, F32)]).reshape(1, 2 * M_QK)
    ggw = g_gate_w[0].astype(BF16)
    ggb = g_gate_b[0].reshape(2, 1, G_QK)

    feat_consts = (g_mix2, wtok, wfeat, wsmall, wgt, bsm, bgt, conv_w9, kscale)

    def features(inp, m6, cols, tm):
        mqk, gq, gk, gv, gr, mvt, mot, small, gt = _features(inp, m6[:, 0:1], m6[:, 1:2], feat_consts, cols, tm)
        return dict(mqk=mqk, gq=gq, gk=gk, gv=gv, gr=gr, mvt=mvt, mot=mot, small=small, gt=gt)

    fc = features(ctx, mod_c, tc, tc)
    c0 = jnp.zeros((b, 2 * M_HEADS, M_AUG, M_DK), F32)
    m0 = jnp.full((b, 2 * M_HEADS, LANE), NEG, F32)
    c_ctx_state, m_ctx_state = _mlstm_scan(fc["mqk"], fc["mvt"], fc["gt"], c0, m0, False,
                                           _largest_tile(tc, M_CHUNK))
    s0 = jnp.zeros((b, 2, G_DV, G_QK), F32)
    (s_ctx_state,) = _gla_scan(fc["gq"], fc["gk"], fc["gv"], fc["small"], ggw, ggb, s0, False,
                               _largest_tile(tc, 256))

    fx = features(x, mod_x, GRID_W, _largest_tile(t, 1024))
    hf_t, hb_t, _, _ = _mlstm_scan(fx["mqk"], fx["mvt"], fx["gt"], c_ctx_state, m_ctx_state, True,
                                   _largest_tile(t, 1024))
    of, ob, _ = _gla_scan(fx["gq"], fx["gk"], fx["gv"], fx["small"], ggw, ggb, s_ctx_state, True,
                          _largest_tile(t, 512))

    tm = _largest_tile(t, 512)
    mod4 =jnp.stack([mod_x[:, 2], mod_x[:, 3], mod_x[:, 4], mod_x[:, 5]], 1)
    mng = jnp.broadcast_to(m_norm_g[0].reshape(M_WIDTH, 1), (M_WIDTH, tm))
    return _out_block(x, hf_t, hb_t, of, ob, fx["mot"], fx["gr"], mod4,
                      mng, g_norm_g[0].reshape(1, -1),
                      g_ffn[0].reshape(1, d), g_final.reshape(1, d),
                      w_out[0].astype(BF16), w_gu[0].astype(BF16), w_down[0].astype(BF16), tm)
```

```python
import functools

import numpy as np
import jax
import jax.numpy as jnp
from jax import lax
from jax.experimental import pallas as pl
from jax.experimental.pallas import tpu as pltpu

F32 = jnp.float32
BF16 = jnp.bfloat16

GRID_W = 64
M_HEADS = 4
M_DK = 128
M_DV = 128
M_QK = M_HEADS * M_DK
M_WIDTH = M_HEADS * M_DV
M_CHUNK = 256
G_HEADS = 4
G_DK = 64
G_DV = 128
G_QK = G_HEADS * G_DK
G_WIDTH = G_HEADS * G_DV
G_RANK = 16
G_TAU = 16.0
G_CHUNK = 64
EPS = 1e-6
NEG = -1e30
LOG2_E = 1.4426950408889634

LANE = 128
VMEM_LIMIT = 56 * 1024 * 1024


def _params(*sem):
    return pltpu.CompilerParams(dimension_semantics=sem, vmem_limit_bytes=VMEM_LIMIT)


def _sigmoid(x):
    return 1.0 / (1.0 + jnp.exp(-x))


def _log_sigmoid(x):
    return jnp.minimum(x, 0.0) - jnp.log(1.0 + jnp.exp(-jnp.abs(x)))


def _split(x, n):
    pieces = []
    for _ in range(n - 1):
        p = x.astype(BF16)
        pieces.append(p)
        x = x - p.astype(F32)
    pieces.append(x.astype(BF16))
    return pieces


def _split3(x):
    return _split(x, 3)


G_SPLIT = 2


def _dot(a, b):
    return jnp.dot(a, b, preferred_element_type=F32)


def _dot_nt(a, b):
    return lax.dot_general(a, b, (((1,), (1,)), ((), ())), preferred_element_type=F32)


def _dot_tn(a, b):
    return lax.dot_general(a, b, (((0,), (0,)), ((), ())), preferred_element_type=F32)


def _mod_kernel(c_ref, w_ref, b_ref, o_ref):
    cs = c_ref[...]
    a = (cs * _sigmoid(cs)).astype(BF16)
    o_ref[...] = _dot(a, w_ref[...].astype(BF16)) + b_ref[...]


def _modulation(c_all, w_ada, b_ada):
    rows, d = c_all.shape
    n = w_ada.shape[1]
    bn = n // 6
    return pl.pallas_call(
        _mod_kernel,
        grid=(n // bn,),
        in_specs=[pl.BlockSpec((rows, d), lambda j: (0, 0)),
                  pl.BlockSpec((d, bn), lambda j: (0, j)),
                  pl.BlockSpec((1, bn), lambda j: (0, j))],
        out_specs=pl.BlockSpec((rows, bn), lambda j: (0, j)),
        out_shape=jax.ShapeDtypeStruct((rows, n), F32),
        compiler_params=_params("arbitrary"),
        name="adaln_mod",
    )(c_all, w_ada, b_ada.reshape(1, n))


_TOK_SIZES = (("mqk", 2 * M_QK), ("gq", G_QK), ("gk", G_QK), ("gv", G_WIDTH), ("gr", G_WIDTH))
_FEAT_SIZES = (("mv", M_WIDTH), ("mo", M_WIDTH))
N_GATES = 4 * M_HEADS
SMALL_COLS = LANE


CONV_CB = 256


def _conv_silu(ext, w9, cols, nrow):
    tot = (nrow + 2) * cols
    col = lax.broadcasted_iota(jnp.int32, ext.shape, 0) % cols
    left = jnp.where(col == 0, 0.0, pltpu.roll(ext, 1, 0))
    right = jnp.where(col == cols - 1, 0.0, pltpu.roll(ext, tot - 1, 0))
    srcs = (left, ext, right)
    acc = None
    for di in range(3):
        for dj in range(3):
            term = srcs[dj][di * cols:(di + nrow) * cols] * w9[di * 3 + dj:di * 3 + dj + 1]
            acc = term if acc is None else acc + term
    return acc * _sigmoid(acc)


def _feat_kernel(cols, halo, x_ref, *rest):
    xn_ref = rest[0] if halo else None
    (mod_ref, g_ref, wtok_ref, wfeat_ref, wsmall_ref, bsm_ref, bgt_ref, cw_ref, ksc_ref,
     mqk_ref, gq_ref, gk_ref, gv_ref, gr_ref, mvt_ref, mot_ref, small_ref, gt_ref, prev_s) = rest[1 if halo else 0:]
    j = pl.program_id(1)
    last = pl.num_programs(1) - 1
    tm = x_ref.shape[1]
    shift = mod_ref[0, 0:1]
    gain = g_ref[...] * (1.0 + mod_ref[0, 1:2])

    def norm_mod(x):
        return (x * lax.rsqrt(jnp.mean(x * x, -1, keepdims=True) + EPS) * gain + shift).astype(BF16)

    hb = norm_mod(x_ref[0])
    hext = jnp.concatenate([hb, norm_mod(xn_ref[0])], 0) if halo else hb

    @pl.when(j == 0)
    def _():
        prev_s[...] = jnp.zeros_like(prev_s)

    outs = dict(gq=gq_ref, gk=gk_ref, gv=gv_ref, gr=gr_ref, mv=mvt_ref, mo=mot_ref)
    tasks = []

    def tok_task(name, off, c0):
        def run():
            z = _dot(hb, wtok_ref[:, off + c0:off + c0 + CONV_CB])
            if name == "gq":
                z = z * (G_DK ** -0.5)
            outs[name][0, :, c0:c0 + CONV_CB] = z.astype(BF16)
        return run

    def feat_task():
        z = _dot_nt(wfeat_ref[...], hb)
        off = 0
        for name, size in _FEAT_SIZES:
            outs[name][0] = z[off:off + size].astype(BF16)
            off += size
        gt_ref[0] = z[off:] + bgt_ref[...]

    off = 2 * M_QK
    for name, size in _TOK_SIZES[1:]:
        tasks += [tok_task(name, off, c0) for c0 in range(0, size, CONV_CB)]
        off += size
    tasks.append(feat_task)

    nblk = 2 * M_QK // CONV_CB
    for i in range(nblk):
        cs = slice(i * CONV_CB, (i + 1) * CONV_CB)
        z = _dot(hext, wtok_ref[:, cs])
        for task in tasks[i * len(tasks) // nblk:(i + 1) * len(tasks) // nblk]:
            task()
        below = jnp.where(j < last, z[tm:], 0.0) if halo else jnp.zeros((cols, CONV_CB), F32)
        ext = jnp.concatenate([prev_s[:, cs], z[:tm], below], 0)
        prev_s[:, cs] = z[tm - cols:tm]
        mqk_ref[0, :, cs] = (_conv_silu(ext, cw_ref[:, cs], cols, tm // cols) * ksc_ref[:, cs]).astype(BF16)
    small_ref[0] = _dot(hb, wsmall_ref[...]) + bsm_ref[...]


def _features(x, mod, mod_row, consts, cols, tm):
    b, t, d = x.shape
    nt = t // tm
    per = tm // cols
    halo = nt > 1
    tok = lambda w: pl.BlockSpec((1, tm, w), lambda i, j: (i, j, 0))
    feat = lambda w: pl.BlockSpec((1, w, tm), lambda i, j: (i, 0, j))
    below = pl.BlockSpec((1, cols, d), lambda i, j: (i, jnp.minimum((j + 1) * per, t // cols - 1), 0))
    const = lambda a: pl.BlockSpec(a.shape, lambda i, j: (0,) * a.ndim)
    vec = pl.BlockSpec((1,) + mod.shape[1:], lambda i, j: (mod_row(i), 0, 0))
    out_shape = [jax.ShapeDtypeStruct((b, t, s), BF16) for _, s in _TOK_SIZES]
    out_specs = [tok(s) for _, s in _TOK_SIZES]
    out_shape += [jax.ShapeDtypeStruct((b, s, t), BF16) for _, s in _FEAT_SIZES]
    out_specs += [feat(s) for _, s in _FEAT_SIZES]
    out_shape += [jax.ShapeDtypeStruct((b, t, SMALL_COLS), F32), jax.ShapeDtypeStruct((b, N_GATES, t), F32)]
    out_specs += [tok(SMALL_COLS), feat(N_GATES)]
    return pl.pallas_call(
        functools.partial(_feat_kernel, cols, halo),
        grid=(b, nt),
        in_specs=[tok(d)] + ([below] if halo else []) + [vec] + [const(a) for a in consts],
        out_specs=out_specs,
        out_shape=out_shape,
        scratch_shapes=[pltpu.VMEM((cols, 2 * M_QK), F32)],
        compiler_params=_params("parallel", "arbitrary"),
        name="features",
    )(*([x, x] if halo else [x]), mod, *consts)


M_AUG = M_DV + 16


def _mlstm_cumsum_matrix(reverse):
    L = M_CHUNK
    low = np.tril(np.ones((L, L), np.float32))
    tri = low.T if reverse else low
    return jnp.asarray(np.tile(np.concatenate([tri.T, np.ones((L, L), np.float32)], 1), (3, 1)), BF16)


def _mlstm_kernel(emit, nch, qf_ref, kf_ref, vf_ref, grf_ref, qb_ref, kb_ref, vb_ref, grb_ref,
                  rowf_ref, rowb_ref, c0_ref, m0_ref, *rest):
    if emit:
        hf_ref, hb_ref, cout_ref, mout_ref, c_s, m_s = rest
    else:
        cout_ref, mout_ref, c_s, m_s = rest
        hf_ref = hb_ref = None
    j = pl.program_id(1)
    L = M_CHUNK

    @pl.when(j == 0)
    def _():
        c_s[...] = c0_ref[0]
        m_s[...] = m0_ref[0]

    sid = lax.broadcasted_iota(jnp.int32, (L, L), 0)
    tid = lax.broadcasted_iota(jnp.int32, (L, L), 1)
    ones_rows = jnp.ones((M_AUG - M_DV, L), BF16)
    lane = lax.broadcasted_iota(jnp.int32, (M_HEADS, L), 1)

    refs = ((qf_ref, kf_ref, vf_ref, grf_ref, rowf_ref, hf_ref),
            (qb_ref, kb_ref, vb_ref, grb_ref, rowb_ref, hb_ref))
    masks = (sid <= tid, sid >= tid)

    def gate_stage(i):
        chains = []
        for d in range(2):
            q_ref, k_ref, v_ref, gr_ref, row_ref, h_ref = refs[d]
            c = i if d == 0 else nch - 1 - i
            tok = slice(c * L, (c + 1) * L)
            grow = gr_ref[0, :, tok]
            cum_ext = _dot(jnp.concatenate(_split3(_log_sigmoid(grow) * LOG2_E), 1), row_ref[...])
            grow = grow * LOG2_E
            i0 = 2 * M_HEADS * d
            cc_rows = grow[i0:i0 + M_HEADS] - cum_ext[i0 + M_HEADS:i0 + 2 * M_HEADS, :L]
            cmax = cc_rows
            step = 1
            while step < L:
                if d == 0:
                    moved = jnp.where(lane >= step, pltpu.roll(cmax, step, 1), NEG)
                else:
                    moved = jnp.where(lane < L - step, pltpu.roll(cmax, L - step, 1), NEG)
                cmax = jnp.maximum(cmax, moved)
                step *= 2
            cc_cols = jnp.transpose(jnp.concatenate([cc_rows, cc_rows], 0))
            for hd in range(M_HEADS):
                cf = i0 + M_HEADS + hd
                sl = slice(hd * M_DK, (hd + 1) * M_DK)
                chains.append(dict(
                    d=d, r=d * M_HEADS + hd, sl=sl, tok=tok,
                    q=q_ref[0, tok, sl], k=k_ref[0, tok, sl],
                    v_aug=jnp.concatenate([v_ref[0, sl, tok], ones_rows], 0),
                    br=cum_ext[cf:cf + 1, :L],
                    btot=cum_ext[cf:cf + 1, L:],
                    ir=grow[i0 + hd:i0 + hd + 1, :],
                    cmax=cmax[hd:hd + 1, :],
                    cc=cc_cols[:, hd:hd + 1]))
        return chains

    for chains in [gate_stage(i) for i in range(nch)]:
        for ch in chains:
            r = ch["r"]
            m = jnp.concatenate([m_s[r:r + 1, :]] * (L // LANE), 1)
            ct = c_s[r]
            ch["m"] = m
            if emit:
                ch["big"] = _dot_nt(jnp.concatenate([ch["k"], ct.astype(BF16)], 0), ch["q"])
            src = ch["btot"] + ch["ir"] - ch["br"]
            m_new = jnp.maximum(m + ch["btot"], jnp.max(src, -1, keepdims=True))
            ws = jnp.exp2(src - m_new).astype(BF16)
            wc = jnp.exp2((m + ch["btot"] - m_new)[:, :M_DK])
            c_s[r] = wc * ct + _dot(ch["v_aug"] * ws, ch["k"])
            m_s[r:r + 1, :] = m_new[:, :LANE]

        if emit:
            for ch in chains:
                m, br, big = ch["m"], ch["br"], ch["big"]
                mt = br + jnp.maximum(m, ch["cmax"])
                pt = jnp.where(masks[ch["d"]], big[:L] * jnp.exp2(ch["cc"] + (br - mt)), 0.0)
                num = jnp.exp2(m + br - mt) * big[L:] + _dot(ch["v_aug"], pt.astype(BF16))
                inv = 1.0 / jnp.maximum(jnp.abs(num[M_DV:M_DV + 1]), jnp.exp2(-mt))
                refs[ch["d"]][5][0, ch["sl"], ch["tok"]] = num[:M_DV] * inv

    @pl.when(j == pl.num_programs(1) - 1)
    def _():
        cout_ref[0] = c_s[...]
        mout_ref[0] = m_s[...]


def _mlstm_scan(mqk, mvt, gt, c0, m0, emit, tb):
    b, _, t = mvt.shape
    nb = t // tb
    consts = (_mlstm_cumsum_matrix(False), _mlstm_cumsum_matrix(True))
    fw = lambda w, cblk: pl.BlockSpec((1, tb, w), lambda i, j: (i, j, cblk))
    bw = lambda w, cblk: pl.BlockSpec((1, tb, w), lambda i, j: (i, nb - 1 - j, cblk))
    ftf = lambda w: pl.BlockSpec((1, w, tb), lambda i, j: (i, 0, j))
    ftb = lambda w: pl.BlockSpec((1, w, tb), lambda i, j: (i, 0, nb - 1 - j))
    const = lambda a: pl.BlockSpec(a.shape, lambda i, j: (0,) * a.ndim)
    cst = pl.BlockSpec((1, 2 * M_HEADS, M_AUG, M_DK), lambda i, j: (i, 0, 0, 0))
    mst = pl.BlockSpec((1, 2 * M_HEADS, LANE), lambda i, j: (i, 0, 0))
    in_specs = [fw(M_QK, 0), fw(M_QK, 1), ftf(M_WIDTH), ftf(N_GATES),
                bw(M_QK, 0), bw(M_QK, 1), ftb(M_WIDTH), ftb(N_GATES)]
    in_specs += [const(a) for a in consts] + [cst, mst]
    out_specs = [cst, mst]
    out_shape = [jax.ShapeDtypeStruct(c0.shape, F32), jax.ShapeDtypeStruct(m0.shape, F32)]
    if emit:
        out_specs = [ftf(M_WIDTH), ftb(M_WIDTH)] + out_specs
        out_shape = [jax.ShapeDtypeStruct((b, M_WIDTH, t), F32)] * 2 + out_shape
    return pl.pallas_call(
        functools.partial(_mlstm_kernel, emit, tb // M_CHUNK),
        grid=(b, nb),
        in_specs=in_specs,
        out_specs=out_specs,
        out_shape=out_shape,
        scratch_shapes=[pltpu.VMEM((2 * M_HEADS, M_AUG, M_DK), F32), pltpu.VMEM((2 * M_HEADS, LANE), F32)],
        compiler_params=_params("parallel", "arbitrary"),
        name="mlstm_scan" if emit else "mlstm_ctx_state",
    )(mqk, mqk, mvt, gt, mqk, mqk, mvt, gt, *consts, c0, m0)


G_LEVELS = (32, 16, 8, 4, 2, 1)


def _gla_select_matrices(reverse):
    L = G_CHUNK
    t = np.arange(L)
    u = np.arange(L)[None, :]
    tt = t[:, None]
    mats = [(u <= tt), (u > tt)]
    for h in G_LEVELS:
        start = (tt // (2 * h)) * (2 * h)
        second = tt >= start + h
        m = np.where(second, (u >= start + h) & (u <= tt), (u > tt) & (u < start + h))
        mats.append(m)
    m = np.concatenate([x.astype(np.float32) for x in mats], 0)
    if reverse:
        m = m.reshape(len(mats), L, L)[:, ::-1, ::-1].reshape(len(mats) * L, L)
    return jnp.asarray(np.concatenate([m] * G_SPLIT, 1), BF16)


def _gla_pair_masks(reverse):
    L = G_CHUNK
    s = np.arange(L)[:, None]
    t = np.arange(L)[None, :]
    masks = [s == t]
    for h in G_LEVELS:
        same = (s // (2 * h)) == (t // (2 * h))
        masks.append(same & ((t // h) % 2 == 1) & ((s // h) % 2 == 0))
    m = np.stack(masks).astype(np.float32)
    if reverse:
        m = m[:, ::-1, ::-1]
    return jnp.asarray(np.tile(m, (1, 1, G_HEADS)), F32)


def _gla_fast_matrices(reverse):
    L = G_CHUNK
    h = L // 2
    tt = np.arange(L)[:, None]
    u = np.arange(L)[None, :]
    mid = np.where(tt >= h, ((u >= h) & (u <= tt)).astype(np.float32), -((u > tt) & (u < h)).astype(np.float32))
    m = np.stack([(u <= tt).astype(np.float32), (u > tt).astype(np.float32), mid])
    if reverse:
        m = m[:, ::-1, ::-1]
    m = m.reshape(3 * L, L)
    return jnp.asarray(np.concatenate([m] * G_SPLIT, 1), BF16)


def _gla_causal_mask(reverse):
    s = np.arange(G_CHUNK)[:, None]
    t = np.arange(G_CHUNK)[None, :]
    m = (s >= t) if reverse else (s <= t)
    return jnp.asarray(np.tile(m.astype(np.float32), (1, G_HEADS)), F32)


def _gla_head_mask():
    r = np.arange(G_HEADS * G_CHUNK)[:, None] // G_CHUNK
    c = np.arange(G_QK)[None, :] // G_DK
    return jnp.asarray((r == c).astype(np.float32), BF16)


GLA_FAST_LIMIT = 40.0


def _gla_kernel(emit, nch, pipelined, qf_ref, kf_ref, vf_ref, lrf_ref, qb_ref, kb_ref, vb_ref, lrb_ref,
                lrnf_ref, lrnb_ref, lvf_ref, lvb_ref, fastf_ref, fastb_ref, mkf_ref, mkb_ref, cmf_ref, cmb_ref,
                hm_ref, gw_ref, gb_ref, s0_ref, *rest):
    if emit:
        of_ref, ob_ref, sout_ref, s_s, a3_s, lo_s = rest
    else:
        sout_ref, s_s, a3_s, lo_s = rest
        of_ref = ob_ref = None
    j = pl.program_id(1)
    L = G_CHUNK
    slot = j % 2

    def stage_decays(lr_refs, dst):
        lo = None
        for d, lr_ref in enumerate(lr_refs):
            lr = lr_ref[0][:, N_GATES + d * G_RANK:N_GATES + (d + 1) * G_RANK].astype(BF16)
            loga = _log_sigmoid(_dot(lr, gw_ref[d]) + gb_ref[d]) * (LOG2_E / G_TAU)
            pieces = _split(loga, G_SPLIT)
            a3_s[dst, d] = jnp.concatenate(
                [jnp.concatenate([p[c * L:(c + 1) * L] for p in pieces], 0) for c in range(nch)], 1)
            m = jnp.min(loga)
            lo = m if lo is None else jnp.minimum(lo, m)
        lo_s[dst] = lo * L

    @pl.when(j == 0)
    def _():
        s_s[...] = s0_ref[0]
        stage_decays((lrf_ref, lrb_ref), 0)

    hm = hm_ref[...]
    stack = lambda a: jnp.concatenate([a] * G_HEADS, 0)

    refs = ((qf_ref, kf_ref, vf_ref, mkf_ref, cmf_ref, of_ref), (qb_ref, kb_ref, vb_ref, mkb_ref, cmb_ref, ob_ref))

    def run(fast):
        chunks = []
        for d in range(2):
            q_ref, k_ref, v_ref, mk_ref, cm_ref, o_ref = refs[d]
            sel_ref = ((fastf_ref, fastb_ref) if fast else (lvf_ref, lvb_ref))[d]
            g = _dot(sel_ref[...], a3_s[slot, d])
            expo = jnp.exp2(g[:2 * L] if fast else g)
            for c in (range(nch) if d == 0 else reversed(range(nch))):
                rows = slice(c * L, (c + 1) * L)
                cols = slice(c * G_QK, (c + 1) * G_QK)
                chunks.append(dict(
                    d=d, rows=rows, cols=cols, g=g, expo=expo, k=k_ref[0, rows, :],
                    v_st=jnp.concatenate([v_ref[0, rows, hd * G_DV:(hd + 1) * G_DV] for hd in range(G_HEADS)], 0),
                    e_tot=expo[L - 1:L, cols] if d == 0 else expo[0:1, cols]))

        def e(ch, i):
            return ch["expo"][i * L:(i + 1) * L, ch["cols"]].astype(BF16)

        for ch in chunks:
            k = ch["k"]
            ch["kd_st"] = stack(k * e(ch, 1)) * hm
            if emit:
                q_ref, mk_ref, cm_ref = refs[ch["d"]][0], refs[ch["d"]][3], refs[ch["d"]][4]
                qs = stack(q_ref[0, ch["rows"], :]) * hm
                ch["qs"] = qs
                if fast:
                    gm = ch["g"][2 * L:3 * L, ch["cols"]]
                    att_t = _dot_nt(k * jnp.exp2(-gm).astype(BF16), qs * stack(jnp.exp2(gm).astype(BF16)))
                    att_t = jnp.where(cm_ref[...] != 0.0, att_t, 0.0)
                else:
                    att_t = mk_ref[0] * _dot_nt(k, qs)
                    for lv in range(len(G_LEVELS)):
                        ev = e(ch, 2 + lv)
                        att_t = att_t + mk_ref[1 + lv] * _dot_nt(k * ev, qs * stack(ev))
                ch["att_t"] = att_t

        if pipelined:
            stage_decays((lrnf_ref, lrnb_ref), 1 - slot)

        for ch in chunks:
            if emit:
                att_st = stack(ch["att_t"].astype(BF16)) * hm
                ch["res"] = _dot_tn(ch["v_st"], jnp.concatenate([att_st, ch["kd_st"]], 1))
            else:
                ch["res"] = _dot_tn(ch["v_st"], ch["kd_st"])

        order = [chunks[dd * nch + i] for i in range(nch) for dd in range(2)]
        for ch in order:
            d = ch["d"]
            st = s_s[d]
            res = ch["res"]
            if emit:
                inter_t = _dot_nt(st.astype(BF16), ch["qs"] * stack(e(ch, 0)))
                o = jnp.transpose(inter_t + res[:, :G_HEADS * L])
                o_ref = refs[d][5]
                for hd in range(G_HEADS):
                    o_ref[0, ch["rows"], hd * G_DV:(hd + 1) * G_DV] = o[hd * L:(hd + 1) * L]
                res = res[:, G_HEADS * L:]
            s_s[d] = ch["e_tot"] * st + res

    if emit:
        mild = lo_s[slot] >= -GLA_FAST_LIMIT
        pl.when(mild)(lambda: run(True))
        pl.when(jnp.logical_not(mild))(lambda: run(False))
    else:
        run(True)

    @pl.when(j == pl.num_programs(1) - 1)
    def _():
        sout_ref[0] = s_s[...]


def _gla_scan(gq, gk, gv, small, g_gate_w, g_gate_b, s0, emit, tb):
    b, t, _ = gv.shape
    nb = t // tb
    consts = (_gla_select_matrices(False), _gla_select_matrices(True),
              _gla_fast_matrices(False), _gla_fast_matrices(True),
              _gla_pair_masks(False), _gla_pair_masks(True),
              _gla_causal_mask(False), _gla_causal_mask(True), _gla_head_mask(), g_gate_w, g_gate_b)
    fw = lambda w: pl.BlockSpec((1, tb, w), lambda i, j: (i, j, 0))
    bw = lambda w: pl.BlockSpec((1, tb, w), lambda i, j: (i, nb - 1 - j, 0))
    const = lambda a: pl.BlockSpec(a.shape, lambda i, j: (0,) * a.ndim)
    sst = pl.BlockSpec((1, 2, G_DV, G_QK), lambda i, j: (i, 0, 0, 0))
    fw_next = pl.BlockSpec((1, tb, SMALL_COLS), lambda i, j: (i, jnp.minimum(j + 1, nb - 1), 0))
    bw_next = pl.BlockSpec((1, tb, SMALL_COLS), lambda i, j: (i, jnp.maximum(nb - 2 - j, 0), 0))
    in_specs = [fw(G_QK), fw(G_QK), fw(G_WIDTH), fw(SMALL_COLS),
                bw(G_QK), bw(G_QK), bw(G_WIDTH), bw(SMALL_COLS), fw_next, bw_next]
    in_specs += [const(a) for a in consts] + [sst]
    out_specs = [sst]
    out_shape = [jax.ShapeDtypeStruct(s0.shape, F32)]
    if emit:
        out_specs = [fw(G_WIDTH), bw(G_WIDTH)] + out_specs
        out_shape = [jax.ShapeDtypeStruct((b, t, G_WIDTH), F32)] * 2 + out_shape
    nch = tb // G_CHUNK
    return pl.pallas_call(
        functools.partial(_gla_kernel, emit, nch, nb > 1),
        grid=(b, nb),
        in_specs=in_specs,
        out_specs=out_specs,
        out_shape=out_shape,
        scratch_shapes=[pltpu.VMEM((2, G_DV, G_QK), F32),
                        pltpu.VMEM((2, 2, G_SPLIT * G_CHUNK, nch * G_QK), BF16),
                        pltpu.SMEM((2,), F32)],
        compiler_params=_params("parallel", "arbitrary"),
        name="gla_scan" if emit else "gla_ctx_state",
    )(gq, gk, gv, small, gq, gk, gv, small, small, small, *consts, s0)


FFN_CHUNK = 256


def _head_norm(h, nheads, width):
    parts = []
    for hd in range(nheads):
        blk = h[:, hd * width:(hd + 1) * width]
        parts.append(blk * lax.rsqrt(jnp.mean(blk * blk, -1, keepdims=True) + EPS))
    return jnp.concatenate(parts, 1)


def _out_kernel(d_ff, x_ref, hf_ref, hb_ref, of_ref, ob_ref, mo_ref, gr_ref, mod_ref,
                mng_ref, gng_ref, gffn_ref, gfin_ref, wout_ref, wgu_ref, wdown_ref, o_ref):
    x = x_ref[0]
    mod = mod_ref[0][2:6]
    hm_t = hf_ref[0] + hb_ref[0]
    parts = []
    for hd in range(M_HEADS):
        blk = hm_t[hd * M_DV:(hd + 1) * M_DV]
        parts.append(blk * lax.rsqrt(jnp.mean(blk * blk, 0, keepdims=True) + EPS))
    hm_t = jnp.concatenate(parts, 0) * mng_ref[...] * _sigmoid(mo_ref[0].astype(F32))
    gr = gr_ref[0].astype(F32)
    hg = _head_norm(of_ref[0] + ob_ref[0], G_HEADS, G_DV) * gng_ref[...] * (gr * _sigmoid(gr))
    mixed = _dot_tn(hm_t.astype(BF16), wout_ref[:M_WIDTH, :]) + _dot(hg.astype(BF16), wout_ref[M_WIDTH:, :])
    x1 = x + mod[0:1] * mixed
    gain2 = gffn_ref[...] * (1.0 + mod[2:3])
    h2 = (x1 * lax.rsqrt(jnp.mean(x1 * x1, -1, keepdims=True) + EPS) * gain2 + mod[1:2]).astype(BF16)
    def gated(c0):
        a = _dot(h2, wgu_ref[:, c0:c0 + FFN_CHUNK])
        g = _dot(h2, wgu_ref[:, d_ff + c0:d_ff + c0 + FFN_CHUNK])
        return (a * _sigmoid(a) * g).astype(BF16)

    starts = list(range(0, d_ff, FFN_CHUNK))
    act = gated(starts[0])
    acc = None
    for i, c0 in enumerate(starts):
        nxt = gated(starts[i + 1]) if i + 1 < len(starts) else None
        part = _dot(act, wdown_ref[c0:c0 + FFN_CHUNK, :])
        acc = part if acc is None else acc + part
        act = nxt
    x2 = x1 + mod[3:4] * acc
    o_ref[0] = x2 * lax.rsqrt(jnp.mean(x2 * x2, -1, keepdims=True) + EPS) * gfin_ref[...]


def _out_block(x, hf, hb, of, ob, mo, gr, mod4, mng, gng, gffn, gfin, wout, wgu, wdown, tm):
    b, t, d = x.shape
    d_ff = wdown.shape[0]
    tok = lambda w: pl.BlockSpec((1, tm, w), lambda i, j: (i, j, 0))
    feat = lambda w: pl.BlockSpec((1, w, tm), lambda i, j: (i, 0, j))
    const = lambda a: pl.BlockSpec(a.shape, lambda i, j: (0,) * a.ndim, pipeline_mode=pl.Buffered(1))
    return pl.pallas_call(
        functools.partial(_out_kernel, d_ff),
        grid=(b, t // tm),
        in_specs=[tok(d), feat(M_WIDTH), feat(M_WIDTH), tok(G_WIDTH), tok(G_WIDTH), feat(M_WIDTH), tok(G_WIDTH),
                  pl.BlockSpec((1,) + mod4.shape[1:], lambda i, j: (i, 0, 0)),
                  const(mng), const(gng), const(gffn), const(gfin), const(wout), const(wgu), const(wdown)],
        out_specs=tok(d),
        out_shape=jax.ShapeDtypeStruct((b, t, d), F32),
        compiler_params=_params("parallel", "arbitrary"),
        name="out_ffn",
    )(x, hf, hb, of, ob, mo, gr, mod4, mng, gng, gffn, gfin, wout, wgu, wdown)


def _largest_tile(t, cap):
    tm = min(t, cap)
    while t % tm:
        tm //= 2
    return tm


def kernel(x, c, ctx, c_ctx, w_ada, b_ada, g_mix, w_in, conv_w, m_gate_b, m_norm_g, g_gate_w, g_gate_b,
           g_norm_g, w_out, g_ffn, w_gu, w_down, g_final):
    b, t, d = x.shape
    tc = ctx.shape[1]
    assert w_ada.shape[0] == 1, "single layer block"
    assert t % (GRID_W * 2) == 0 and tc % M_CHUNK == 0

    pad = (-(b + 1)) % 8
    c_all = jnp.concatenate([c, c_ctx[None, :], jnp.zeros((pad, d), F32)], 0)
    mod = _modulation(c_all, w_ada[0], b_ada[0]).reshape(b + 1 + pad, 6, d)

    w = w_in[0]
    sizes = (2 * M_QK, M_WIDTH, M_WIDTH, N_GATES, G_QK, G_QK, G_WIDTH, G_WIDTH, 2 * G_RANK)
    offs = np.concatenate([[0], np.cumsum(sizes)])
    seg = lambda i: w[:, offs[i]:offs[i + 1]]
    wtok = jnp.concatenate([seg(0), seg(4), seg(5), seg(6), seg(7)], 1).astype(BF16)
    wfeat = jnp.transpose(jnp.concatenate([seg(1), seg(2), seg(3)], 1)).astype(BF16)
    n_small = N_GATES + 2 * G_RANK
    wsmall = jnp.concatenate([seg(3), seg(8), jnp.zeros((d, SMALL_COLS - n_small), F32)], 1).astype(BF16)
    bsm = jnp.concatenate([m_gate_b[0], jnp.zeros((SMALL_COLS - N_GATES,), F32)]).reshape(1, SMALL_COLS)
    bgt = m_gate_b[0].reshape(N_GATES, 1)
    g_mix2 = g_mix[0].reshape(1, d)
    conv_w9 = conv_w[0].reshape(9, 2 * M_QK)
    kscale = jnp.concatenate([jnp.ones((M_QK,), F32), jnp.full((M_QK,), M_DK ** -0.5, F32)]).reshape(1, 2 * M_QK)
    ggw = g_gate_w[0].astype(BF16)
    ggb = g_gate_b[0].reshape(2, 1, G_QK)

    feat_consts = (g_mix2, wtok, wfeat, wsmall, bsm, bgt, conv_w9, kscale)

    def features(inp, mod_row, cols, tm):
        mqk, gq, gk, gv, gr, mvt, mot, small, gt = _features(inp, mod, mod_row, feat_consts, cols, tm)
        return dict(mqk=mqk, gq=gq, gk=gk, gv=gv, gr=gr, mvt=mvt, mot=mot, small=small, gt=gt)

    fc = features(ctx, lambda i: b, tc, tc)
    c0 = jnp.zeros((b, 2 * M_HEADS, M_AUG, M_DK), F32)
    m0 = jnp.full((b, 2 * M_HEADS, LANE), NEG, F32)
    c_ctx_state, m_ctx_state = _mlstm_scan(fc["mqk"], fc["mvt"], fc["gt"], c0, m0, False,
                                           _largest_tile(tc, M_CHUNK))
    s0 = jnp.zeros((b, 2, G_DV, G_QK), F32)
    (s_ctx_state,) = _gla_scan(fc["gq"], fc["gk"], fc["gv"], fc["small"], ggw, ggb, s0, False,
                               _largest_tile(tc, 256))

    fx = features(x, lambda i: i, GRID_W, _largest_tile(t, 1024))
    hf_t, hb_t, _, _ = _mlstm_scan(fx["mqk"], fx["mvt"], fx["gt"], c_ctx_state, m_ctx_state, True,
                                   _largest_tile(t, 1024))
    of, ob, _ = _gla_scan(fx["gq"], fx["gk"], fx["gv"], fx["small"], ggw, ggb, s_ctx_state, True,
                          _largest_tile(t, 512))

    tm = _largest_tile(t, 512)
    mng = jnp.broadcast_to(m_norm_g[0].reshape(M_WIDTH, 1), (M_WIDTH, tm))
    return _out_block(x, hf_t, hb_t, of, ob, fx["mot"], fx["gr"], mod,
                      mng, g_norm_g[0].reshape(1, -1),
                      g_ffn[0].reshape(1, d), g_final.reshape(1, d),
                      w_out[0].astype(BF16), w_gu[0].astype(BF16), w_down[0].astype(BF16), tm)
```

```python
import functools

import numpy as np
import jax
import jax.numpy as jnp
from jax import lax
from jax.experimental import pallas as pl
from jax.experimental.pallas import tpu as pltpu

F32 = jnp.float32
BF16 = jnp.bfloat16

GRID_W = 64
M_HEADS = 4
M_DK = 128
M_DV = 128
M_QK = M_HEADS * M_DK
M_WIDTH = M_HEADS * M_DV
M_CHUNK = 256
G_HEADS = 4
G_DK = 64
G_DV = 128
G_QK = G_HEADS * G_DK
G_WIDTH = G_HEADS * G_DV
G_RANK = 16
G_TAU = 16.0
G_CHUNK = 64
EPS = 1e-6
NEG = -1e30
LOG2_E = 1.4426950408889634

LANE = 128
VMEM_LIMIT = 56 * 1024 * 1024


def _params(*sem):
    return pltpu.CompilerParams(dimension_semantics=sem, vmem_limit_bytes=VMEM_LIMIT)


def _sigmoid(x):
    return 1.0 / (1.0 + jnp.exp(-x))


def _log_sigmoid(x):
    return jnp.minimum(x, 0.0) - jnp.log(1.0 + jnp.exp(-jnp.abs(x)))


def _split(x, n):
    pieces = []
    for _ in range(n - 1):
        p = x.astype(BF16)
        pieces.append(p)
        x = x - p.astype(F32)
    pieces.append(x.astype(BF16))
    return pieces


def _split3(x):
    return _split(x, 3)


G_SPLIT = 2


def _dot(a, b):
    return jnp.dot(a, b, preferred_element_type=F32)


def _dot_nt(a, b):
    return lax.dot_general(a, b, (((1,), (1,)), ((), ())), preferred_element_type=F32)


def _dot_tn(a, b):
    return lax.dot_general(a, b, (((0,), (0,)), ((), ())), preferred_element_type=F32)


def _mod_kernel(c_ref, w_ref, b_ref, o_ref):
    cs = c_ref[...]
    a = (cs * _sigmoid(cs)).astype(BF16)
    o_ref[...] = _dot(a, w_ref[...].astype(BF16)) + b_ref[...]


def _modulation(c_all, w_ada, b_ada):
    rows, d = c_all.shape
    n = w_ada.shape[1]
    bn = n // 6
    return pl.pallas_call(
        _mod_kernel,
        grid=(n // bn,),
        in_specs=[pl.BlockSpec((rows, d), lambda j: (0, 0)),
                  pl.BlockSpec((d, bn), lambda j: (0, j)),
                  pl.BlockSpec((1, bn), lambda j: (0, j))],
        out_specs=pl.BlockSpec((rows, bn), lambda j: (0, j)),
        out_shape=jax.ShapeDtypeStruct((rows, n), F32),
        compiler_params=_params("arbitrary"),
        name="adaln_mod",
    )(c_all, w_ada, b_ada.reshape(1, n))


_TOK_SIZES = (("mqk", 2 * M_QK), ("gq", G_QK), ("gk", G_QK), ("gv", G_WIDTH), ("gr", G_WIDTH))
_FEAT_SIZES = (("mv", M_WIDTH), ("mo", M_WIDTH))
N_GATES = 4 * M_HEADS
GT_ROWS = N_GATES + 2 * G_RANK


CONV_CB = 256


def _conv_silu(ext, w9, cols, nrow):
    tot = (nrow + 2) * cols
    col = lax.broadcasted_iota(jnp.int32, ext.shape, 0) % cols
    left = jnp.where(col == 0, 0.0, pltpu.roll(ext, 1, 0))
    right = jnp.where(col == cols - 1, 0.0, pltpu.roll(ext, tot - 1, 0))
    srcs = (left, ext, right)
    acc = None
    for di in range(3):
        for dj in range(3):
            term = srcs[dj][di * cols:(di + nrow) * cols] * w9[di * 3 + dj:di * 3 + dj + 1]
            acc = term if acc is None else acc + term
    return acc * _sigmoid(acc)


def _feat_kernel(cols, halo, x_ref, *rest):
    xn_ref = rest[0] if halo else None
    (mod_ref, g_ref, wtok_ref, wfeat_ref, bgt_ref, cw_ref, ksc_ref,
     mqk_ref, gq_ref, gk_ref, gv_ref, gr_ref, mvt_ref, mot_ref, gt_ref, prev_s) = rest[1 if halo else 0:]
    j = pl.program_id(1)
    last = pl.num_programs(1) - 1
    tm = x_ref.shape[1]
    shift = mod_ref[0, 0:1]
    gain = g_ref[...] * (1.0 + mod_ref[0, 1:2])

    def norm_mod(x):
        return (x * lax.rsqrt(jnp.mean(x * x, -1, keepdims=True) + EPS) * gain + shift).astype(BF16)

    hb = norm_mod(x_ref[0])
    hext = jnp.concatenate([hb, norm_mod(xn_ref[0])], 0) if halo else hb

    @pl.when(j == 0)
    def _():
        prev_s[...] = jnp.zeros_like(prev_s)

    outs = dict(gq=gq_ref, gk=gk_ref, gv=gv_ref, gr=gr_ref, mv=mvt_ref, mo=mot_ref)
    tasks = []

    def tok_task(name, off, c0):
        def run():
            z = _dot(hb, wtok_ref[:, off + c0:off + c0 + CONV_CB])
            if name == "gq":
                z = z * (G_DK ** -0.5)
            outs[name][0, :, c0:c0 + CONV_CB] = z.astype(BF16)
        return run

    def feat_task():
        z = _dot_nt(wfeat_ref[...], hb)
        off = 0
        for name, size in _FEAT_SIZES:
            outs[name][0] = z[off:off + size].astype(BF16)
            off += size
        gt_ref[0] = z[off:] + bgt_ref[...]

    off = 2 * M_QK
    for name, size in _TOK_SIZES[1:]:
        tasks += [tok_task(name, off, c0) for c0 in range(0, size, CONV_CB)]
        off += size
    tasks.append(feat_task)

    nblk = 2 * M_QK // CONV_CB
    for i in range(nblk):
        cs = slice(i * CONV_CB, (i + 1) * CONV_CB)
        z = _dot(hext, wtok_ref[:, cs])
        for task in tasks[i * len(tasks) // nblk:(i + 1) * len(tasks) // nblk]:
            task()
        below = jnp.where(j < last, z[tm:], 0.0) if halo else jnp.zeros((cols, CONV_CB), F32)
        ext = jnp.concatenate([prev_s[:, cs], z[:tm], below], 0)
        prev_s[:, cs] = z[tm - cols:tm]
        mqk_ref[0, :, cs] = (_conv_silu(ext, cw_ref[:, cs], cols, tm // cols) * ksc_ref[:, cs]).astype(BF16)


def _features(x, mod, mod_row, consts, cols, tm):
    b, t, d = x.shape
    nt = t // tm
    per = tm // cols
    halo = nt > 1
    tok = lambda w: pl.BlockSpec((1, tm, w), lambda i, j: (i, j, 0))
    feat = lambda w: pl.BlockSpec((1, w, tm), lambda i, j: (i, 0, j))
    below = pl.BlockSpec((1, cols, d), lambda i, j: (i, jnp.minimum((j + 1) * per, t // cols - 1), 0))
    const = lambda a: pl.BlockSpec(a.shape, lambda i, j: (0,) * a.ndim)
    vec = pl.BlockSpec((1,) + mod.shape[1:], lambda i, j: (mod_row(i), 0, 0))
    out_shape = [jax.ShapeDtypeStruct((b, t, s), BF16) for _, s in _TOK_SIZES]
    out_specs = [tok(s) for _, s in _TOK_SIZES]
    out_shape += [jax.ShapeDtypeStruct((b, s, t), BF16) for _, s in _FEAT_SIZES]
    out_specs += [feat(s) for _, s in _FEAT_SIZES]
    out_shape += [jax.ShapeDtypeStruct((b, GT_ROWS, t), F32)]
    out_specs += [feat(GT_ROWS)]
    return pl.pallas_call(
        functools.partial(_feat_kernel, cols, halo),
        grid=(b, nt),
        in_specs=[tok(d)] + ([below] if halo else []) + [vec] + [const(a) for a in consts],
        out_specs=out_specs,
        out_shape=out_shape,
        scratch_shapes=[pltpu.VMEM((cols, 2 * M_QK), F32)],
        compiler_params=_params("parallel", "arbitrary"),
        name="features",
    )(*([x, x] if halo else [x]), mod, *consts)


M_AUG = M_DV + 16


def _mlstm_cumsum_matrix(reverse):
    L = M_CHUNK
    low = np.tril(np.ones((L, L), np.float32))
    tri = low.T if reverse else low
    return jnp.asarray(np.tile(np.concatenate([tri.T, np.ones((L, L), np.float32)], 1), (3, 1)), BF16)


def _mlstm_kernel(emit, nch, qf_ref, kf_ref, vf_ref, grf_ref, qb_ref, kb_ref, vb_ref, grb_ref,
                  rowf_ref, rowb_ref, c0_ref, m0_ref, *rest):
    if emit:
        hf_ref, hb_ref, cout_ref, mout_ref, c_s, m_s = rest
    else:
        cout_ref, mout_ref, c_s, m_s = rest
        hf_ref = hb_ref = None
    j = pl.program_id(1)
    L = M_CHUNK

    @pl.when(j == 0)
    def _():
        c_s[...] = c0_ref[0]
        m_s[...] = m0_ref[0]

    sid = lax.broadcasted_iota(jnp.int32, (L, L), 0)
    tid = lax.broadcasted_iota(jnp.int32, (L, L), 1)
    ones_rows = jnp.ones((M_AUG - M_DV, L), BF16)
    lane = lax.broadcasted_iota(jnp.int32, (M_HEADS, L), 1)

    refs = ((qf_ref, kf_ref, vf_ref, grf_ref, rowf_ref, hf_ref),
            (qb_ref, kb_ref, vb_ref, grb_ref, rowb_ref, hb_ref))
    masks = (sid <= tid, sid >= tid)

    def gate_stage(i):
        chains = []
        for d in range(2):
            q_ref, k_ref, v_ref, gr_ref, row_ref, h_ref = refs[d]
            c = i if d == 0 else nch - 1 - i
            tok = slice(c * L, (c + 1) * L)
            grow = gr_ref[0, :, tok]
            cum_ext = _dot(jnp.concatenate(_split3(_log_sigmoid(grow) * LOG2_E), 1), row_ref[...])
            grow = grow * LOG2_E
            i0 = 2 * M_HEADS * d
            cc_rows = grow[i0:i0 + M_HEADS] - cum_ext[i0 + M_HEADS:i0 + 2 * M_HEADS, :L]
            cmax = cc_rows
            step = 1
            while step < L:
                if d == 0:
                    moved = jnp.where(lane >= step, pltpu.roll(cmax, step, 1), NEG)
                else:
                    moved = jnp.where(lane < L - step, pltpu.roll(cmax, L - step, 1), NEG)
                cmax = jnp.maximum(cmax, moved)
                step *= 2
            cc_cols = jnp.transpose(jnp.concatenate([cc_rows, cc_rows], 0))
            for hd in range(M_HEADS):
                cf = i0 + M_HEADS + hd
                sl = slice(hd * M_DK, (hd + 1) * M_DK)
                chains.append(dict(
                    d=d, r=d * M_HEADS + hd, sl=sl, tok=tok,
                    q=q_ref[0, tok, sl], k=k_ref[0, tok, sl],
                    v_aug=jnp.concatenate([v_ref[0, sl, tok], ones_rows], 0),
                    br=cum_ext[cf:cf + 1, :L],
                    btot=cum_ext[cf:cf + 1, L:],
                    ir=grow[i0 + hd:i0 + hd + 1, :],
                    cmax=cmax[hd:hd + 1, :],
                    cc=cc_cols[:, hd:hd + 1]))
        return chains

    for chains in [gate_stage(i) for i in range(nch)]:
        for ch in chains:
            r = ch["r"]
            m = jnp.concatenate([m_s[r:r + 1, :]] * (L // LANE), 1)
            ct = c_s[r]
            ch["m"] = m
            if emit:
                ch["big"] = _dot_nt(jnp.concatenate([ch["k"], ct.astype(BF16)], 0), ch["q"])
            src = ch["btot"] + ch["ir"] - ch["br"]
            m_new = jnp.maximum(m + ch["btot"], jnp.max(src, -1, keepdims=True))
            ws = jnp.exp2(src - m_new).astype(BF16)
            wc = jnp.exp2((m + ch["btot"] - m_new)[:, :M_DK])
            c_s[r] = wc * ct + _dot(ch["v_aug"] * ws, ch["k"])
            m_s[r:r + 1, :] = m_new[:, :LANE]

        if emit:
            for ch in chains:
                m, br, big = ch["m"], ch["br"], ch["big"]
                mt = br + jnp.maximum(m, ch["cmax"])
                pt = jnp.where(masks[ch["d"]], big[:L] * jnp.exp2(ch["cc"] + (br - mt)), 0.0)
                num = jnp.exp2(m + br - mt) * big[L:] + _dot(ch["v_aug"], pt.astype(BF16))
                inv = 1.0 / jnp.maximum(jnp.abs(num[M_DV:M_DV + 1]), jnp.exp2(-mt))
                refs[ch["d"]][5][0, ch["sl"], ch["tok"]] = num[:M_DV] * inv

    @pl.when(j == pl.num_programs(1) - 1)
    def _():
        cout_ref[0] = c_s[...]
        mout_ref[0] = m_s[...]


def _mlstm_scan(mqk, mvt, gt, c0, m0, emit, tb):
    b, _, t = mvt.shape
    nb = t // tb
    consts = (_mlstm_cumsum_matrix(False), _mlstm_cumsum_matrix(True))
    fw = lambda w, cblk: pl.BlockSpec((1, tb, w), lambda i, j: (i, j, cblk))
    bw = lambda w, cblk: pl.BlockSpec((1, tb, w), lambda i, j: (i, nb - 1 - j, cblk))
    ftf = lambda w: pl.BlockSpec((1, w, tb), lambda i, j: (i, 0, j))
    ftb = lambda w: pl.BlockSpec((1, w, tb), lambda i, j: (i, 0, nb - 1 - j))
    const = lambda a: pl.BlockSpec(a.shape, lambda i, j: (0,) * a.ndim)
    cst = pl.BlockSpec((1, 2 * M_HEADS, M_AUG, M_DK), lambda i, j: (i, 0, 0, 0))
    mst = pl.BlockSpec((1, 2 * M_HEADS, LANE), lambda i, j: (i, 0, 0))
    in_specs = [fw(M_QK, 0), fw(M_QK, 1), ftf(M_WIDTH), ftf(N_GATES),
                bw(M_QK, 0), bw(M_QK, 1), ftb(M_WIDTH), ftb(N_GATES)]
    in_specs += [const(a) for a in consts] + [cst, mst]
    out_specs = [cst, mst]
    out_shape = [jax.ShapeDtypeStruct(c0.shape, F32), jax.ShapeDtypeStruct(m0.shape, F32)]
    if emit:
        out_specs = [ftf(M_WIDTH), ftb(M_WIDTH)] + out_specs
        out_shape = [jax.ShapeDtypeStruct((b, M_WIDTH, t), F32)] * 2 + out_shape
    return pl.pallas_call(
        functools.partial(_mlstm_kernel, emit, tb // M_CHUNK),
        grid=(b, nb),
        in_specs=in_specs,
        out_specs=out_specs,
        out_shape=out_shape,
        scratch_shapes=[pltpu.VMEM((2 * M_HEADS, M_AUG, M_DK), F32), pltpu.VMEM((2 * M_HEADS, LANE), F32)],
        compiler_params=_params("parallel", "arbitrary"),
        name="mlstm_scan" if emit else "mlstm_ctx_state",
    )(mqk, mqk, mvt, gt, mqk, mqk, mvt, gt, *consts, c0, m0)


G_LEVELS = (32, 16, 8, 4, 2, 1)


def _gla_select_matrices(reverse):
    L = G_CHUNK
    t = np.arange(L)
    u = np.arange(L)[None, :]
    tt = t[:, None]
    mats = [(u <= tt), (u > tt)]
    for h in G_LEVELS:
        start = (tt // (2 * h)) * (2 * h)
        second = tt >= start + h
        m = np.where(second, (u >= start + h) & (u <= tt), (u > tt) & (u < start + h))
        mats.append(m)
    m = np.concatenate([x.astype(np.float32) for x in mats], 0)
    if reverse:
        m = m.reshape(len(mats), L, L)[:, ::-1, ::-1].reshape(len(mats) * L, L)
    return jnp.asarray(np.concatenate([m] * G_SPLIT, 1), BF16)


def _gla_pair_masks(reverse):
    L = G_CHUNK
    s = np.arange(L)[:, None]
    t = np.arange(L)[None, :]
    masks = [s == t]
    for h in G_LEVELS:
        same = (s // (2 * h)) == (t // (2 * h))
        masks.append(same & ((t // h) % 2 == 1) & ((s // h) % 2 == 0))
    m = np.stack(masks).astype(np.float32)
    if reverse:
        m = m[:, ::-1, ::-1]
    return jnp.asarray(np.tile(m, (1, 1, G_HEADS)), F32)


def _gla_fast_matrices(reverse):
    L = G_CHUNK
    h = L // 2
    tt = np.arange(L)[:, None]
    u = np.arange(L)[None, :]
    mid = np.where(tt >= h, ((u >= h) & (u <= tt)).astype(np.float32), -((u > tt) & (u < h)).astype(np.float32))
    m = np.stack([(u <= tt).astype(np.float32), (u > tt).astype(np.float32), mid])
    if reverse:
        m = m[:, ::-1, ::-1]
    m = m.reshape(3 * L, L)
    return jnp.asarray(np.concatenate([m] * G_SPLIT, 1), BF16)


def _gla_causal_mask(reverse):
    s = np.arange(G_CHUNK)[:, None]
    t = np.arange(G_CHUNK)[None, :]
    m = (s >= t) if reverse else (s <= t)
    return jnp.asarray(np.tile(m.astype(np.float32), (1, G_HEADS)), F32)


def _gla_head_mask():
    r = np.arange(G_HEADS * G_CHUNK)[:, None] // G_CHUNK
    c = np.arange(G_QK)[None, :] // G_DK
    return jnp.asarray((r == c).astype(np.float32), BF16)


GLA_FAST_LIMIT = 40.0


def _gla_kernel(emit, nch, pipelined, qf_ref, kf_ref, vf_ref, lrf_ref, qb_ref, kb_ref, vb_ref, lrb_ref,
                lrnf_ref, lrnb_ref, lvf_ref, lvb_ref, fastf_ref, fastb_ref, mkf_ref, mkb_ref, cmf_ref, cmb_ref,
                hm_ref, gw_ref, gb_ref, s0_ref, *rest):
    if emit:
        of_ref, ob_ref, sout_ref, s_s, a3_s, lo_s = rest
    else:
        sout_ref, s_s, a3_s, lo_s = rest
        of_ref = ob_ref = None
    j = pl.program_id(1)
    L = G_CHUNK
    slot = j % 2

    def stage_decays(lr_refs, dst):
        lo = None
        for d, lr_ref in enumerate(lr_refs):
            lr_t = lr_ref[0].astype(BF16)
            loga = _log_sigmoid(_dot_tn(lr_t, gw_ref[d]) + gb_ref[d]) * (LOG2_E / G_TAU)
            pieces = _split(loga, G_SPLIT)
            a3_s[dst, d] = jnp.concatenate(
                [jnp.concatenate([p[c * L:(c + 1) * L] for p in pieces], 0) for c in range(nch)], 1)
            m = jnp.min(loga)
            lo = m if lo is None else jnp.minimum(lo, m)
        lo_s[dst] = lo * L

    @pl.when(j == 0)
    def _():
        s_s[...] = s0_ref[0]
        stage_decays((lrf_ref, lrb_ref), 0)

    hm = hm_ref[...]
    stack = lambda a: jnp.concatenate([a] * G_HEADS, 0)

    refs = ((qf_ref, kf_ref, vf_ref, mkf_ref, cmf_ref, of_ref), (qb_ref, kb_ref, vb_ref, mkb_ref, cmb_ref, ob_ref))

    def run(fast):
        chunks = []
        for d in range(2):
            q_ref, k_ref, v_ref, mk_ref, cm_ref, o_ref = refs[d]
            sel_ref = ((fastf_ref, fastb_ref) if fast else (lvf_ref, lvb_ref))[d]
            g = _dot(sel_ref[...], a3_s[slot, d])
            expo = jnp.exp2(g[:2 * L] if fast else g)
            for c in (range(nch) if d == 0 else reversed(range(nch))):
                rows = slice(c * L, (c + 1) * L)
                cols = slice(c * G_QK, (c + 1) * G_QK)
                chunks.append(dict(
                    d=d, rows=rows, cols=cols, g=g, expo=expo, k=k_ref[0, rows, :],
                    v_st=jnp.concatenate([v_ref[0, rows, hd * G_DV:(hd + 1) * G_DV] for hd in range(G_HEADS)], 0),
                    e_tot=expo[L - 1:L, cols] if d == 0 else expo[0:1, cols]))

        def e(ch, i):
            return ch["expo"][i * L:(i + 1) * L, ch["cols"]].astype(BF16)

        for ch in chunks:
            k = ch["k"]
            ch["kd_st"] = stack(k * e(ch, 1)) * hm
            if emit:
                q_ref, mk_ref, cm_ref = refs[ch["d"]][0], refs[ch["d"]][3], refs[ch["d"]][4]
                qs = stack(q_ref[0, ch["rows"], :]) * hm
                ch["qs"] = qs
                if fast:
                    gm = ch["g"][2 * L:3 * L, ch["cols"]]
                    att_t = _dot_nt(k * jnp.exp2(-gm).astype(BF16), qs * stack(jnp.exp2(gm).astype(BF16)))
                    att_t = jnp.where(cm_ref[...] != 0.0, att_t, 0.0)
                else:
                    att_t = mk_ref[0] * _dot_nt(k, qs)
                    for lv in range(len(G_LEVELS)):
                        ev = e(ch, 2 + lv)
                        att_t = att_t + mk_ref[1 + lv] * _dot_nt(k * ev, qs * stack(ev))
                ch["att_t"] = att_t

        if pipelined:
            stage_decays((lrnf_ref, lrnb_ref), 1 - slot)

        for ch in chunks:
            if emit:
                att_st = stack(ch["att_t"].astype(BF16)) * hm
                ch["res"] = _dot_tn(ch["v_st"], jnp.concatenate([att_st, ch["kd_st"]], 1))
            else:
                ch["res"] = _dot_tn(ch["v_st"], ch["kd_st"])

        order = [chunks[dd * nch + i] for i in range(nch) for dd in range(2)]
        for ch in order:
            d = ch["d"]
            st = s_s[d]
            res = ch["res"]
            if emit:
                inter_t = _dot_nt(st.astype(BF16), ch["qs"] * stack(e(ch, 0)))
                o = jnp.transpose(inter_t + res[:, :G_HEADS * L])
                o_ref = refs[d][5]
                for hd in range(G_HEADS):
                    o_ref[0, ch["rows"], hd * G_DV:(hd + 1) * G_DV] = o[hd * L:(hd + 1) * L]
                res = res[:, G_HEADS * L:]
            s_s[d] = ch["e_tot"] * st + res

    if emit:
        mild = lo_s[slot] >= -GLA_FAST_LIMIT
        pl.when(mild)(lambda: run(True))
        pl.when(jnp.logical_not(mild))(lambda: run(False))
    else:
        run(True)

    @pl.when(j == pl.num_programs(1) - 1)
    def _():
        sout_ref[0] = s_s[...]


def _gla_scan(gq, gk, gv, gt, g_gate_w, g_gate_b, s0, emit, tb):
    b, t, _ = gv.shape
    nb = t // tb
    consts = (_gla_select_matrices(False), _gla_select_matrices(True),
              _gla_fast_matrices(False), _gla_fast_matrices(True),
              _gla_pair_masks(False), _gla_pair_masks(True),
              _gla_causal_mask(False), _gla_causal_mask(True), _gla_head_mask(), g_gate_w, g_gate_b)
    fw = lambda w: pl.BlockSpec((1, tb, w), lambda i, j: (i, j, 0))
    bw = lambda w: pl.BlockSpec((1, tb, w), lambda i, j: (i, nb - 1 - j, 0))
    const = lambda a: pl.BlockSpec(a.shape, lambda i, j: (0,) * a.ndim)
    sst = pl.BlockSpec((1, 2, G_DV, G_QK), lambda i, j: (i, 0, 0, 0))
    assert N_GATES == G_RANK
    lr = lambda rows, at: pl.BlockSpec((1, G_RANK, tb), lambda i, j: (i, rows, at(j)))
    fw_lr, bw_lr = lr(1, lambda j: j), lr(2, lambda j: nb - 1 - j)
    fw_next = lr(1, lambda j: jnp.minimum(j + 1, nb - 1))
    bw_next = lr(2, lambda j: jnp.maximum(nb - 2 - j, 0))
    in_specs = [fw(G_QK), fw(G_QK), fw(G_WIDTH), fw_lr,
                bw(G_QK), bw(G_QK), bw(G_WIDTH), bw_lr, fw_next, bw_next]
    in_specs += [const(a) for a in consts] + [sst]
    out_specs = [sst]
    out_shape = [jax.ShapeDtypeStruct(s0.shape, F32)]
    if emit:
        out_specs = [fw(G_WIDTH), bw(G_WIDTH)] + out_specs
        out_shape = [jax.ShapeDtypeStruct((b, t, G_WIDTH), F32)] * 2 + out_shape
    nch = tb // G_CHUNK
    return pl.pallas_call(
        functools.partial(_gla_kernel, emit, nch, nb > 1),
        grid=(b, nb),
        in_specs=in_specs,
        out_specs=out_specs,
        out_shape=out_shape,
        scratch_shapes=[pltpu.VMEM((2, G_DV, G_QK), F32),
                        pltpu.VMEM((2, 2, G_SPLIT * G_CHUNK, nch * G_QK), BF16),
                        pltpu.SMEM((2,), F32)],
        compiler_params=_params("parallel", "arbitrary"),
        name="gla_scan" if emit else "gla_ctx_state",
    )(gq, gk, gv, gt, gq, gk, gv, gt, gt, gt, *consts, s0)


FFN_CHUNK = 256


def _head_norm(h, nheads, width):
    parts = []
    for hd in range(nheads):
        blk = h[:, hd * width:(hd + 1) * width]
        parts.append(blk * lax.rsqrt(jnp.mean(blk * blk, -1, keepdims=True) + EPS))
    return jnp.concatenate(parts, 1)


def _out_kernel(d_ff, x_ref, hf_ref, hb_ref, of_ref, ob_ref, mo_ref, gr_ref, mod_ref,
                mng_ref, gng_ref, gffn_ref, gfin_ref, wout_ref, wgu_ref, wdown_ref, o_ref):
    x = x_ref[0]
    mod = mod_ref[0][2:6]
    hm_t = hf_ref[0] + hb_ref[0]
    parts = []
    for hd in range(M_HEADS):
        blk = hm_t[hd * M_DV:(hd + 1) * M_DV]
        parts.append(blk * lax.rsqrt(jnp.mean(blk * blk, 0, keepdims=True) + EPS))
    hm_t = jnp.concatenate(parts, 0) * mng_ref[...] * _sigmoid(mo_ref[0].astype(F32))
    gr = gr_ref[0].astype(F32)
    hg = _head_norm(of_ref[0] + ob_ref[0], G_HEADS, G_DV) * gng_ref[...] * (gr * _sigmoid(gr))
    mixed = _dot_tn(hm_t.astype(BF16), wout_ref[:M_WIDTH, :]) + _dot(hg.astype(BF16), wout_ref[M_WIDTH:, :])
    x1 = x + mod[0:1] * mixed
    gain2 = gffn_ref[...] * (1.0 + mod[2:3])
    h2 = (x1 * lax.rsqrt(jnp.mean(x1 * x1, -1, keepdims=True) + EPS) * gain2 + mod[1:2]).astype(BF16)
    def gated(c0):
        a = _dot(h2, wgu_ref[:, c0:c0 + FFN_CHUNK])
        g = _dot(h2, wgu_ref[:, d_ff + c0:d_ff + c0 + FFN_CHUNK])
        return (a * _sigmoid(a) * g).astype(BF16)

    starts = list(range(0, d_ff, FFN_CHUNK))
    act = gated(starts[0])
    acc = None
    for i, c0 in enumerate(starts):
        nxt = gated(starts[i + 1]) if i + 1 < len(starts) else None
        part = _dot(act, wdown_ref[c0:c0 + FFN_CHUNK, :])
        acc = part if acc is None else acc + part
        act = nxt
    x2 = x1 + mod[3:4] * acc
    o_ref[0] = x2 * lax.rsqrt(jnp.mean(x2 * x2, -1, keepdims=True) + EPS) * gfin_ref[...]


def _out_block(x, hf, hb, of, ob, mo, gr, mod4, mng, gng, gffn, gfin, wout, wgu, wdown, tm):
    b, t, d = x.shape
    d_ff = wdown.shape[0]
    tok = lambda w: pl.BlockSpec((1, tm, w), lambda i, j: (i, j, 0))
    feat = lambda w: pl.BlockSpec((1, w, tm), lambda i, j: (i, 0, j))
    const = lambda a: pl.BlockSpec(a.shape, lambda i, j: (0,) * a.ndim, pipeline_mode=pl.Buffered(1))
    return pl.pallas_call(
        functools.partial(_out_kernel, d_ff),
        grid=(b, t // tm),
        in_specs=[tok(d), feat(M_WIDTH), feat(M_WIDTH), tok(G_WIDTH), tok(G_WIDTH), feat(M_WIDTH), tok(G_WIDTH),
                  pl.BlockSpec((1,) + mod4.shape[1:], lambda i, j: (i, 0, 0)),
                  const(mng), const(gng), const(gffn), const(gfin), const(wout), const(wgu), const(wdown)],
        out_specs=tok(d),
        out_shape=jax.ShapeDtypeStruct((b, t, d), F32),
        compiler_params=_params("parallel", "arbitrary"),
        name="out_ffn",
    )(x, hf, hb, of, ob, mo, gr, mod4, mng, gng, gffn, gfin, wout, wgu, wdown)


def _largest_tile(t, cap):
    tm = min(t, cap)
    while t % tm:
        tm //= 2
    return tm


def kernel(x, c, ctx, c_ctx, w_ada, b_ada, g_mix, w_in, conv_w, m_gate_b, m_norm_g, g_gate_w, g_gate_b,
           g_norm_g, w_out, g_ffn, w_gu, w_down, g_final):
    b, t, d = x.shape
    tc = ctx.shape[1]
    assert w_ada.shape[0] == 1, "single layer block"
    assert t % (GRID_W * 2) == 0 and tc % M_CHUNK == 0

    pad = (-(b + 1)) % 8
    c_all = jnp.concatenate([c, c_ctx[None, :], jnp.zeros((pad, d), F32)], 0)
    mod = _modulation(c_all, w_ada[0], b_ada[0]).reshape(b + 1 + pad, 6, d)

    w = w_in[0]
    sizes = (2 * M_QK, M_WIDTH, M_WIDTH, N_GATES, G_QK, G_QK, G_WIDTH, G_WIDTH, 2 * G_RANK)
    offs = np.concatenate([[0], np.cumsum(sizes)])
    seg = lambda i: w[:, offs[i]:offs[i + 1]]
    wtok = jnp.concatenate([seg(0), seg(4), seg(5), seg(6), seg(7)], 1).astype(BF16)
    wfeat = jnp.transpose(jnp.concatenate([seg(1), seg(2), seg(3), seg(8)], 1)).astype(BF16)
    bgt = jnp.concatenate([m_gate_b[0], jnp.zeros((2 * G_RANK,), F32)]).reshape(GT_ROWS, 1)
    g_mix2 = g_mix[0].reshape(1, d)
    conv_w9 = conv_w[0].reshape(9, 2 * M_QK)
    kscale = jnp.concatenate([jnp.ones((M_QK,), F32), jnp.full((M_QK,), M_DK ** -0.5, F32)]).reshape(1, 2 * M_QK)
    ggw = g_gate_w[0].astype(BF16)
    ggb = g_gate_b[0].reshape(2, 1, G_QK)

    feat_consts = (g_mix2, wtok, wfeat, bgt, conv_w9, kscale)

    def features(inp, mod_row, cols, tm):
        mqk, gq, gk, gv, gr, mvt, mot, gt = _features(inp, mod, mod_row, feat_consts, cols, tm)
        return dict(mqk=mqk, gq=gq, gk=gk, gv=gv, gr=gr, mvt=mvt, mot=mot, gt=gt)

    fc = features(ctx, lambda i: b, tc, tc)
    c0 = jnp.zeros((b, 2 * M_HEADS, M_AUG, M_DK), F32)
    m0 = jnp.full((b, 2 * M_HEADS, LANE), NEG, F32)
    c_ctx_state, m_ctx_state = _mlstm_scan(fc["mqk"], fc["mvt"], fc["gt"], c0, m0, False,
                                           _largest_tile(tc, M_CHUNK))
    s0 = jnp.zeros((b, 2, G_DV, G_QK), F32)
    (s_ctx_state,) = _gla_scan(fc["gq"], fc["gk"], fc["gv"], fc["gt"], ggw, ggb, s0, False,
                               _largest_tile(tc, 256))

    fx = features(x, lambda i: i, GRID_W, _largest_tile(t, 1024))
    hf_t, hb_t, _, _ = _mlstm_scan(fx["mqk"], fx["mvt"], fx["gt"], c_ctx_state, m_ctx_state, True,
                                   _largest_tile(t, 1024))
    of, ob, _ = _gla_scan(fx["gq"], fx["gk"], fx["gv"], fx["gt"], ggw, ggb, s_ctx_state, True,
                          _largest_tile(t, 512))

    tm = _largest_tile(t, 512)
    mng = jnp.broadcast_to(m_norm_g[0].reshape(M_WIDTH, 1), (M_WIDTH, tm))
    return _out_block(x, hf_t, hb_t, of, ob, fx["mot"], fx["gr"], mod,
                      mng, g_norm_g[0].reshape(1, -1),
                      g_ffn[0].reshape(1, d), g_final.reshape(1, d),
                      w_out[0].astype(BF16), w_gu[0].astype(BF16), w_down[0].astype(BF16), tm)
```

```python
import functools

import numpy as np
import jax
import jax.numpy as jnp
from jax import lax
from jax.experimental import pallas as pl
from jax.experimental.pallas import tpu as pltpu

F32 = jnp.float32
BF16 = jnp.bfloat16

GRID_W = 64
M_HEADS = 4
M_DK = 128
M_DV = 128
M_QK = M_HEADS * M_DK
M_WIDTH = M_HEADS * M_DV
M_CHUNK = 256
G_HEADS = 4
G_DK = 64
G_DV = 128
G_QK = G_HEADS * G_DK
G_WIDTH = G_HEADS * G_DV
G_RANK = 16
G_TAU = 16.0
G_CHUNK = 64
EPS = 1e-6
NEG = -1e30
LOG2_E = 1.4426950408889634

LANE = 128
VMEM_LIMIT = 56 * 1024 * 1024


def _params(*sem):
    return pltpu.CompilerParams(dimension_semantics=sem, vmem_limit_bytes=VMEM_LIMIT)


def _sigmoid(x):
    return 1.0 / (1.0 + jnp.exp2(x * -LOG2_E))


def _log_sigmoid(x):
    return jnp.minimum(x, 0.0) - jnp.log(1.0 + jnp.exp2(jnp.abs(x) * -LOG2_E))


def _split(x, n):
    pieces = []
    for _ in range(n - 1):
        p = x.astype(BF16)
        pieces.append(p)
        x = x - p.astype(F32)
    pieces.append(x.astype(BF16))
    return pieces


def _split3(x):
    return _split(x, 3)


G_SPLIT = 2


def _dot(a, b):
    return jnp.dot(a, b, preferred_element_type=F32)


def _dot_nt(a, b):
    return lax.dot_general(a, b, (((1,), (1,)), ((), ())), preferred_element_type=F32)


def _dot_tn(a, b):
    return lax.dot_general(a, b, (((0,), (0,)), ((), ())), preferred_element_type=F32)


def _mod_kernel(c_ref, w_ref, b_ref, o_ref):
    cs = c_ref[...]
    a = (cs * _sigmoid(cs)).astype(BF16)
    o_ref[...] = _dot(a, w_ref[...].astype(BF16)) + b_ref[...]


def _modulation(c_all, w_ada, b_ada):
    rows, d = c_all.shape
    n = w_ada.shape[1]
    bn = n // 6
    return pl.pallas_call(
        _mod_kernel,
        grid=(n // bn,),
        in_specs=[pl.BlockSpec((rows, d), lambda j: (0, 0)),
                  pl.BlockSpec((d, bn), lambda j: (0, j)),
                  pl.BlockSpec((1, bn), lambda j: (0, j))],
        out_specs=pl.BlockSpec((rows, bn), lambda j: (0, j)),
        out_shape=jax.ShapeDtypeStruct((rows, n), F32),
        compiler_params=_params("arbitrary"),
        name="adaln_mod",
    )(c_all, w_ada, b_ada.reshape(1, n))


_TOK_SIZES = (("mqk", 2 * M_QK), ("gq", G_QK), ("gk", G_QK), ("gv", G_WIDTH), ("gr", G_WIDTH))
_FEAT_SIZES = (("mv", M_WIDTH), ("mo", M_WIDTH))
N_GATES = 4 * M_HEADS
GT_ROWS = N_GATES + 2 * G_RANK


CONV_CB = 256


def _conv_silu(ext, w9, cols, nrow):
    tot = (nrow + 2) * cols
    col = lax.broadcasted_iota(jnp.int32, ext.shape, 0) % cols
    left = jnp.where(col == 0, 0.0, pltpu.roll(ext, 1, 0))
    right = jnp.where(col == cols - 1, 0.0, pltpu.roll(ext, tot - 1, 0))
    srcs = (left, ext, right)
    acc = None
    for di in range(3):
        for dj in range(3):
            term = srcs[dj][di * cols:(di + nrow) * cols] * w9[di * 3 + dj:di * 3 + dj + 1]
            acc = term if acc is None else acc + term
    return acc * _sigmoid(acc)


def _feat_kernel(cols, halo, x_ref, *rest):
    xn_ref = rest[0] if halo else None
    (mod_ref, g_ref, wtok_ref, wfeat_ref, bgt_ref, cw_ref,
     mqk_ref, gq_ref, gk_ref, gv_ref, gr_ref, mvt_ref, mot_ref, gt_ref, prev_s) = rest[1 if halo else 0:]
    j = pl.program_id(1)
    last = pl.num_programs(1) - 1
    tm = x_ref.shape[1]
    shift = mod_ref[0, 0:1]
    gain = g_ref[...] * (1.0 + mod_ref[0, 1:2])

    def norm_mod(x):
        return (x * lax.rsqrt(jnp.mean(x * x, -1, keepdims=True) + EPS) * gain + shift).astype(BF16)

    hb = norm_mod(x_ref[0])
    hext = jnp.concatenate([hb, norm_mod(xn_ref[0])], 0) if halo else hb

    @pl.when(j == 0)
    def _():
        prev_s[...] = jnp.zeros_like(prev_s)

    outs = dict(gq=gq_ref, gk=gk_ref, gv=gv_ref, gr=gr_ref, mv=mvt_ref, mo=mot_ref)
    tasks = []

    def tok_task(name, off, c0):
        def run():
            z = _dot(hb, wtok_ref[:, off + c0:off + c0 + CONV_CB])
            if name == "gq":
                z = z * (G_DK ** -0.5)
            outs[name][0, :, c0:c0 + CONV_CB] = z.astype(BF16)
        return run

    def feat_task():
        z = _dot_nt(wfeat_ref[...], hb)
        off = 0
        for name, size in _FEAT_SIZES:
            outs[name][0] = z[off:off + size].astype(BF16)
            off += size
        gt_ref[0] = z[off:] + bgt_ref[...]

    off = 2 * M_QK
    for name, size in _TOK_SIZES[1:]:
        tasks += [tok_task(name, off, c0) for c0 in range(0, size, CONV_CB)]
        off += size
    tasks.append(feat_task)

    nblk = 2 * M_QK // CONV_CB
    for i in range(nblk):
        cs = slice(i * CONV_CB, (i + 1) * CONV_CB)
        z = _dot(hext, wtok_ref[:, cs])
        for task in tasks[i * len(tasks) // nblk:(i + 1) * len(tasks) // nblk]:
            task()
        below = jnp.where(j < last, z[tm:], 0.0) if halo else jnp.zeros((cols, CONV_CB), F32)
        ext = jnp.concatenate([prev_s[:, cs], z[:tm], below], 0)
        prev_s[:, cs] = z[tm - cols:tm]
        act = _conv_silu(ext, cw_ref[:, cs], cols, tm // cols)
        if i * CONV_CB >= M_QK:
            act = act * (M_DK ** -0.5)
        mqk_ref[0, :, cs] = act.astype(BF16)


def _features(x, mod, mod_row, consts, cols, tm):
    b, t, d = x.shape
    nt = t // tm
    per = tm // cols
    halo = nt > 1
    tok = lambda w: pl.BlockSpec((1, tm, w), lambda i, j: (i, j, 0))
    feat = lambda w: pl.BlockSpec((1, w, tm), lambda i, j: (i, 0, j))
    below = pl.BlockSpec((1, cols, d), lambda i, j: (i, jnp.minimum((j + 1) * per, t // cols - 1), 0))
    const = lambda a: pl.BlockSpec(a.shape, lambda i, j: (0,) * a.ndim)
    vec = pl.BlockSpec((1,) + mod.shape[1:], lambda i, j: (mod_row(i), 0, 0))
    out_shape = [jax.ShapeDtypeStruct((b, t, s), BF16) for _, s in _TOK_SIZES]
    out_specs = [tok(s) for _, s in _TOK_SIZES]
    out_shape += [jax.ShapeDtypeStruct((b, s, t), BF16) for _, s in _FEAT_SIZES]
    out_specs += [feat(s) for _, s in _FEAT_SIZES]
    out_shape += [jax.ShapeDtypeStruct((b, GT_ROWS, t), F32)]
    out_specs += [feat(GT_ROWS)]
    return pl.pallas_call(
        functools.partial(_feat_kernel, cols, halo),
        grid=(b, nt),
        in_specs=[tok(d)] + ([below] if halo else []) + [vec] + [const(a) for a in consts],
        out_specs=out_specs,
        out_shape=out_shape,
        scratch_shapes=[pltpu.VMEM((cols, 2 * M_QK), F32)],
        compiler_params=_params("parallel", "arbitrary"),
        name="features",
    )(*([x, x] if halo else [x]), mod, *consts)


M_AUG = M_DV + 16


def _mlstm_cumsum_matrix(reverse):
    L = M_CHUNK
    low = np.tril(np.ones((L, L), np.float32))
    tri = low.T if reverse else low
    return jnp.asarray(np.tile(np.concatenate([tri.T, np.ones((L, L), np.float32)], 1), (3, 1)), BF16)


def _mlstm_kernel(emit, nch, qf_ref, kf_ref, vf_ref, grf_ref, qb_ref, kb_ref, vb_ref, grb_ref,
                  rowf_ref, rowb_ref, c0_ref, m0_ref, *rest):
    if emit:
        hf_ref, hb_ref, cout_ref, mout_ref, c_s, m_s = rest
    else:
        cout_ref, mout_ref, c_s, m_s = rest
        hf_ref = hb_ref = None
    j = pl.program_id(1)
    L = M_CHUNK

    @pl.when(j == 0)
    def _():
        c_s[...] = c0_ref[0]
        m_s[...] = m0_ref[0]

    sid = lax.broadcasted_iota(jnp.int32, (L, L), 0)
    tid = lax.broadcasted_iota(jnp.int32, (L, L), 1)
    ones_rows = jnp.ones((M_AUG - M_DV, L), BF16)
    lane = lax.broadcasted_iota(jnp.int32, (M_HEADS, L), 1)

    refs = ((qf_ref, kf_ref, vf_ref, grf_ref, rowf_ref, hf_ref),
            (qb_ref, kb_ref, vb_ref, grb_ref, rowb_ref, hb_ref))
    masks = (sid <= tid, sid >= tid)

    def gate_stage(i):
        chains = []
        for d in range(2):
            q_ref, k_ref, v_ref, gr_ref, row_ref, h_ref = refs[d]
            c = i if d == 0 else nch - 1 - i
            tok = slice(c * L, (c + 1) * L)
            grow = gr_ref[0, :, tok]
            cum_ext = _dot(jnp.concatenate(_split3(_log_sigmoid(grow) * LOG2_E), 1), row_ref[...])
            grow = grow * LOG2_E
            i0 = 2 * M_HEADS * d
            cc_rows = grow[i0:i0 + M_HEADS] - cum_ext[i0 + M_HEADS:i0 + 2 * M_HEADS, :L]
            cmax = cc_rows
            step = 1
            while step < L:
                if d == 0:
                    moved = jnp.where(lane >= step, pltpu.roll(cmax, step, 1), NEG)
                else:
                    moved = jnp.where(lane < L - step, pltpu.roll(cmax, L - step, 1), NEG)
                cmax = jnp.maximum(cmax, moved)
                step *= 2
            cc_cols = jnp.transpose(jnp.concatenate([cc_rows, cc_rows], 0))
            for hd in range(M_HEADS):
                cf = i0 + M_HEADS + hd
                sl = slice(hd * M_DK, (hd + 1) * M_DK)
                chains.append(dict(
                    d=d, r=d * M_HEADS + hd, sl=sl, tok=tok,
                    q=q_ref[0, tok, sl], k=k_ref[0, tok, sl],
                    v_aug=jnp.concatenate([v_ref[0, sl, tok], ones_rows], 0),
                    br=cum_ext[cf:cf + 1, :L],
                    btot=cum_ext[cf:cf + 1, L:],
                    ir=grow[i0 + hd:i0 + hd + 1, :],
                    cmax=cmax[hd:hd + 1, :],
                    cc=cc_cols[:, hd:hd + 1]))
        return chains

    for chains in [gate_stage(i) for i in range(nch)]:
        for ch in chains:
            r = ch["r"]
            m = jnp.concatenate([m_s[r:r + 1, :]] * (L // LANE), 1)
            ct = c_s[r]
            ch["m"] = m
            if emit:
                ch["big"] = _dot_nt(jnp.concatenate([ch["k"], ct.astype(BF16)], 0), ch["q"])
            src = ch["btot"] + ch["ir"] - ch["br"]
            m_new = jnp.maximum(m + ch["btot"], jnp.max(src, -1, keepdims=True))
            ws = jnp.exp2(src - m_new).astype(BF16)
            wc = jnp.exp2((m + ch["btot"] - m_new)[:, :M_DK])
            c_s[r] = wc * ct + _dot(ch["v_aug"] * ws, ch["k"])
            m_s[r:r + 1, :] = m_new[:, :LANE]

        if emit:
            for ch in chains:
                m, br, big = ch["m"], ch["br"], ch["big"]
                mt = br + jnp.maximum(m, ch["cmax"])
                pt = jnp.where(masks[ch["d"]], big[:L] * jnp.exp2(ch["cc"] + (br - mt)), 0.0)
                num = jnp.exp2(m + br - mt) * big[L:] + _dot(ch["v_aug"], pt.astype(BF16))
                inv = 1.0 / jnp.maximum(jnp.abs(num[M_DV:M_DV + 1]), jnp.exp2(-mt))
                refs[ch["d"]][5][0, ch["sl"], ch["tok"]] = num[:M_DV] * inv

    @pl.when(j == pl.num_programs(1) - 1)
    def _():
        cout_ref[0] = c_s[...]
        mout_ref[0] = m_s[...]


def _mlstm_scan(mqk, mvt, gt, c0, m0, emit, tb):
    b, _, t = mvt.shape
    nb = t // tb
    consts = (_mlstm_cumsum_matrix(False), _mlstm_cumsum_matrix(True))
    fw = lambda w, cblk: pl.BlockSpec((1, tb, w), lambda i, j: (i, j, cblk))
    bw = lambda w, cblk: pl.BlockSpec((1, tb, w), lambda i, j: (i, nb - 1 - j, cblk))
    ftf = lambda w: pl.BlockSpec((1, w, tb), lambda i, j: (i, 0, j))
    ftb = lambda w: pl.BlockSpec((1, w, tb), lambda i, j: (i, 0, nb - 1 - j))
    const = lambda a: pl.BlockSpec(a.shape, lambda i, j: (0,) * a.ndim)
    cst = pl.BlockSpec((1, 2 * M_HEADS, M_AUG, M_DK), lambda i, j: (i, 0, 0, 0))
    mst = pl.BlockSpec((1, 2 * M_HEADS, LANE), lambda i, j: (i, 0, 0))
    in_specs = [fw(M_QK, 0), fw(M_QK, 1), ftf(M_WIDTH), ftf(N_GATES),
                bw(M_QK, 0), bw(M_QK, 1), ftb(M_WIDTH), ftb(N_GATES)]
    in_specs += [const(a) for a in consts] + [cst, mst]
    out_specs = [cst, mst]
    out_shape = [jax.ShapeDtypeStruct(c0.shape, F32), jax.ShapeDtypeStruct(m0.shape, F32)]
    if emit:
        out_specs = [ftf(M_WIDTH), ftb(M_WIDTH)] + out_specs
        out_shape = [jax.ShapeDtypeStruct((b, M_WIDTH, t), F32)] * 2 + out_shape
    return pl.pallas_call(
        functools.partial(_mlstm_kernel, emit, tb // M_CHUNK),
        grid=(b, nb),
        in_specs=in_specs,
        out_specs=out_specs,
        out_shape=out_shape,
        scratch_shapes=[pltpu.VMEM((2 * M_HEADS, M_AUG, M_DK), F32), pltpu.VMEM((2 * M_HEADS, LANE), F32)],
        compiler_params=_params("parallel", "arbitrary"),
        name="mlstm_scan" if emit else "mlstm_ctx_state",
    )(mqk, mqk, mvt, gt, mqk, mqk, mvt, gt, *consts, c0, m0)


G_LEVELS = (32, 16, 8, 4, 2, 1)


def _gla_select_matrices(reverse):
    L = G_CHUNK
    t = np.arange(L)
    u = np.arange(L)[None, :]
    tt = t[:, None]
    mats = [(u <= tt), (u > tt)]
    for h in G_LEVELS:
        start = (tt // (2 * h)) * (2 * h)
        second = tt >= start + h
        m = np.where(second, (u >= start + h) & (u <= tt), (u > tt) & (u < start + h))
        mats.append(m)
    m = np.concatenate([x.astype(np.float32) for x in mats], 0)
    if reverse:
        m = m.reshape(len(mats), L, L)[:, ::-1, ::-1].reshape(len(mats) * L, L)
    return jnp.asarray(np.concatenate([m] * G_SPLIT, 1), BF16)


def _gla_pair_masks(reverse):
    L = G_CHUNK
    s = np.arange(L)[:, None]
    t = np.arange(L)[None, :]
    masks = [s == t]
    for h in G_LEVELS:
        same = (s // (2 * h)) == (t // (2 * h))
        masks.append(same & ((t // h) % 2 == 1) & ((s // h) % 2 == 0))
    m = np.stack(masks).astype(np.float32)
    if reverse:
        m = m[:, ::-1, ::-1]
    return jnp.asarray(np.tile(m, (1, 1, G_HEADS)), F32)


def _gla_fast_matrices(reverse):
    L = G_CHUNK
    h = L // 2
    tt = np.arange(L)[:, None]
    u = np.arange(L)[None, :]
    mid = np.where(tt >= h, ((u >= h) & (u <= tt)).astype(np.float32), -((u > tt) & (u < h)).astype(np.float32))
    m = np.stack([(u <= tt).astype(np.float32), (u > tt).astype(np.float32), mid])
    if reverse:
        m = m[:, ::-1, ::-1]
    m = m.reshape(3 * L, L)
    return jnp.asarray(np.concatenate([m] * G_SPLIT, 1), BF16)


def _gla_causal_mask(reverse):
    s = np.arange(G_CHUNK)[:, None]
    t = np.arange(G_CHUNK)[None, :]
    m = (s >= t) if reverse else (s <= t)
    return jnp.asarray(np.tile(m.astype(np.float32), (1, G_HEADS)), F32)


def _gla_head_mask():
    r = np.arange(G_HEADS * G_CHUNK)[:, None] // G_CHUNK
    c = np.arange(G_QK)[None, :] // G_DK
    return jnp.asarray((r == c).astype(np.float32), BF16)


GLA_FAST_LIMIT = 40.0


def _gla_kernel(emit, nch, pipelined, qf_ref, kf_ref, vf_ref, lrf_ref, qb_ref, kb_ref, vb_ref, lrb_ref,
                lrnf_ref, lrnb_ref, lvf_ref, lvb_ref, fastf_ref, fastb_ref, mkf_ref, mkb_ref, cmf_ref, cmb_ref,
                hm_ref, gw_ref, gb_ref, s0_ref, *rest):
    if emit:
        of_ref, ob_ref, sout_ref, s_s, a3_s, lo_s = rest
    else:
        sout_ref, s_s, a3_s, lo_s = rest
        of_ref = ob_ref = None
    j = pl.program_id(1)
    L = G_CHUNK
    slot = j % 2

    def stage_decays(lr_refs, dst):
        lo = None
        for d, lr_ref in enumerate(lr_refs):
            lr_t = lr_ref[0].astype(BF16)
            loga = _log_sigmoid(_dot_tn(lr_t, gw_ref[d]) + gb_ref[d]) * (LOG2_E / G_TAU)
            pieces = _split(loga, G_SPLIT)
            a3_s[dst, d] = jnp.concatenate(
                [jnp.concatenate([p[c * L:(c + 1) * L] for p in pieces], 0) for c in range(nch)], 1)
            m = jnp.min(loga)
            lo = m if lo is None else jnp.minimum(lo, m)
        lo_s[dst] = lo * L

    @pl.when(j == 0)
    def _():
        s_s[...] = s0_ref[0]
        stage_decays((lrf_ref, lrb_ref), 0)

    hm = hm_ref[...]
    stack = lambda a: jnp.concatenate([a] * G_HEADS, 0)

    refs = ((qf_ref, kf_ref, vf_ref, mkf_ref, cmf_ref, of_ref), (qb_ref, kb_ref, vb_ref, mkb_ref, cmb_ref, ob_ref))

    def run(fast):
        chunks = []
        for d in range(2):
            q_ref, k_ref, v_ref, mk_ref, cm_ref, o_ref = refs[d]
            sel_ref = ((fastf_ref, fastb_ref) if fast else (lvf_ref, lvb_ref))[d]
            g = _dot(sel_ref[...], a3_s[slot, d])
            expo = jnp.exp2(g[:2 * L] if fast else g)
            for c in (range(nch) if d == 0 else reversed(range(nch))):
                rows = slice(c * L, (c + 1) * L)
                cols = slice(c * G_QK, (c + 1) * G_QK)
                chunks.append(dict(
                    d=d, rows=rows, cols=cols, g=g, expo=expo, k=k_ref[0, rows, :],
                    v_st=jnp.concatenate([v_ref[0, rows, hd * G_DV:(hd + 1) * G_DV] for hd in range(G_HEADS)], 0),
                    e_tot=expo[L - 1:L, cols] if d == 0 else expo[0:1, cols]))

        def e(ch, i):
            return ch["expo"][i * L:(i + 1) * L, ch["cols"]].astype(BF16)

        for ch in chunks:
            k = ch["k"]
            ch["kd_st"] = stack(k * e(ch, 1)) * hm
            if emit:
                q_ref, mk_ref, cm_ref = refs[ch["d"]][0], refs[ch["d"]][3], refs[ch["d"]][4]
                qs = stack(q_ref[0, ch["rows"], :]) * hm
                ch["qs"] = qs
                if fast:
                    gm = ch["g"][2 * L:3 * L, ch["cols"]]
                    att_t = _dot_nt(k * jnp.exp2(-gm).astype(BF16), qs * stack(jnp.exp2(gm).astype(BF16)))
                    att_t = jnp.where(cm_ref[...] != 0.0, att_t, 0.0)
                else:
                    att_t = mk_ref[0] * _dot_nt(k, qs)
                    for lv in range(len(G_LEVELS)):
                        ev = e(ch, 2 + lv)
                        att_t = att_t + mk_ref[1 + lv] * _dot_nt(k * ev, qs * stack(ev))
                ch["att_t"] = att_t

        if pipelined:
            stage_decays((lrnf_ref, lrnb_ref), 1 - slot)

        for ch in chunks:
            if emit:
                att_st = stack(ch["att_t"].astype(BF16)) * hm
                ch["res"] = _dot_tn(ch["v_st"], jnp.concatenate([att_st, ch["kd_st"]], 1))
            else:
                ch["res"] = _dot_tn(ch["v_st"], ch["kd_st"])

        order = [chunks[dd * nch + i] for i in range(nch) for dd in range(2)]
        for ch in order:
            d = ch["d"]
            st = s_s[d]
            res = ch["res"]
            if emit:
                inter_t = _dot_nt(st.astype(BF16), ch["qs"] * stack(e(ch, 0)))
                o = jnp.transpose(inter_t + res[:, :G_HEADS * L])
                o_ref = refs[d][5]
                for hd in range(G_HEADS):
                    o_ref[0, ch["rows"], hd * G_DV:(hd + 1) * G_DV] = o[hd * L:(hd + 1) * L]
                res = res[:, G_HEADS * L:]
            s_s[d] = ch["e_tot"] * st + res

    if emit:
        mild = lo_s[slot] >= -GLA_FAST_LIMIT
        pl.when(mild)(lambda: run(True))
        pl.when(jnp.logical_not(mild))(lambda: run(False))
    else:
        run(True)

    @pl.when(j == pl.num_programs(1) - 1)
    def _():
        sout_ref[0] = s_s[...]


def _gla_scan(gq, gk, gv, gt, g_gate_w, g_gate_b, s0, emit, tb):
    b, t, _ = gv.shape
    nb = t // tb
    consts = (_gla_select_matrices(False), _gla_select_matrices(True),
              _gla_fast_matrices(False), _gla_fast_matrices(True),
              _gla_pair_masks(False), _gla_pair_masks(True),
              _gla_causal_mask(False), _gla_causal_mask(True), _gla_head_mask(), g_gate_w, g_gate_b)
    fw = lambda w: pl.BlockSpec((1, tb, w), lambda i, j: (i, j, 0))
    bw = lambda w: pl.BlockSpec((1, tb, w), lambda i, j: (i, nb - 1 - j, 0))
    const = lambda a: pl.BlockSpec(a.shape, lambda i, j: (0,) * a.ndim)
    sst = pl.BlockSpec((1, 2, G_DV, G_QK), lambda i, j: (i, 0, 0, 0))
    assert N_GATES == G_RANK
    lr = lambda rows, at: pl.BlockSpec((1, G_RANK, tb), lambda i, j: (i, rows, at(j)))
    fw_lr, bw_lr = lr(1, lambda j: j), lr(2, lambda j: nb - 1 - j)
    fw_next = lr(1, lambda j: jnp.minimum(j + 1, nb - 1))
    bw_next = lr(2, lambda j: jnp.maximum(nb - 2 - j, 0))
    in_specs = [fw(G_QK), fw(G_QK), fw(G_WIDTH), fw_lr,
                bw(G_QK), bw(G_QK), bw(G_WIDTH), bw_lr, fw_next, bw_next]
    in_specs += [const(a) for a in consts] + [sst]
    out_specs = [sst]
    out_shape = [jax.ShapeDtypeStruct(s0.shape, F32)]
    if emit:
        out_specs = [fw(G_WIDTH), bw(G_WIDTH)] + out_specs
        out_shape = [jax.ShapeDtypeStruct((b, t, G_WIDTH), F32)] * 2 + out_shape
    nch = tb // G_CHUNK
    return pl.pallas_call(
        functools.partial(_gla_kernel, emit, nch, nb > 1),
        grid=(b, nb),
        in_specs=in_specs,
        out_specs=out_specs,
        out_shape=out_shape,
        scratch_shapes=[pltpu.VMEM((2, G_DV, G_QK), F32),
                        pltpu.VMEM((2, 2, G_SPLIT * G_CHUNK, nch * G_QK), BF16),
                        pltpu.SMEM((2,), F32)],
        compiler_params=_params("parallel", "arbitrary"),
        name="gla_scan" if emit else "gla_ctx_state",
    )(gq, gk, gv, gt, gq, gk, gv, gt, gt, gt, *consts, s0)


FFN_CHUNK = 256


def _head_norm(h, nheads, width):
    parts = []
    for hd in range(nheads):
        blk = h[:, hd * width:(hd + 1) * width]
        parts.append(blk * lax.rsqrt(jnp.mean(blk * blk, -1, keepdims=True) + EPS))
    return jnp.concatenate(parts, 1)


def _out_kernel(d_ff, x_ref, hf_ref, hb_ref, of_ref, ob_ref, mo_ref, gr_ref, mod_ref,
                mng_ref, gng_ref, gffn_ref, gfin_ref, wout_ref, wgu_ref, wdown_ref, o_ref):
    x = x_ref[0]
    mod = mod_ref[0][2:6]
    hm_t = hf_ref[0] + hb_ref[0]
    parts = []
    for hd in range(M_HEADS):
        blk = hm_t[hd * M_DV:(hd + 1) * M_DV]
        parts.append(blk * lax.rsqrt(jnp.mean(blk * blk, 0, keepdims=True) + EPS))
    hm_t = jnp.concatenate(parts, 0) * mng_ref[...] * _sigmoid(mo_ref[0].astype(F32))
    gr = gr_ref[0].astype(F32)
    hg = _head_norm(of_ref[0] + ob_ref[0], G_HEADS, G_DV) * gng_ref[...] * (gr * _sigmoid(gr))
    mixed = _dot_tn(hm_t.astype(BF16), wout_ref[:M_WIDTH, :]) + _dot(hg.astype(BF16), wout_ref[M_WIDTH:, :])
    x1 = x + mod[0:1] * mixed
    gain2 = gffn_ref[...] * (1.0 + mod[2:3])
    h2 = (x1 * lax.rsqrt(jnp.mean(x1 * x1, -1, keepdims=True) + EPS) * gain2 + mod[1:2]).astype(BF16)
    def gated(c0):
        a = _dot(h2, wgu_ref[:, c0:c0 + FFN_CHUNK])
        g = _dot(h2, wgu_ref[:, d_ff + c0:d_ff + c0 + FFN_CHUNK])
        return (a * _sigmoid(a) * g).astype(BF16)

    starts = list(range(0, d_ff, FFN_CHUNK))
    act = gated(starts[0])
    acc = None
    for i, c0 in enumerate(starts):
        nxt = gated(starts[i + 1]) if i + 1 < len(starts) else None
        part = _dot(act, wdown_ref[c0:c0 + FFN_CHUNK, :])
        acc = part if acc is None else acc + part
        act = nxt
    x2 = x1 + mod[3:4] * acc
    o_ref[0] = x2 * lax.rsqrt(jnp.mean(x2 * x2, -1, keepdims=True) + EPS) * gfin_ref[...]


def _out_block(x, hf, hb, of, ob, mo, gr, mod4, mng, gng, gffn, gfin, wout, wgu, wdown, tm):
    b, t, d = x.shape
    d_ff = wdown.shape[0]
    tok = lambda w: pl.BlockSpec((1, tm, w), lambda i, j: (i, j, 0))
    feat = lambda w: pl.BlockSpec((1, w, tm), lambda i, j: (i, 0, j))
    const = lambda a: pl.BlockSpec(a.shape, lambda i, j: (0,) * a.ndim, pipeline_mode=pl.Buffered(1))
    return pl.pallas_call(
        functools.partial(_out_kernel, d_ff),
        grid=(b, t // tm),
        in_specs=[tok(d), feat(M_WIDTH), feat(M_WIDTH), tok(G_WIDTH), tok(G_WIDTH), feat(M_WIDTH), tok(G_WIDTH),
                  pl.BlockSpec((1,) + mod4.shape[1:], lambda i, j: (i, 0, 0)),
                  const(mng), const(gng), const(gffn), const(gfin), const(wout), const(wgu), const(wdown)],
        out_specs=tok(d),
        out_shape=jax.ShapeDtypeStruct((b, t, d), F32),
        compiler_params=_params("parallel", "arbitrary"),
        name="out_ffn",
    )(x, hf, hb, of, ob, mo, gr, mod4, mng, gng, gffn, gfin, wout, wgu, wdown)


def _largest_tile(t, cap):
    tm = min(t, cap)
    while t % tm:
        tm //= 2
    return tm


def kernel(x, c, ctx, c_ctx, w_ada, b_ada, g_mix, w_in, conv_w, m_gate_b, m_norm_g, g_gate_w, g_gate_b,
           g_norm_g, w_out, g_ffn, w_gu, w_down, g_final):
    b, t, d = x.shape
    tc = ctx.shape[1]
    assert w_ada.shape[0] == 1, "single layer block"
    assert t % (GRID_W * 2) == 0 and tc % M_CHUNK == 0

    pad = (-(b + 1)) % 8
    c_all = jnp.concatenate([c, c_ctx[None, :], jnp.zeros((pad, d), F32)], 0)
    mod = _modulation(c_all, w_ada[0], b_ada[0]).reshape(b + 1 + pad, 6, d)

    w = w_in[0]
    sizes = (2 * M_QK, M_WIDTH, M_WIDTH, N_GATES, G_QK, G_QK, G_WIDTH, G_WIDTH, 2 * G_RANK)
    offs = np.concatenate([[0], np.cumsum(sizes)])
    seg = lambda i: w[:, offs[i]:offs[i + 1]]
    wtok = jnp.concatenate([seg(0), seg(4), seg(5), seg(6), seg(7)], 1).astype(BF16)
    wfeat = jnp.transpose(jnp.concatenate([seg(1), seg(2), seg(3), seg(8)], 1)).astype(BF16)
    bgt = jnp.concatenate([m_gate_b[0], jnp.zeros((2 * G_RANK,), F32)]).reshape(GT_ROWS, 1)
    g_mix2 = g_mix[0].reshape(1, d)
    conv_w9 = conv_w[0].reshape(9, 2 * M_QK)
    ggw = g_gate_w[0].astype(BF16)
    ggb = g_gate_b[0].reshape(2, 1, G_QK)

    feat_consts = (g_mix2, wtok, wfeat, bgt, conv_w9)

    def features(inp, mod_row, cols, tm):
        mqk, gq, gk, gv, gr, mvt, mot, gt = _features(inp, mod, mod_row, feat_consts, cols, tm)
        return dict(mqk=mqk, gq=gq, gk=gk, gv=gv, gr=gr, mvt=mvt, mot=mot, gt=gt)

    fc = features(ctx, lambda i: b, tc, tc)
    c0 = jnp.zeros((b, 2 * M_HEADS, M_AUG, M_DK), F32)
    m0 = jnp.full((b, 2 * M_HEADS, LANE), NEG, F32)
    c_ctx_state, m_ctx_state = _mlstm_scan(fc["mqk"], fc["mvt"], fc["gt"], c0, m0, False,
                                           _largest_tile(tc, M_CHUNK))
    s0 = jnp.zeros((b, 2, G_DV, G_QK), F32)
    (s_ctx_state,) = _gla_scan(fc["gq"], fc["gk"], fc["gv"], fc["gt"], ggw, ggb, s0, False,
                               _largest_tile(tc, 256))

    fx = features(x, lambda i: i, GRID_W, _largest_tile(t, 1024))
    hf_t, hb_t, _, _ = _mlstm_scan(fx["mqk"], fx["mvt"], fx["gt"], c_ctx_state, m_ctx_state, True,
                                   _largest_tile(t, 1024))
    of, ob, _ = _gla_scan(fx["gq"], fx["gk"], fx["gv"], fx["gt"], ggw, ggb, s_ctx_state, True,
                          _largest_tile(t, 512))

    tm = _largest_tile(t, 512)
    mng = jnp.broadcast_to(m_norm_g[0].reshape(M_WIDTH, 1), (M_WIDTH, tm))
    return _out_block(x, hf_t, hb_t, of, ob, fx["mot"], fx["gr"], mod,
                      mng, g_norm_g[0].reshape(1, -1),
                      g_ffn[0].reshape(1, d), g_final.reshape(1, d),
                      w_out[0].astype(BF16), w_gu[0].astype(BF16), w_down[0].astype(BF16), tm)
```

```python
import functools

import numpy as np
import jax
import jax.numpy as jnp
from jax import lax
from jax.experimental import pallas as pl
from jax.experimental.pallas import tpu as pltpu

F32 = jnp.float32
BF16 = jnp.bfloat16

GRID_W = 64
M_HEADS = 4
M_DK = 128
M_DV = 128
M_QK = M_HEADS * M_DK
M_WIDTH = M_HEADS * M_DV
M_CHUNK = 256
G_HEADS = 4
G_DK = 64
G_DV = 128
G_QK = G_HEADS * G_DK
G_WIDTH = G_HEADS * G_DV
G_RANK = 16
G_TAU = 16.0
G_CHUNK = 64
EPS = 1e-6
NEG = -1e30
LOG2_E = 1.4426950408889634

LANE = 128
VMEM_LIMIT = 56 * 1024 * 1024


def _params(*sem):
    return pltpu.CompilerParams(dimension_semantics=sem, vmem_limit_bytes=VMEM_LIMIT)


def _sigmoid(x):
    return 1.0 / (1.0 + jnp.exp2(x * -LOG2_E))


def _log_sigmoid(x):
    return jnp.minimum(x, 0.0) - jnp.log(1.0 + jnp.exp2(jnp.abs(x) * -LOG2_E))


def _split(x, n):
    pieces = []
    for _ in range(n - 1):
        p = x.astype(BF16)
        pieces.append(p)
        x = x - p.astype(F32)
    pieces.append(x.astype(BF16))
    return pieces


def _split3(x):
    return _split(x, 3)


G_SPLIT = 2


def _dot(a, b):
    return jnp.dot(a, b, preferred_element_type=F32)


def _dot_nt(a, b):
    return lax.dot_general(a, b, (((1,), (1,)), ((), ())), preferred_element_type=F32)


def _dot_tn(a, b):
    return lax.dot_general(a, b, (((0,), (0,)), ((), ())), preferred_element_type=F32)


def _mod_kernel(c_ref, w_ref, b_ref, o_ref):
    cs = c_ref[...]
    a = (cs * _sigmoid(cs)).astype(BF16)
    o_ref[...] = _dot(a, w_ref[...].astype(BF16)) + b_ref[...]


def _modulation(c_all, w_ada, b_ada):
    rows, d = c_all.shape
    n = w_ada.shape[1]
    bn = n // 6
    return pl.pallas_call(
        _mod_kernel,
        grid=(n // bn,),
        in_specs=[pl.BlockSpec((rows, d), lambda j: (0, 0)),
                  pl.BlockSpec((d, bn), lambda j: (0, j)),
                  pl.BlockSpec((1, bn), lambda j: (0, j))],
        out_specs=pl.BlockSpec((rows, bn), lambda j: (0, j)),
        out_shape=jax.ShapeDtypeStruct((rows, n), F32),
        compiler_params=_params("arbitrary"),
        name="adaln_mod",
    )(c_all, w_ada, b_ada.reshape(1, n))


_TOK_SIZES = (("mqk", 2 * M_QK), ("gq", G_QK), ("gk", G_QK), ("gv", G_WIDTH), ("gr", G_WIDTH))
_FEAT_SIZES = (("mv", M_WIDTH), ("mo", M_WIDTH))
N_GATES = 4 * M_HEADS
GT_ROWS = N_GATES + 2 * G_RANK


CONV_CB = 256


def _conv_silu(ext, w9, cols, nrow):
    tot = (nrow + 2) * cols
    col = lax.broadcasted_iota(jnp.int32, ext.shape, 0) % cols
    left = jnp.where(col == 0, 0.0, pltpu.roll(ext, 1, 0))
    right = jnp.where(col == cols - 1, 0.0, pltpu.roll(ext, tot - 1, 0))
    srcs = (left, ext, right)
    acc = None
    for di in range(3):
        for dj in range(3):
            term = srcs[dj][di * cols:(di + nrow) * cols] * w9[di * 3 + dj:di * 3 + dj + 1]
            acc = term if acc is None else acc + term
    return acc * _sigmoid(acc)


def _feat_kernel(cols, halo, x_ref, *rest):
    xn_ref = rest[0] if halo else None
    (mod_ref, g_ref, wtok_ref, wfeat_ref, bgt_ref, cw_ref,
     mqk_ref, gq_ref, gk_ref, gv_ref, gr_ref, mvt_ref, mot_ref, gt_ref, prev_s) = rest[1 if halo else 0:]
    j = pl.program_id(1)
    last = pl.num_programs(1) - 1
    tm = x_ref.shape[1]
    shift = mod_ref[0, 0:1]
    gain = g_ref[...] * (1.0 + mod_ref[0, 1:2])

    def norm_mod(x):
        return (x * lax.rsqrt(jnp.mean(x * x, -1, keepdims=True) + EPS) * gain + shift).astype(BF16)

    hb = norm_mod(x_ref[0])
    hext = jnp.concatenate([hb, norm_mod(xn_ref[0])], 0) if halo else hb

    @pl.when(j == 0)
    def _():
        prev_s[...] = jnp.zeros_like(prev_s)

    outs = dict(gq=gq_ref, gk=gk_ref, gv=gv_ref, gr=gr_ref, mv=mvt_ref, mo=mot_ref)
    tasks = []

    def tok_task(name, off, c0):
        def run():
            z = _dot(hb, wtok_ref[:, off + c0:off + c0 + CONV_CB])
            if name == "gq":
                z = z * (G_DK ** -0.5)
            outs[name][0, :, c0:c0 + CONV_CB] = z.astype(BF16)
        return run

    def feat_task():
        z = _dot_nt(wfeat_ref[...], hb)
        off = 0
        for name, size in _FEAT_SIZES:
            outs[name][0] = z[off:off + size].astype(BF16)
            off += size
        gt_ref[0] = z[off:] + bgt_ref[...]

    off = 2 * M_QK
    for name, size in _TOK_SIZES[1:]:
        tasks += [tok_task(name, off, c0) for c0 in range(0, size, CONV_CB)]
        off += size
    tasks.append(feat_task)

    nblk = 2 * M_QK // CONV_CB
    for i in range(nblk):
        cs = slice(i * CONV_CB, (i + 1) * CONV_CB)
        z = _dot(hext, wtok_ref[:, cs])
        for task in tasks[i * len(tasks) // nblk:(i + 1) * len(tasks) // nblk]:
            task()
        below = jnp.where(j < last, z[tm:], 0.0) if halo else jnp.zeros((cols, CONV_CB), F32)
        ext = jnp.concatenate([prev_s[:, cs], z[:tm], below], 0)
        prev_s[:, cs] = z[tm - cols:tm]
        act = _conv_silu(ext, cw_ref[:, cs], cols, tm // cols)
        if i * CONV_CB >= M_QK:
            act = act * (M_DK ** -0.5)
        mqk_ref[0, :, cs] = act.astype(BF16)


def _features(x, mod, mod_row, consts, cols, tm):
    b, t, d = x.shape
    nt = t // tm
    per = tm // cols
    halo = nt > 1
    tok = lambda w: pl.BlockSpec((1, tm, w), lambda i, j: (i, j, 0))
    feat = lambda w: pl.BlockSpec((1, w, tm), lambda i, j: (i, 0, j))
    below = pl.BlockSpec((1, cols, d), lambda i, j: (i, jnp.minimum((j + 1) * per, t // cols - 1), 0))
    const = lambda a: pl.BlockSpec(a.shape, lambda i, j: (0,) * a.ndim)
    vec = pl.BlockSpec((1,) + mod.shape[1:], lambda i, j: (mod_row(i), 0, 0))
    out_shape = [jax.ShapeDtypeStruct((b, t, s), BF16) for _, s in _TOK_SIZES]
    out_specs = [tok(s) for _, s in _TOK_SIZES]
    out_shape += [jax.ShapeDtypeStruct((b, s, t), BF16) for _, s in _FEAT_SIZES]
    out_specs += [feat(s) for _, s in _FEAT_SIZES]
    out_shape += [jax.ShapeDtypeStruct((b, GT_ROWS, t), F32)]
    out_specs += [feat(GT_ROWS)]
    return pl.pallas_call(
        functools.partial(_feat_kernel, cols, halo),
        grid=(b, nt),
        in_specs=[tok(d)] + ([below] if halo else []) + [vec] + [const(a) for a in consts],
        out_specs=out_specs,
        out_shape=out_shape,
        scratch_shapes=[pltpu.VMEM((cols, 2 * M_QK), F32)],
        compiler_params=_params("parallel", "arbitrary"),
        name="features",
    )(*([x, x] if halo else [x]), mod, *consts)


M_AUG = M_DV + 16


def _mlstm_cumsum_matrix(reverse):
    L = M_CHUNK
    low = np.tril(np.ones((L, L), np.float32))
    tri = low.T if reverse else low
    return jnp.asarray(np.tile(np.concatenate([tri.T, np.ones((L, L), np.float32)], 1), (3, 1)), BF16)


def _mlstm_kernel(emit, nch, qf_ref, kf_ref, vf_ref, grf_ref, qb_ref, kb_ref, vb_ref, grb_ref,
                  rowf_ref, rowb_ref, c0_ref, m0_ref, *rest):
    if emit:
        hf_ref, hb_ref, cout_ref, mout_ref, c_s, m_s = rest
    else:
        cout_ref, mout_ref, c_s, m_s = rest
        hf_ref = hb_ref = None
    j = pl.program_id(1)
    L = M_CHUNK

    @pl.when(j == 0)
    def _():
        c_s[...] = c0_ref[0]
        m_s[...] = m0_ref[0]

    sid = lax.broadcasted_iota(jnp.int32, (L, L), 0)
    tid = lax.broadcasted_iota(jnp.int32, (L, L), 1)
    ones_rows = jnp.ones((M_AUG - M_DV, L), BF16)
    lane = lax.broadcasted_iota(jnp.int32, (M_HEADS, L), 1)

    refs = ((qf_ref, kf_ref, vf_ref, grf_ref, rowf_ref, hf_ref),
            (qb_ref, kb_ref, vb_ref, grb_ref, rowb_ref, hb_ref))
    masks = (sid <= tid, sid >= tid)

    def gate_stage(i):
        chains = []
        for d in range(2):
            q_ref, k_ref, v_ref, gr_ref, row_ref, h_ref = refs[d]
            c = i if d == 0 else nch - 1 - i
            tok = slice(c * L, (c + 1) * L)
            grow = gr_ref[0, :, tok]
            cum_ext = _dot(jnp.concatenate(_split3(_log_sigmoid(grow) * LOG2_E), 1), row_ref[...])
            grow = grow * LOG2_E
            i0 = 2 * M_HEADS * d
            cc_rows = grow[i0:i0 + M_HEADS] - cum_ext[i0 + M_HEADS:i0 + 2 * M_HEADS, :L]
            cmax = cc_rows
            step = 1
            while step < L:
                if d == 0:
                    moved = jnp.where(lane >= step, pltpu.roll(cmax, step, 1), NEG)
                else:
                    moved = jnp.where(lane < L - step, pltpu.roll(cmax, L - step, 1), NEG)
                cmax = jnp.maximum(cmax, moved)
                step *= 2
            cc_cols = jnp.transpose(jnp.concatenate([cc_rows, cc_rows], 0))
            for hd in range(M_HEADS):
                cf = i0 + M_HEADS + hd
                sl = slice(hd * M_DK, (hd + 1) * M_DK)
                chains.append(dict(
                    d=d, r=d * M_HEADS + hd, sl=sl, tok=tok,
                    q=q_ref[0, tok, sl], k=k_ref[0, tok, sl],
                    v_aug=jnp.concatenate([v_ref[0, sl, tok], ones_rows], 0),
                    br=cum_ext[cf:cf + 1, :L],
                    btot=cum_ext[cf:cf + 1, L:],
                    ir=grow[i0 + hd:i0 + hd + 1, :],
                    cmax=cmax[hd:hd + 1, :],
                    cc=cc_cols[:, hd:hd + 1]))
        return chains

    for chains in [gate_stage(i) for i in range(nch)]:
        for ch in chains:
            r = ch["r"]
            m = jnp.concatenate([m_s[r:r + 1, :]] * (L // LANE), 1)
            ct = c_s[r]
            ch["m"] = m
            if emit:
                ch["big"] = _dot_nt(jnp.concatenate([ch["k"], ct.astype(BF16)], 0), ch["q"])
            src = ch["btot"] + ch["ir"] - ch["br"]
            m_new = jnp.maximum(m + ch["btot"], jnp.max(src, -1, keepdims=True))
            ws = jnp.exp2(src - m_new).astype(BF16)
            wc = jnp.exp2((m + ch["btot"] - m_new)[:, :M_DK])
            c_s[r] = wc * ct + _dot(ch["v_aug"] * ws, ch["k"])
            m_s[r:r + 1, :] = m_new[:, :LANE]

        if emit:
            for ch in chains:
                m, br, big = ch["m"], ch["br"], ch["big"]
                mt = br + jnp.maximum(m, ch["cmax"])
                pt = jnp.where(masks[ch["d"]], big[:L] * jnp.exp2(ch["cc"] + (br - mt)), 0.0)
                num = jnp.exp2(m + br - mt) * big[L:] + _dot(ch["v_aug"], pt.astype(BF16))
                inv = 1.0 / jnp.maximum(jnp.abs(num[M_DV:M_DV + 1]), jnp.exp2(-mt))
                refs[ch["d"]][5][0, ch["sl"], ch["tok"]] = num[:M_DV] * inv

    @pl.when(j == pl.num_programs(1) - 1)
    def _():
        cout_ref[0] = c_s[...]
        mout_ref[0] = m_s[...]


def _mlstm_scan(mqk, mvt, gt, c0, m0, emit, tb):
    b, _, t = mvt.shape
    nb = t // tb
    consts = (_mlstm_cumsum_matrix(False), _mlstm_cumsum_matrix(True))
    fw = lambda w, cblk: pl.BlockSpec((1, tb, w), lambda i, j: (i, j, cblk))
    bw = lambda w, cblk: pl.BlockSpec((1, tb, w), lambda i, j: (i, nb - 1 - j, cblk))
    ftf = lambda w: pl.BlockSpec((1, w, tb), lambda i, j: (i, 0, j))
    ftb = lambda w: pl.BlockSpec((1, w, tb), lambda i, j: (i, 0, nb - 1 - j))
    const = lambda a: pl.BlockSpec(a.shape, lambda i, j: (0,) * a.ndim)
    cst = pl.BlockSpec((1, 2 * M_HEADS, M_AUG, M_DK), lambda i, j: (i, 0, 0, 0))
    mst = pl.BlockSpec((1, 2 * M_HEADS, LANE), lambda i, j: (i, 0, 0))
    in_specs = [fw(M_QK, 0), fw(M_QK, 1), ftf(M_WIDTH), ftf(N_GATES),
                bw(M_QK, 0), bw(M_QK, 1), ftb(M_WIDTH), ftb(N_GATES)]
    in_specs += [const(a) for a in consts] + [cst, mst]
    out_specs = [cst, mst]
    out_shape = [jax.ShapeDtypeStruct(c0.shape, F32), jax.ShapeDtypeStruct(m0.shape, F32)]
    if emit:
        out_specs = [ftf(M_WIDTH), ftb(M_WIDTH)] + out_specs
        out_shape = [jax.ShapeDtypeStruct((b, M_WIDTH, t), F32)] * 2 + out_shape
    return pl.pallas_call(
        functools.partial(_mlstm_kernel, emit, tb // M_CHUNK),
        grid=(b, nb),
        in_specs=in_specs,
        out_specs=out_specs,
        out_shape=out_shape,
        scratch_shapes=[pltpu.VMEM((2 * M_HEADS, M_AUG, M_DK), F32), pltpu.VMEM((2 * M_HEADS, LANE), F32)],
        compiler_params=_params("parallel", "arbitrary"),
        name="mlstm_scan" if emit else "mlstm_ctx_state",
    )(mqk, mqk, mvt, gt, mqk, mqk, mvt, gt, *consts, c0, m0)


G_LEVELS = (32, 16, 8, 4, 2, 1)


def _gla_select_matrices(reverse):
    L = G_CHUNK
    t = np.arange(L)
    u = np.arange(L)[None, :]
    tt = t[:, None]
    mats = [(u <= tt), (u > tt)]
    for h in G_LEVELS:
        start = (tt // (2 * h)) * (2 * h)
        second = tt >= start + h
        m = np.where(second, (u >= start + h) & (u <= tt), (u > tt) & (u < start + h))
        mats.append(m)
    m = np.concatenate([x.astype(np.float32) for x in mats], 0)
    if reverse:
        m = m.reshape(len(mats), L, L)[:, ::-1, ::-1].reshape(len(mats) * L, L)
    return jnp.asarray(np.concatenate([m] * G_SPLIT, 1), BF16)


def _gla_pair_masks(reverse):
    L = G_CHUNK
    s = np.arange(L)[:, None]
    t = np.arange(L)[None, :]
    masks = [s == t]
    for h in G_LEVELS:
        same = (s // (2 * h)) == (t // (2 * h))
        masks.append(same & ((t // h) % 2 == 1) & ((s // h) % 2 == 0))
    m = np.stack(masks).astype(np.float32)
    if reverse:
        m = m[:, ::-1, ::-1]
    return jnp.asarray(np.tile(m, (1, 1, G_HEADS)), F32)


def _gla_fast_matrices(reverse):
    L = G_CHUNK
    h = L // 2
    tt = np.arange(L)[:, None]
    u = np.arange(L)[None, :]
    mid = np.where(tt >= h, ((u >= h) & (u <= tt)).astype(np.float32), -((u > tt) & (u < h)).astype(np.float32))
    m = np.stack([(u <= tt).astype(np.float32), (u > tt).astype(np.float32), mid])
    if reverse:
        m = m[:, ::-1, ::-1]
    m = m.reshape(3 * L, L)
    return jnp.asarray(np.concatenate([m] * G_SPLIT, 1), BF16)


def _gla_causal_mask(reverse):
    s = np.arange(G_CHUNK)[:, None]
    t = np.arange(G_CHUNK)[None, :]
    m = (s >= t) if reverse else (s <= t)
    return jnp.asarray(np.tile(m.astype(np.float32), (1, G_HEADS)), F32)


def _gla_head_mask():
    r = np.arange(G_HEADS * G_CHUNK)[:, None] // G_CHUNK
    c = np.arange(G_QK)[None, :] // G_DK
    return jnp.asarray((r == c).astype(np.float32), BF16)


GLA_FAST_LIMIT = 40.0


def _gla_kernel(emit, nch, pipelined, qf_ref, kf_ref, vf_ref, lrf_ref, qb_ref, kb_ref, vb_ref, lrb_ref,
                lrnf_ref, lrnb_ref, lvf_ref, lvb_ref, fastf_ref, fastb_ref, mkf_ref, mkb_ref, cmf_ref, cmb_ref,
                hm_ref, gw_ref, gb_ref, s0_ref, *rest):
    if emit:
        of_ref, ob_ref, sout_ref, s_s, a3_s, lo_s = rest
    else:
        sout_ref, s_s, a3_s, lo_s = rest
        of_ref = ob_ref = None
    j = pl.program_id(1)
    L = G_CHUNK
    slot = j % 2

    def stage_decays(lr_refs, dst):
        lo = None
        for d, lr_ref in enumerate(lr_refs):
            lr_t = lr_ref[0].astype(BF16)
            loga = _log_sigmoid(_dot_tn(lr_t, gw_ref[d]) + gb_ref[d]) * (LOG2_E / G_TAU)
            pieces = _split(loga, G_SPLIT)
            a3_s[dst, d] = jnp.concatenate(
                [jnp.concatenate([p[c * L:(c + 1) * L] for p in pieces], 0) for c in range(nch)], 1)
            m = jnp.min(loga)
            lo = m if lo is None else jnp.minimum(lo, m)
        lo_s[dst] = lo * L

    @pl.when(j == 0)
    def _():
        s_s[...] = s0_ref[0]
        stage_decays((lrf_ref, lrb_ref), 0)

    hm = hm_ref[...]
    stack = lambda a: jnp.concatenate([a] * G_HEADS, 0)

    refs = ((qf_ref, kf_ref, vf_ref, mkf_ref, cmf_ref, of_ref), (qb_ref, kb_ref, vb_ref, mkb_ref, cmb_ref, ob_ref))

    def run(fast):
        chunks = []
        for d in range(2):
            q_ref, k_ref, v_ref, mk_ref, cm_ref, o_ref = refs[d]
            sel_ref = ((fastf_ref, fastb_ref) if fast else (lvf_ref, lvb_ref))[d]
            g = _dot(sel_ref[...], a3_s[slot, d])
            expo = jnp.exp2(g[:2 * L] if fast else g)
            for c in (range(nch) if d == 0 else reversed(range(nch))):
                rows = slice(c * L, (c + 1) * L)
                cols = slice(c * G_QK, (c + 1) * G_QK)
                chunks.append(dict(
                    d=d, rows=rows, cols=cols, g=g, expo=expo, k=k_ref[0, rows, :],
                    v_st=jnp.concatenate([v_ref[0, rows, hd * G_DV:(hd + 1) * G_DV] for hd in range(G_HEADS)], 0),
                    e_tot=expo[L - 1:L, cols] if d == 0 else expo[0:1, cols]))

        def e(ch, i):
            return ch["expo"][i * L:(i + 1) * L, ch["cols"]].astype(BF16)

        for ch in chunks:
            k = ch["k"]
            ch["kd_st"] = stack(k * e(ch, 1)) * hm
            if emit:
                q_ref, mk_ref, cm_ref = refs[ch["d"]][0], refs[ch["d"]][3], refs[ch["d"]][4]
                qs = stack(q_ref[0, ch["rows"], :]) * hm
                ch["qs"] = qs
                if fast:
                    gm = ch["g"][2 * L:3 * L, ch["cols"]]
                    att_t = _dot_nt(k * jnp.exp2(-gm).astype(BF16), qs * stack(jnp.exp2(gm).astype(BF16)))
                    att_t = jnp.where(cm_ref[...] != 0.0, att_t, 0.0)
                else:
                    att_t = mk_ref[0] * _dot_nt(k, qs)
                    for lv in range(len(G_LEVELS)):
                        ev = e(ch, 2 + lv)
                        att_t = att_t + mk_ref[1 + lv] * _dot_nt(k * ev, qs * stack(ev))
                ch["att_t"] = att_t

        if pipelined:
            stage_decays((lrnf_ref, lrnb_ref), 1 - slot)

        for ch in chunks:
            if emit:
                att_st = stack(ch["att_t"].astype(BF16)) * hm
                ch["res"] = _dot_tn(ch["v_st"], jnp.concatenate([att_st, ch["kd_st"]], 1))
            else:
                ch["res"] = _dot_tn(ch["v_st"], ch["kd_st"])

        order = [chunks[dd * nch + i] for i in range(nch) for dd in range(2)]
        for ch in order:
            d = ch["d"]
            st = s_s[d]
            res = ch["res"]
            if emit:
                inter_t = _dot_nt(st.astype(BF16), ch["qs"] * stack(e(ch, 0)))
                o = jnp.transpose(inter_t + res[:, :G_HEADS * L])
                o_ref = refs[d][5]
                for hd in range(G_HEADS):
                    o_ref[0, ch["rows"], hd * G_DV:(hd + 1) * G_DV] = o[hd * L:(hd + 1) * L]
                res = res[:, G_HEADS * L:]
            s_s[d] = ch["e_tot"] * st + res

    if emit:
        mild = lo_s[slot] >= -GLA_FAST_LIMIT
        pl.when(mild)(lambda: run(True))
        pl.when(jnp.logical_not(mild))(lambda: run(False))
    else:
        run(True)

    @pl.when(j == pl.num_programs(1) - 1)
    def _():
        sout_ref[0] = s_s[...]


def _gla_scan(gq, gk, gv, gt, g_gate_w, g_gate_b, s0, emit, tb):
    b, t, _ = gv.shape
    nb = t // tb
    consts = (_gla_select_matrices(False), _gla_select_matrices(True),
              _gla_fast_matrices(False), _gla_fast_matrices(True),
              _gla_pair_masks(False), _gla_pair_masks(True),
              _gla_causal_mask(False), _gla_causal_mask(True), _gla_head_mask(), g_gate_w, g_gate_b)
    fw = lambda w: pl.BlockSpec((1, tb, w), lambda i, j: (i, j, 0))
    bw = lambda w: pl.BlockSpec((1, tb, w), lambda i, j: (i, nb - 1 - j, 0))
    const = lambda a: pl.BlockSpec(a.shape, lambda i, j: (0,) * a.ndim)
    sst = pl.BlockSpec((1, 2, G_DV, G_QK), lambda i, j: (i, 0, 0, 0))
    assert N_GATES == G_RANK
    lr = lambda rows, at: pl.BlockSpec((1, G_RANK, tb), lambda i, j: (i, rows, at(j)))
    fw_lr, bw_lr = lr(1, lambda j: j), lr(2, lambda j: nb - 1 - j)
    fw_next = lr(1, lambda j: jnp.minimum(j + 1, nb - 1))
    bw_next = lr(2, lambda j: jnp.maximum(nb - 2 - j, 0))
    in_specs = [fw(G_QK), fw(G_QK), fw(G_WIDTH), fw_lr,
                bw(G_QK), bw(G_QK), bw(G_WIDTH), bw_lr, fw_next, bw_next]
    in_specs += [const(a) for a in consts] + [sst]
    out_specs = [sst]
    out_shape = [jax.ShapeDtypeStruct(s0.shape, F32)]
    if emit:
        out_specs = [fw(G_WIDTH), bw(G_WIDTH)] + out_specs
        out_shape = [jax.ShapeDtypeStruct((b, t, G_WIDTH), F32)] * 2 + out_shape
    nch = tb // G_CHUNK
    return pl.pallas_call(
        functools.partial(_gla_kernel, emit, nch, nb > 1),
        grid=(b, nb),
        in_specs=in_specs,
        out_specs=out_specs,
        out_shape=out_shape,
        scratch_shapes=[pltpu.VMEM((2, G_DV, G_QK), F32),
                        pltpu.VMEM((2, 2, G_SPLIT * G_CHUNK, nch * G_QK), BF16),
                        pltpu.SMEM((2,), F32)],
        compiler_params=_params("parallel", "arbitrary"),
        name="gla_scan" if emit else "gla_ctx_state",
    )(gq, gk, gv, gt, gq, gk, gv, gt, gt, gt, *consts, s0)


FFN_CHUNK = 256


def _head_norm(h, nheads, width):
    parts = []
    for hd in range(nheads):
        blk = h[:, hd * width:(hd + 1) * width]
        parts.append(blk * lax.rsqrt(jnp.mean(blk * blk, -1, keepdims=True) + EPS))
    return jnp.concatenate(parts, 1)


def _out_kernel(d_ff, x_ref, hf_ref, hb_ref, of_ref, ob_ref, mo_ref, gr_ref, mod_ref,
                mng_ref, gng_ref, gffn_ref, gfin_ref, wout_ref, wgu_ref, wdown_ref, o_ref):
    x = x_ref[0]
    mod = mod_ref[0][2:6]
    hm_t = hf_ref[0] + hb_ref[0]
    parts = []
    for hd in range(M_HEADS):
        blk = hm_t[hd * M_DV:(hd + 1) * M_DV]
        parts.append(blk * lax.rsqrt(jnp.mean(blk * blk, 0, keepdims=True) + EPS))
    hm_t = jnp.concatenate(parts, 0) * mng_ref[...] * _sigmoid(mo_ref[0].astype(F32))
    gr = gr_ref[0].astype(F32)
    hg = _head_norm(of_ref[0] + ob_ref[0], G_HEADS, G_DV) * gng_ref[...] * (gr * _sigmoid(gr))
    mixed = _dot_tn(hm_t.astype(BF16), wout_ref[:M_WIDTH, :]) + _dot(hg.astype(BF16), wout_ref[M_WIDTH:, :])
    x1 = x + mod[0:1] * mixed
    gain2 = gffn_ref[...] * (1.0 + mod[2:3])
    h2 = (x1 * lax.rsqrt(jnp.mean(x1 * x1, -1, keepdims=True) + EPS) * gain2 + mod[1:2]).astype(BF16)
    def gated(c0):
        a = _dot(h2, wgu_ref[:, c0:c0 + FFN_CHUNK])
        g = _dot(h2, wgu_ref[:, d_ff + c0:d_ff + c0 + FFN_CHUNK])
        return (a * _sigmoid(a) * g).astype(BF16)

    act = jnp.concatenate([gated(c0) for c0 in range(0, d_ff, FFN_CHUNK)], 1)
    acc = _dot(act, wdown_ref[...])
    x2 = x1 + mod[3:4] * acc
    o_ref[0] = x2 * lax.rsqrt(jnp.mean(x2 * x2, -1, keepdims=True) + EPS) * gfin_ref[...]


def _out_block(x, hf, hb, of, ob, mo, gr, mod4, mng, gng, gffn, gfin, wout, wgu, wdown, tm):
    b, t, d = x.shape
    d_ff = wdown.shape[0]
    tok = lambda w: pl.BlockSpec((1, tm, w), lambda i, j: (i, j, 0))
    feat = lambda w: pl.BlockSpec((1, w, tm), lambda i, j: (i, 0, j))
    const = lambda a: pl.BlockSpec(a.shape, lambda i, j: (0,) * a.ndim, pipeline_mode=pl.Buffered(1))
    return pl.pallas_call(
        functools.partial(_out_kernel, d_ff),
        grid=(b, t // tm),
        in_specs=[tok(d), feat(M_WIDTH), feat(M_WIDTH), tok(G_WIDTH), tok(G_WIDTH), feat(M_WIDTH), tok(G_WIDTH),
                  pl.BlockSpec((1,) + mod4.shape[1:], lambda i, j: (i, 0, 0)),
                  const(mng), const(gng), const(gffn), const(gfin), const(wout), const(wgu), const(wdown)],
        out_specs=tok(d),
        out_shape=jax.ShapeDtypeStruct((b, t, d), F32),
        compiler_params=_params("parallel", "arbitrary"),
        name="out_ffn",
    )(x, hf, hb, of, ob, mo, gr, mod4, mng, gng, gffn, gfin, wout, wgu, wdown)


def _largest_tile(t, cap):
    tm = min(t, cap)
    while t % tm:
        tm //= 2
    return tm


def kernel(x, c, ctx, c_ctx, w_ada, b_ada, g_mix, w_in, conv_w, m_gate_b, m_norm_g, g_gate_w, g_gate_b,
           g_norm_g, w_out, g_ffn, w_gu, w_down, g_final):
    b, t, d = x.shape
    tc = ctx.shape[1]
    assert w_ada.shape[0] == 1, "single layer block"
    assert t % (GRID_W * 2) == 0 and tc % M_CHUNK == 0

    pad = (-(b + 1)) % 8
    c_all = jnp.concatenate([c, c_ctx[None, :], jnp.zeros((pad, d), F32)], 0)
    mod = _modulation(c_all, w_ada[0], b_ada[0]).reshape(b + 1 + pad, 6, d)

    w = w_in[0]
    sizes = (2 * M_QK, M_WIDTH, M_WIDTH, N_GATES, G_QK, G_QK, G_WIDTH, G_WIDTH, 2 * G_RANK)
    offs = np.concatenate([[0], np.cumsum(sizes)])
    seg = lambda i: w[:, offs[i]:offs[i + 1]]
    wtok = jnp.concatenate([seg(0), seg(4), seg(5), seg(6), seg(7)], 1).astype(BF16)
    wfeat = jnp.transpose(jnp.concatenate([seg(1), seg(2), seg(3), seg(8)], 1)).astype(BF16)
    bgt = jnp.concatenate([m_gate_b[0], jnp.zeros((2 * G_RANK,), F32)]).reshape(GT_ROWS, 1)
    g_mix2 = g_mix[0].reshape(1, d)
    conv_w9 = conv_w[0].reshape(9, 2 * M_QK)
    ggw = g_gate_w[0].astype(BF16)
    ggb = g_gate_b[0].reshape(2, 1, G_QK)

    feat_consts = (g_mix2, wtok, wfeat, bgt, conv_w9)

    def features(inp, mod_row, cols, tm):
        mqk, gq, gk, gv, gr, mvt, mot, gt = _features(inp, mod, mod_row, feat_consts, cols, tm)
        return dict(mqk=mqk, gq=gq, gk=gk, gv=gv, gr=gr, mvt=mvt, mot=mot, gt=gt)

    fc = features(ctx, lambda i: b, tc, tc)
    c0 = jnp.zeros((b, 2 * M_HEADS, M_AUG, M_DK), F32)
    m0 = jnp.full((b, 2 * M_HEADS, LANE), NEG, F32)
    c_ctx_state, m_ctx_state = _mlstm_scan(fc["mqk"], fc["mvt"], fc["gt"], c0, m0, False,
                                           _largest_tile(tc, M_CHUNK))
    s0 = jnp.zeros((b, 2, G_DV, G_QK), F32)
    (s_ctx_state,) = _gla_scan(fc["gq"], fc["gk"], fc["gv"], fc["gt"], ggw, ggb, s0, False,
                               _largest_tile(tc, 256))

    fx = features(x, lambda i: i, GRID_W, _largest_tile(t, 1024))
    hf_t, hb_t, _, _ = _mlstm_scan(fx["mqk"], fx["mvt"], fx["gt"], c_ctx_state, m_ctx_state, True,
                                   _largest_tile(t, 1024))
    of, ob, _ = _gla_scan(fx["gq"], fx["gk"], fx["gv"], fx["gt"], ggw, ggb, s_ctx_state, True,
                          _largest_tile(t, 512))

    tm = _largest_tile(t, 512)
    mng = jnp.broadcast_to(m_norm_g[0].reshape(M_WIDTH, 1), (M_WIDTH, tm))
    return _out_block(x, hf_t, hb_t, of, ob, fx["mot"], fx["gr"], mod,
                      mng, g_norm_g[0].reshape(1, -1),
                      g_ffn[0].reshape(1, d), g_final.reshape(1, d),
                      w_out[0].astype(BF16), w_gu[0].astype(BF16), w_down[0].astype(BF16), tm)
```
